```python
import jax, jax.numpy as jnp
from jax import lax
import numpy as np

D_MODEL = 1024
BATCH = 8
SEQ = 2048
DEPTH = 1

CHUNK = 64
CONV_DIM = D_MODEL
CONV_WIDTH = 31
GLA_HEADS = 4
GLA_DK = D_MODEL // 2
GLA_DV = D_MODEL
GLA_HK = GLA_DK // GLA_HEADS
GLA_HV = GLA_DV // GLA_HEADS
GATE_RANK = 16
GATE_TAU = 16.0
N_GROUPS = 8
EXPERTS_PER_GROUP = 8
N_EXPERTS = N_GROUPS * EXPERTS_PER_GROUP
TOP_K = 2
D_EXPERT = D_MODEL // 2
MOE_BLOCK = 128
LN_EPS = 1e-5
RMS_EPS = 1e-6
DEEPNORM_ALPHA = (2.0 * DEPTH) ** 0.25
DEEPNORM_BETA = (8.0 * DEPTH) ** -0.25
IN_SPLITS = (CONV_DIM, CONV_DIM, GLA_DK, GLA_DK, GLA_DV, GLA_DV, GATE_RANK, D_MODEL, D_MODEL)
D_IN_PROJ = sum(IN_SPLITS)

kernel_name = "hybrid_conv_gla_hmoe_deepnorm"


def layer_norm(x, g, b):
    xf = x.astype(jnp.float32)
    mu = jnp.mean(xf, axis=-1, keepdims=True)
    var = jnp.mean(jnp.square(xf - mu), axis=-1, keepdims=True)
    return ((xf - mu) * lax.rsqrt(var + LN_EPS) * g + b).astype(x.dtype)


def conv_module(a, g, conv_w, conv_b, ln_g, ln_b, w_co, b_co):
    u = a * jax.nn.sigmoid(g)
    u = lax.conv_general_dilated(
        u, conv_w[:, None, :].astype(u.dtype), window_strides=(1,),
        padding=[(CONV_WIDTH - 1, 0)],
        dimension_numbers=("NWC", "WIO", "NWC"),
        feature_group_count=CONV_DIM) + conv_b
    u = jax.nn.silu(layer_norm(u, ln_g, ln_b))
    return u @ w_co + b_co


def gla_branch(q, k, v, r, f_lr, w_gate_up, b_gate_up, norm_g, w_o):
    bsz, seq, _ = q.shape
    n_chunks = seq // CHUNK
    dt = q.dtype
    f32 = jnp.float32
    log_a = jax.nn.log_sigmoid((f_lr @ w_gate_up + b_gate_up).astype(f32)) / GATE_TAU

    def chunked(t, hd):
        return t.astype(f32).reshape(bsz, n_chunks, CHUNK, GLA_HEADS, hd)

    qc = chunked(q, GLA_HK) * (GLA_HK ** -0.5)
    kc = chunked(k, GLA_HK)
    vc = chunked(v, GLA_HV)
    cum = jnp.cumsum(chunked(log_a, GLA_HK), axis=2)
    cum_end = cum[:, :, -1:]
    k_dec = kc * jnp.exp(cum_end - cum)
    chunk_kv = jnp.einsum("bnchk,bnchv->nbhkv", k_dec, vc)
    chunk_decay = jnp.exp(cum_end[:, :, 0]).transpose(1, 0, 2, 3)

    def step(state, inp):
        dec, kv = inp
        state = dec[..., None] * state + kv
        return state, state

    s0 = jnp.zeros((bsz, GLA_HEADS, GLA_HK, GLA_HV), f32)
    _, states = lax.scan(step, s0, (chunk_decay, chunk_kv))
    o = jnp.einsum("bnchk,nbhkv->bnchv", qc, states).reshape(bsz, seq, GLA_HEADS, GLA_HV)
    o = o * lax.rsqrt(jnp.mean(jnp.square(o), axis=-1, keepdims=True) + RMS_EPS) * norm_g
    o = o * jax.nn.silu(r.astype(f32)).reshape(bsz, seq, GLA_HEADS, GLA_HV)
    return o.reshape(bsz, seq, GLA_DV).astype(dt) @ w_o


def hier_moe(x, w_rg, b_rg, w_re, b_re, w1, w3, w2):
    bsz, seq, d = x.shape
    n_tok = bsz * seq
    xt = x.reshape(n_tok, d)
    f32 = jnp.float32
    g_logits = (xt @ w_rg + b_rg).astype(f32)
    g_prob = jax.nn.softmax(g_logits, axis=-1)
    _, g_sel = lax.top_k(g_logits, 1)
    g_w = jnp.take_along_axis(g_prob, g_sel, axis=1)
    e_all = (xt @ w_re + b_re).astype(f32).reshape(n_tok, N_GROUPS, EXPERTS_PER_GROUP)
    e_logits = jnp.take_along_axis(e_all, g_sel[:, :, None], axis=1)[:, 0]
    top_v, top_i = lax.top_k(e_logits, TOP_K)
    weights = g_w * jax.nn.softmax(top_v, axis=-1)
    expert = g_sel * EXPERTS_PER_GROUP + top_i

    n_assign = n_tok * TOP_K
    flat_e = expert.reshape(-1)
    flat_tok = jnp.repeat(jnp.arange(n_tok, dtype=jnp.int32), TOP_K)
    flat_w = weights.reshape(-1)
    order = jnp.argsort(flat_e)
    se, st, sw = flat_e[order], flat_tok[order], flat_w[order]
    counts = jnp.bincount(flat_e, length=N_EXPERTS)
    starts = jnp.cumsum(counts) - counts
    pcounts = (counts + MOE_BLOCK - 1) // MOE_BLOCK * MOE_BLOCK
    pends = jnp.cumsum(pcounts)
    pstarts = pends - pcounts
    dest = pstarts[se] + (jnp.arange(n_assign) - starts[se])
    n_blocks = -(-n_assign // MOE_BLOCK) + N_EXPERTS
    n_slots = n_blocks * MOE_BLOCK
    slot_tok = jnp.zeros((n_slots,), jnp.int32).at[dest].set(st)
    slot_w = jnp.zeros((n_slots,), x.dtype).at[dest].set(sw.astype(x.dtype))
    block_e = jnp.minimum(
        jnp.searchsorted(pends, jnp.arange(n_blocks) * MOE_BLOCK, side="right"), N_EXPERTS - 1)
    xs = xt[slot_tok].reshape(n_blocks, MOE_BLOCK, d)

    def expert_block(args):
        xb, e = args
        hdn = jax.nn.silu(xb @ w1[e]) * (xb @ w3[e])
        return hdn @ w2[e]

    ys = lax.map(expert_block, (xs, block_e)).reshape(n_slots, d) * slot_w[:, None]
    out = jnp.zeros((n_tok, d), x.dtype).at[slot_tok].add(ys)
    return out.reshape(bsz, seq, d)


def setup_inputs(seed: int = 0) -> dict:
    key = jax.random.key(seed)
    ks = jax.random.split(key, 26)
    L = DEPTH

    def nrm(k, shape, scale):
        return jax.random.normal(k, shape, jnp.float32) * scale

    return {
        "x": nrm(ks[0], (BATCH, SEQ, D_MODEL), 1.0),
        "w_in": nrm(ks[1], (L, D_MODEL, D_IN_PROJ), D_MODEL ** -0.5),
        "b_in": nrm(ks[2], (L, D_IN_PROJ), 0.02),
        "conv_w": nrm(ks[3], (L, CONV_WIDTH, CONV_DIM), CONV_WIDTH ** -0.5),
        "conv_b": nrm(ks[4], (L, CONV_DIM), 0.02),
        "conv_ln_g": 1.0 + nrm(ks[5], (L, CONV_DIM), 0.05),
        "conv_ln_b": nrm(ks[6], (L, CONV_DIM), 0.02),
        "w_conv_out": nrm(ks[7], (L, CONV_DIM, D_MODEL), CONV_DIM ** -0.5),
        "b_conv_out": nrm(ks[8], (L, D_MODEL), 0.02),
        "w_gate_up": nrm(ks[9], (L, GATE_RANK, GLA_DK), GATE_RANK ** -0.5),
        "b_gate_up": nrm(ks[10], (L, GLA_DK), 0.1),
        "gla_norm_g": 1.0 + nrm(ks[11], (L, GLA_HEADS, GLA_HV), 0.05),
        "w_gla_out": nrm(ks[12], (L, GLA_DV, D_MODEL), GLA_DV ** -0.5),
        "w_out": nrm(ks[13], (L, D_MODEL, D_MODEL), D_MODEL ** -0.5 * DEEPNORM_BETA),
        "b_out": nrm(ks[14], (L, D_MODEL), 0.02),
        "ln1_g": 1.0 + nrm(ks[15], (L, D_MODEL), 0.05),
        "ln1_b": nrm(ks[16], (L, D_MODEL), 0.02),
        "w_router_group": nrm(ks[17], (L, D_MODEL, N_GROUPS), D_MODEL ** -0.5),
        "b_router_group": nrm(ks[18], (L, N_GROUPS), 0.01),
        "w_router_expert": nrm(ks[19], (L, D_MODEL, N_EXPERTS), D_MODEL ** -0.5),
        "b_router_expert": nrm(ks[20], (L, N_EXPERTS), 0.01),
        "w1": nrm(ks[21], (L, N_EXPERTS, D_MODEL, D_EXPERT), D_MODEL ** -0.5),
        "w3": nrm(ks[22], (L, N_EXPERTS, D_MODEL, D_EXPERT), D_MODEL ** -0.5),
        "w2": nrm(ks[23], (L, N_EXPERTS, D_EXPERT, D_MODEL), D_EXPERT ** -0.5 * DEEPNORM_BETA),
        "ln2_g": 1.0 + nrm(ks[24], (L, D_MODEL), 0.05),
        "ln2_b": nrm(ks[25], (L, D_MODEL), 0.02),
    }


def reference(x, w_in, b_in, conv_w, conv_b, conv_ln_g, conv_ln_b, w_conv_out, b_conv_out,
              w_gate_up, b_gate_up, gla_norm_g, w_gla_out, w_out, b_out, ln1_g, ln1_b,
              w_router_group, b_router_group, w_router_expert, b_router_expert,
              w1, w3, w2, ln2_g, ln2_b):
    split_points = np.cumsum(IN_SPLITS)[:-1].tolist()
    for l in range(DEPTH):
        h = x @ w_in[l] + b_in[l]
        conv_a, conv_g, q, k, v, r, f_lr, gate_a, gate_b = jnp.split(h, split_points, axis=-1)
        y_conv = conv_module(conv_a, conv_g, conv_w[l], conv_b[l], conv_ln_g[l], conv_ln_b[l],
                             w_conv_out[l], b_conv_out[l])
        y_gla = gla_branch(q, k, v, r, f_lr, w_gate_up[l], b_gate_up[l], gla_norm_g[l], w_gla_out[l])
        merged = jax.nn.sigmoid(gate_a) * y_conv + jax.nn.sigmoid(gate_b) * y_gla
        mix = merged @ w_out[l] + b_out[l]
        x = layer_norm(DEEPNORM_ALPHA * x + mix, ln1_g[l], ln1_b[l])
        ffn = hier_moe(x, w_router_group[l], b_router_group[l], w_router_expert[l],
                       b_router_expert[l], w1[l], w3[l], w2[l])
        x = layer_norm(DEEPNORM_ALPHA * x + ffn, ln2_g[l], ln2_b[l])
    return x
```

```python
import functools

import jax
import jax.numpy as jnp
from jax import lax
from jax.experimental import pallas as pl
from jax.experimental.pallas import tpu as pltpu

F32 = jnp.float32
BF16 = jnp.bfloat16

D_MODEL = 1024
BATCH = 8
SEQ = 2048
N_TOK = BATCH * SEQ
CHUNK = 64
CONV_WIDTH = 31
GLA_HEADS = 4
GLA_DK = 512
GLA_DV = 1024
GLA_HK = 128
GLA_HV = 256
GATE_RANK = 16
GATE_TAU = 16.0
N_GROUPS = 8
EXPERTS_PER_GROUP = 8
N_EXPERTS = 64
TOP_K = 2
D_EXPERT = 512
LN_EPS = 1e-5
RMS_EPS = 1e-6
DEEPNORM_ALPHA = 2.0 ** 0.25

LANE = 128
F_PAD = LANE
N_PROJ = 7 * D_MODEL + F_PAD
TM_PROJ = 512
TS_CONV = 256
CONV_HALO = 32
CONV_ROWS = 32
TS_GLA = 512
TM_MIX = 512
BM = 256
N_BLOCKS = N_TOK * TOP_K // BM + N_EXPERTS
N_SLOTS = N_BLOCKS * BM
N_ASSIGN = N_TOK * TOP_K
TC_COMB = 256
SCATTER_CHUNK = 512
VMEM_LIMIT = 56 * 1024 * 1024


def _sigmoid(x):
    return 1.0 / (1.0 + jnp.exp(-x))


def _layer_norm(z, g, b):
    mu = jnp.mean(z, axis=-1, keepdims=True)
    zc = z - mu
    var = jnp.mean(zc * zc, axis=-1, keepdims=True)
    return zc * lax.rsqrt(var + LN_EPS) * g + b


def _inproj_kernel(x_ref, w_ref, b_ref, wgu_ref, bgu_ref,
                   u_ref, q_ref, k_ref, v_ref, rs_ref, sga_ref, sgb_ref, la_ref):
    xb = x_ref[...].astype(BF16)
    half = D_MODEL // 2

    def seg(c0, n):
        return (jnp.dot(xb, w_ref[:, c0:c0 + n], preferred_element_type=F32)
                + b_ref[:, c0:c0 + n])

    for j in range(2):
        a = seg(j * half, half)
        g = seg(D_MODEL + j * half, half)
        u_ref[:, j * half:(j + 1) * half] = (a * _sigmoid(g)).astype(BF16)
    q_ref[...] = (seg(2 * D_MODEL, GLA_DK) * (GLA_HK ** -0.5)).astype(BF16)
    k_ref[...] = seg(2 * D_MODEL + GLA_DK, GLA_DK).astype(BF16)
    for j in range(2):
        v_ref[:, j * half:(j + 1) * half] = seg(3 * D_MODEL + j * half, half).astype(BF16)
    for j in range(2):
        r = seg(4 * D_MODEL + j * half, half)
        rs_ref[:, j * half:(j + 1) * half] = (r * _sigmoid(r)).astype(BF16)
    for j in range(2):
        sga_ref[:, j * half:(j + 1) * half] = _sigmoid(
            seg(5 * D_MODEL + j * half, half)).astype(BF16)
    for j in range(2):
        sgb_ref[:, j * half:(j + 1) * half] = _sigmoid(
            seg(6 * D_MODEL + j * half, half)).astype(BF16)
    f = seg(7 * D_MODEL, F_PAD)
    z = jnp.dot(f.astype(BF16), wgu_ref[...], preferred_element_type=F32) + bgu_ref[...]
    la_ref[...] = (jnp.minimum(z, 0.0) - jnp.log(1.0 + jnp.exp(-jnp.abs(z)))) * (1.0 / GATE_TAU)


def _inproj(x2, w_p, b_p, wgu_p, bgu):
    tm = TM_PROJ
    row = lambda i: (i, 0)
    fixed = lambda i: (0, 0)
    tok = lambda n, dt: jax.ShapeDtypeStruct((N_TOK, n), dt)
    return pl.pallas_call(
        _inproj_kernel,
        grid=(N_TOK // tm,),
        in_specs=[
            pl.BlockSpec((tm, D_MODEL), row),
            pl.BlockSpec((D_MODEL, N_PROJ), fixed, pipeline_mode=pl.Buffered(1)),
            pl.BlockSpec((1, N_PROJ), fixed),
            pl.BlockSpec((F_PAD, GLA_DK), fixed),
            pl.BlockSpec((1, GLA_DK), fixed),
        ],
        out_specs=[
            pl.BlockSpec((tm, D_MODEL), row),
            pl.BlockSpec((tm, GLA_DK), row),
            pl.BlockSpec((tm, GLA_DK), row),
            pl.BlockSpec((tm, GLA_DV), row),
            pl.BlockSpec((tm, GLA_DV), row),
            pl.BlockSpec((tm, D_MODEL), row),
            pl.BlockSpec((tm, D_MODEL), row),
            pl.BlockSpec((tm, GLA_DK), row),
        ],
        out_shape=[
            tok(D_MODEL, BF16), tok(GLA_DK, BF16), tok(GLA_DK, BF16), tok(GLA_DV, BF16),
            tok(GLA_DV, BF16), tok(D_MODEL, BF16), tok(D_MODEL, BF16), tok(GLA_DK, F32),
        ],
        compiler_params=pltpu.CompilerParams(
            dimension_semantics=("arbitrary",), vmem_limit_bytes=VMEM_LIMIT),
        name="inproj",
    )(x2, w_p, b_p, wgu_p, bgu)


def _conv_kernel(u_ref, tail_ref, cw_ref, cb_ref, g_ref, b_ref, c_ref, ext_ref):
    s = pl.program_id(1)
    tail = tail_ref[...].astype(F32)
    ext_ref[0:CONV_HALO, :] = jnp.where(s == 0, 0.0, tail)
    ext_ref[CONV_HALO:CONV_HALO + TS_CONV, :] = u_ref[...].astype(F32)
    off = CONV_HALO - (CONV_WIDTH - 1)
    for i in range(TS_CONV // CONV_ROWS):
        r0 = i * CONV_ROWS
        acc = jnp.zeros((CONV_ROWS, D_MODEL), F32)
        for j in range(CONV_WIDTH):
            acc = acc + ext_ref[r0 + off + j:r0 + off + j + CONV_ROWS, :] * cw_ref[j:j + 1, :]
        y = _layer_norm(acc + cb_ref[...], g_ref[...], b_ref[...])
        c_ref[r0:r0 + CONV_ROWS, :] = (y * _sigmoid(y)).astype(BF16)


def _conv(u, cw_p, cb, g, b):
    ts = TS_CONV
    n_s = SEQ // ts
    per_tile = ts // CONV_HALO
    fixed = lambda bi, si: (0, 0)
    return pl.pallas_call(
        _conv_kernel,
        grid=(BATCH, n_s),
        in_specs=[
            pl.BlockSpec((ts, D_MODEL), lambda bi, si: (bi * n_s + si, 0)),
            pl.BlockSpec((CONV_HALO, D_MODEL),
                         lambda bi, si: (jnp.maximum((bi * n_s + si) * per_tile - 1, 0), 0)),
            pl.BlockSpec((CONV_HALO, D_MODEL), fixed),
            pl.BlockSpec((1, D_MODEL), fixed),
            pl.BlockSpec((1, D_MODEL), fixed),
            pl.BlockSpec((1, D_MODEL), fixed),
        ],
        out_specs=pl.BlockSpec((ts, D_MODEL), lambda bi, si: (bi * n_s + si, 0)),
        out_shape=jax.ShapeDtypeStruct((N_TOK, D_MODEL), BF16),
        scratch_shapes=[pltpu.VMEM((CONV_HALO + ts, D_MODEL), F32)],
        compiler_params=pltpu.CompilerParams(
            dimension_semantics=("arbitrary", "arbitrary"), vmem_limit_bytes=VMEM_LIMIT),
        name="conv",
    )(u, u, cw_p, cb, g, b)


def _gla_kernel(q_ref, k_ref, v_ref, la_ref, rs_ref, ng_ref, o_ref, st_ref):
    s = pl.program_id(1)

    @pl.when(s == 0)
    def _():
        st_ref[...] = jnp.zeros_like(st_ref)

    rr = lax.broadcasted_iota(jnp.int32, (CHUNK, CHUNK), 0)
    cc = lax.broadcasted_iota(jnp.int32, (CHUNK, CHUNK), 1)
    tri = jnp.where(rr >= cc, 1.0, 0.0).astype(BF16)

    def chunk(c, carry):
        r0 = pl.multiple_of(c * CHUNK, CHUNK)
        rows = pl.ds(r0, CHUNK)
        la = la_ref[rows, :]
        hi = la.astype(BF16)
        lo = (la - hi.astype(F32)).astype(BF16)
        cum = (jnp.dot(tri, hi, preferred_element_type=F32)
               + jnp.dot(tri, lo, preferred_element_type=F32))
        cend = cum[CHUNK - 1:CHUNK, :]
        kd = (k_ref[rows, :].astype(F32) * jnp.exp(cend - cum)).astype(BF16)
        dec = jnp.exp(cend)
        qc = q_ref[rows, :]
        vc = v_ref[rows, :]
        for h in range(GLA_HEADS):
            ks = slice(h * GLA_HK, (h + 1) * GLA_HK)
            vs = slice(h * GLA_HV, (h + 1) * GLA_HV)
            kv_t = lax.dot_general(vc[:, vs], kd[:, ks], (((0,), (0,)), ((), ())),
                                   preferred_element_type=F32)
            st = st_ref[h] * dec[:, ks] + kv_t
            st_ref[h] = st
            o = lax.dot_general(qc[:, ks], st.astype(BF16), (((1,), (1,)), ((), ())),
                                preferred_element_type=F32)
            ms = jnp.mean(o * o, axis=-1, keepdims=True)
            on = o * lax.rsqrt(ms + RMS_EPS) * ng_ref[:, vs]
            o_ref[rows, vs] = (on * rs_ref[rows, vs].astype(F32)).astype(BF16)
        return carry

    lax.fori_loop(0, TS_GLA // CHUNK, chunk, 0)


def _gla(q, k, v, la, rs, ng):
    ts = TS_GLA
    n_s = SEQ // ts
    row = lambda bi, si: (bi * n_s + si, 0)
    return pl.pallas_call(
        _gla_kernel,
        grid=(BATCH, n_s),
        in_specs=[
            pl.BlockSpec((ts, GLA_DK), row),
            pl.BlockSpec((ts, GLA_DK), row),
            pl.BlockSpec((ts, GLA_DV), row),
            pl.BlockSpec((ts, GLA_DK), row),
            pl.BlockSpec((ts, GLA_DV), row),
            pl.BlockSpec((1, GLA_DV), lambda bi, si: (0, 0)),
        ],
        out_specs=pl.BlockSpec((ts, GLA_DV), row),
        out_shape=jax.ShapeDtypeStruct((N_TOK, GLA_DV), BF16),
        scratch_shapes=[pltpu.VMEM((GLA_HEADS, GLA_HV, GLA_HK), F32)],
        compiler_params=pltpu.CompilerParams(
            dimension_semantics=("arbitrary", "arbitrary"), vmem_limit_bytes=VMEM_LIMIT),
        name="gla",
    )(q, k, v, la, rs, ng)


def _mix_kernel(c_ref, g_ref, sga_ref, sgb_ref, x_ref, wco_ref, bco_ref, wgl_ref, wo_ref, bo_ref,
                l1g_ref, l1b_ref, wrh_ref, wrl_ref, br_ref, tri_ref,
                x1_ref, route_ref, cnt_ref, carry_ref):
    i = pl.program_id(0)

    @pl.when(i == 0)
    def _():
        carry_ref[...] = jnp.zeros_like(carry_ref)

    yc = jnp.dot(c_ref[...], wco_ref[...], preferred_element_type=F32) + bco_ref[...]
    yg = jnp.dot(g_ref[...], wgl_ref[...], preferred_element_type=F32)
    merged = sga_ref[...].astype(F32) * yc + sgb_ref[...].astype(F32) * yg
    mix = jnp.dot(merged.astype(BF16), wo_ref[...], preferred_element_type=F32) + bo_ref[...]
    x1 = _layer_norm(DEEPNORM_ALPHA * x_ref[...] + mix, l1g_ref[...], l1b_ref[...])
    x1_ref[...] = x1

    xh = x1.astype(BF16)
    xl = (x1 - xh.astype(F32)).astype(BF16)
    lg = (jnp.dot(xh, wrh_ref[...], preferred_element_type=F32)
          + jnp.dot(xl, wrh_ref[...], preferred_element_type=F32)
          + jnp.dot(xh, wrl_ref[...], preferred_element_type=F32)) + br_ref[...]

    tm = lg.shape[0]
    lane = lax.broadcasted_iota(jnp.int32, (tm, LANE), 1)
    neg = -jnp.inf
    big = jnp.int32(1 << 20)
    is_g = (lane >= N_EXPERTS) & (lane < N_EXPERTS + N_GROUPS)
    gl = jnp.where(is_g, lg, neg)
    gmax = jnp.max(gl, axis=-1, keepdims=True)
    gsel = jnp.min(jnp.where(gl == gmax, lane, big), axis=-1, keepdims=True) - N_EXPERTS
    gw = 1.0 / jnp.sum(jnp.exp(gl - gmax), axis=-1, keepdims=True)
    in_grp = (lane < N_EXPERTS) & ((lane >> 3) == gsel)
    el = jnp.where(in_grp, lg, neg)
    m1 = jnp.max(el, axis=-1, keepdims=True)
    i1 = jnp.min(jnp.where(el == m1, lane, big), axis=-1, keepdims=True)
    el2 = jnp.where(lane == i1, neg, el)
    m2 = jnp.max(el2, axis=-1, keepdims=True)
    i2 = jnp.min(jnp.where(el2 == m2, lane, big), axis=-1, keepdims=True)
    t = jnp.exp(m2 - m1)
    w1 = gw / (1.0 + t)
    w2 = gw * t / (1.0 + t)

    oh1 = lane == i1
    oh2 = lane == i2
    osum = jnp.where(oh1 | oh2, 1.0, 0.0)
    excl = jnp.dot(tri_ref[...], osum.astype(BF16), preferred_element_type=F32) + carry_ref[...]
    rank1 = jnp.sum(jnp.where(oh1, excl, 0.0), axis=-1, keepdims=True)
    rank2 = jnp.sum(jnp.where(oh2, excl, 0.0), axis=-1, keepdims=True)
    new_cnt = carry_ref[...] + jnp.sum(osum, axis=0, keepdims=True)
    carry_ref[...] = new_cnt
    cnt_ref[...] = new_cnt

    route = jnp.where(lane == 0, i1.astype(F32), 0.0)
    route = jnp.where(lane == 1, i2.astype(F32), route)
    route = jnp.where(lane == 2, w1, route)
    route = jnp.where(lane == 3, w2, route)
    route = jnp.where(lane == 4, rank1, route)
    route = jnp.where(lane == 5, rank2, route)
    route_ref[...] = route


def _mix(c, g, sga, sgb, x2, wco, bco, wgl, wo, bo, l1g, l1b, wrh, wrl, br, tri):
    tm = TM_MIX
    row = lambda i: (i, 0)
    fixed = lambda i: (0, 0)
    mat = pl.BlockSpec((D_MODEL, D_MODEL), fixed, pipeline_mode=pl.Buffered(1))
    vec = pl.BlockSpec((1, D_MODEL), fixed)
    return pl.pallas_call(
        _mix_kernel,
        grid=(N_TOK // tm,),
        in_specs=[
            pl.BlockSpec((tm, D_MODEL), row), pl.BlockSpec((tm, D_MODEL), row),
            pl.BlockSpec((tm, D_MODEL), row), pl.BlockSpec((tm, D_MODEL), row),
            pl.BlockSpec((tm, D_MODEL), row),
            mat, vec, mat, mat, vec, vec, vec,
            pl.BlockSpec((D_MODEL, LANE), fixed), pl.BlockSpec((D_MODEL, LANE), fixed),
            pl.BlockSpec((1, LANE), fixed),
            pl.BlockSpec((tm, tm), fixed),
        ],
        out_specs=[
            pl.BlockSpec((tm, D_MODEL), row),
            pl.BlockSpec((tm, LANE), row),
            pl.BlockSpec((1, LANE), fixed),
        ],
        out_shape=[
            jax.ShapeDtypeStruct((N_TOK, D_MODEL), F32),
            jax.ShapeDtypeStruct((N_TOK, LANE), F32),
            jax.ShapeDtypeStruct((1, LANE), F32),
        ],
        scratch_shapes=[pltpu.VMEM((1, LANE), F32)],
        compiler_params=pltpu.CompilerParams(
            dimension_semantics=("arbitrary",), vmem_limit_bytes=VMEM_LIMIT),
        name="mix",
    )(c, g, sga, sgb, x2, wco, bco, wgl, wo, bo, l1g, l1b, wrh, wrl, br, tri)


def _dispatch_kernel(e_ref, rank_ref, cnt_ref, x1_ref, xs_in_ref,
                     xs_ref, dest_ref, be_ref, nused_ref, pstart_ref, sem):
    del xs_in_ref

    def plan(e, bstart):
        nb = (cnt_ref[e] + (BM - 1)) // BM
        pstart_ref[e] = bstart * BM

        def fill(j, carry):
            be_ref[bstart + j] = e
            return carry

        lax.fori_loop(0, nb, fill, 0)
        return bstart + nb

    n_used = lax.fori_loop(0, N_EXPERTS, plan, jnp.int32(0))
    nused_ref[0] = n_used
    last_e = be_ref[jnp.maximum(n_used - 1, 0)]

    def fill_rest(j, carry):
        be_ref[j] = last_e
        return carry

    lax.fori_loop(n_used, N_BLOCKS, fill_rest, 0)

    def row_copy(t, d):
        return pltpu.make_async_copy(x1_ref.at[pl.ds(t, 1)], xs_ref.at[pl.ds(d, 1)], sem)

    def wait_chunk():
        n = SCATTER_CHUNK * TOP_K
        pltpu.make_async_copy(x1_ref.at[pl.ds(0, n)], xs_ref.at[pl.ds(0, n)], sem).wait()

    def chunk(ci, carry):
        def tok(tt, carry2):
            t = ci * SCATTER_CHUNK + tt
            for kk in range(TOP_K):
                a = t * TOP_K + kk
                d = pstart_ref[e_ref[a]] + rank_ref[a]
                dest_ref[a] = d
                row_copy(t, d).start()
            return carry2

        lax.fori_loop(0, SCATTER_CHUNK, tok, 0)

        @pl.when(ci > 0)
        def _():
            wait_chunk()

        return carry

    lax.fori_loop(0, N_TOK // SCATTER_CHUNK, chunk, 0)
    wait_chunk()


def _dispatch(e_flat, rank_flat, cnt, x1, xs_zero):
    smem = pl.BlockSpec(memory_space=pltpu.SMEM)
    hbm = pl.BlockSpec(memory_space=pl.ANY)
    return pl.pallas_call(
        _dispatch_kernel,
        in_specs=[smem, smem, smem, hbm, hbm],
        out_specs=[hbm, smem, smem, smem],
        out_shape=[
            jax.ShapeDtypeStruct((N_SLOTS, D_MODEL), F32),
            jax.ShapeDtypeStruct((N_ASSIGN,), jnp.int32),
            jax.ShapeDtypeStruct((N_BLOCKS,), jnp.int32),
            jax.ShapeDtypeStruct((1,), jnp.int32),
        ],
        scratch_shapes=[pltpu.SMEM((N_EXPERTS,), jnp.int32), pltpu.SemaphoreType.DMA],
        input_output_aliases={4: 0},
        compiler_params=pltpu.CompilerParams(vmem_limit_bytes=VMEM_LIMIT),
        name="dispatch",
    )(e_flat, rank_flat, cnt, x1, xs_zero)


def _expert_kernel(be_ref, nused_ref, xs_ref, w1_ref, w3_ref, w2_ref, ys_ref,
                   w1b_ref, w3b_ref, w2b_ref):
    i = pl.program_id(0)
    active = i < nused_ref[0]
    changed = (i == 0) | (be_ref[i] != be_ref[jnp.maximum(i - 1, 0)])

    @pl.when(active & changed)
    def _():
        w1b_ref[...] = w1_ref[0].astype(BF16)
        w3b_ref[...] = w3_ref[0].astype(BF16)
        w2b_ref[...] = w2_ref[0].astype(BF16)

    @pl.when(active)
    def _():
        xb = xs_ref[...].astype(BF16)
        h1 = jnp.dot(xb, w1b_ref[...], preferred_element_type=F32)
        h3 = jnp.dot(xb, w3b_ref[...], preferred_element_type=F32)
        hdn = (h1 * _sigmoid(h1) * h3).astype(BF16)
        ys_ref[...] = jnp.dot(hdn, w2b_ref[...], preferred_element_type=F32)

    @pl.when(jnp.logical_not(active))
    def _():
        ys_ref[...] = jnp.zeros_like(ys_ref)


def _experts(block_e, n_used, xs, w1, w3, w2):
    last = lambda i, be, nu: jnp.minimum(i, nu[0] - 1)
    grid_spec = pltpu.PrefetchScalarGridSpec(
        num_scalar_prefetch=2,
        grid=(N_BLOCKS,),
        in_specs=[
            pl.BlockSpec((BM, D_MODEL), lambda i, be, nu: (last(i, be, nu), 0)),
            pl.BlockSpec((1, D_MODEL, D_EXPERT), lambda i, be, nu: (be[i], 0, 0)),
            pl.BlockSpec((1, D_MODEL, D_EXPERT), lambda i, be, nu: (be[i], 0, 0)),
            pl.BlockSpec((1, D_EXPERT, D_MODEL), lambda i, be, nu: (be[i], 0, 0)),
        ],
        out_specs=pl.BlockSpec((BM, D_MODEL), lambda i, be, nu: (i, 0)),
        scratch_shapes=[
            pltpu.VMEM((D_MODEL, D_EXPERT), BF16),
            pltpu.VMEM((D_MODEL, D_EXPERT), BF16),
            pltpu.VMEM((D_EXPERT, D_MODEL), BF16),
        ],
    )
    return pl.pallas_call(
        _expert_kernel,
        grid_spec=grid_spec,
        out_shape=jax.ShapeDtypeStruct((N_SLOTS, D_MODEL), F32),
        compiler_params=pltpu.CompilerParams(
            dimension_semantics=("arbitrary",), vmem_limit_bytes=VMEM_LIMIT),
        name="experts",
    )(block_e, n_used, xs, w1, w3, w2)


def _combine_kernel(dest_ref, route_ref, x1_ref, ys_ref, g_ref, b_ref, o_ref,
                    y0_ref, y1_ref, sem):
    i = pl.program_id(0)
    base = i * TC_COMB

    def issue(t, carry):
        a = (base + t) * TOP_K
        pltpu.make_async_copy(ys_ref.at[pl.ds(dest_ref[a], 1)], y0_ref.at[pl.ds(t, 1)], sem).start()
        pltpu.make_async_copy(ys_ref.at[pl.ds(dest_ref[a + 1], 1)], y1_ref.at[pl.ds(t, 1)],
                              sem).start()
        return carry

    lax.fori_loop(0, TC_COMB, issue, 0)
    pltpu.make_async_copy(ys_ref.at[pl.ds(0, TC_COMB)], y0_ref, sem).wait()
    pltpu.make_async_copy(ys_ref.at[pl.ds(0, TC_COMB)], y1_ref, sem).wait()
    w0 = route_ref[:, 2:3]
    w1 = route_ref[:, 3:4]
    z = DEEPNORM_ALPHA * x1_ref[...] + w0 * y0_ref[...] + w1 * y1_ref[...]
    o_ref[...] = _layer_norm(z, g_ref[...], b_ref[...])


def _combine(dest, route, x1, ys, g, b):
    tc = TC_COMB
    grid_spec = pltpu.PrefetchScalarGridSpec(
        num_scalar_prefetch=1,
        grid=(N_TOK // tc,),
        in_specs=[
            pl.BlockSpec((tc, LANE), lambda i, d: (i, 0)),
            pl.BlockSpec((tc, D_MODEL), lambda i, d: (i, 0)),
            pl.BlockSpec(memory_space=pl.ANY),
            pl.BlockSpec((1, D_MODEL), lambda i, d: (0, 0)),
            pl.BlockSpec((1, D_MODEL), lambda i, d: (0, 0)),
        ],
        out_specs=pl.BlockSpec((tc, D_MODEL), lambda i, d: (i, 0)),
        scratch_shapes=[
            pltpu.VMEM((tc, D_MODEL), F32),
            pltpu.VMEM((tc, D_MODEL), F32),
            pltpu.SemaphoreType.DMA,
        ],
    )
    return pl.pallas_call(
        _combine_kernel,
        grid_spec=grid_spec,
        out_shape=jax.ShapeDtypeStruct((N_TOK, D_MODEL), F32),
        compiler_params=pltpu.CompilerParams(
            dimension_semantics=("arbitrary",), vmem_limit_bytes=VMEM_LIMIT),
        name="combine",
    )(dest, route, x1, ys, g, b)


def _split_bf16(w):
    hi = w.astype(BF16)
    lo = (w - hi.astype(F32)).astype(BF16)
    return hi, lo


def kernel(x, w_in, b_in, conv_w, conv_b, conv_ln_g, conv_ln_b, w_conv_out, b_conv_out, w_gate_up, b_gate_up, gla_norm_g, w_gla_out, w_out, b_out, ln1_g, ln1_b, w_router_group, b_router_group, w_router_expert, b_router_expert, w1, w3, w2, ln2_g, ln2_b):
    x2 = x.reshape(N_TOK, D_MODEL)
    row = lambda v: v.reshape(1, -1)
    for l in range(w_in.shape[0]):
        f0 = 2 * D_MODEL + 2 * GLA_DK + 2 * GLA_DV
        f1 = f0 + GATE_RANK
        w_l, b_l = w_in[l], b_in[l]
        w_p = jnp.concatenate(
            [w_l[:, :f0], w_l[:, f1:], w_l[:, f0:f1],
             jnp.zeros((D_MODEL, F_PAD - GATE_RANK), F32)], axis=1).astype(BF16)
        b_p = row(jnp.concatenate(
            [b_l[:f0], b_l[f1:], b_l[f0:f1], jnp.zeros((F_PAD - GATE_RANK,), F32)]))
        wgu_p = jnp.concatenate(
            [w_gate_up[l], jnp.zeros((F_PAD - GATE_RANK, GLA_DK), F32)], axis=0).astype(BF16)
        u, q, k, v, rs, sga, sgb, la = _inproj(x2, w_p, b_p, wgu_p, row(b_gate_up[l]))

        cw_p = jnp.concatenate(
            [conv_w[l], jnp.zeros((CONV_HALO - CONV_WIDTH, D_MODEL), F32)], axis=0)
        c = _conv(u, cw_p, row(conv_b[l]), row(conv_ln_g[l]), row(conv_ln_b[l]))
        g = _gla(q, k, v, la, rs, row(gla_norm_g[l]))

        w_r = jnp.concatenate(
            [w_router_expert[l], w_router_group[l],
             jnp.zeros((D_MODEL, LANE - N_EXPERTS - N_GROUPS), F32)], axis=1)
        b_r = row(jnp.concatenate(
            [b_router_expert[l], b_router_group[l],
             jnp.zeros((LANE - N_EXPERTS - N_GROUPS,), F32)]))
        wrh, wrl = _split_bf16(w_r)
        ri = lax.broadcasted_iota(jnp.int32, (TM_MIX, TM_MIX), 0)
        ci = lax.broadcasted_iota(jnp.int32, (TM_MIX, TM_MIX), 1)
        tri = (ri > ci).astype(BF16)
        x1, route, cnt = _mix(
            c, g, sga, sgb, x2, w_conv_out[l].astype(BF16), row(b_conv_out[l]),
            w_gla_out[l].astype(BF16), w_out[l].astype(BF16), row(b_out[l]),
            row(ln1_g[l]), row(ln1_b[l]), wrh, wrl, b_r, tri)

        e_flat = route[:, 0:2].astype(jnp.int32).reshape(N_ASSIGN)
        rank_flat = route[:, 4:6].astype(jnp.int32).reshape(N_ASSIGN)
        cnt_i = cnt[0, :N_EXPERTS].astype(jnp.int32)
        xs, dest, block_e, n_used = _dispatch(
            e_flat, rank_flat, cnt_i, x1, jnp.zeros((N_SLOTS, D_MODEL), F32))
        ys = _experts(block_e, n_used, xs, w1[l], w3[l], w2[l])
        x2 = _combine(dest, route, x1, ys, row(ln2_g[l]), row(ln2_b[l]))
    return x2.reshape(x.shape)
```

```python
import functools

import jax
import jax.numpy as jnp
from jax import lax
from jax.experimental import pallas as pl
from jax.experimental.pallas import tpu as pltpu

F32 = jnp.float32
BF16 = jnp.bfloat16

D_MODEL = 1024
BATCH = 8
SEQ = 2048
N_TOK = BATCH * SEQ
CHUNK = 64
CONV_WIDTH = 31
GLA_HEADS = 4
GLA_DK = 512
GLA_DV = 1024
GLA_HK = 128
GLA_HV = 256
GATE_RANK = 16
GATE_TAU = 16.0
N_GROUPS = 8
EXPERTS_PER_GROUP = 8
N_EXPERTS = 64
TOP_K = 2
D_EXPERT = 512
LN_EPS = 1e-5
RMS_EPS = 1e-6
DEEPNORM_ALPHA = 2.0 ** 0.25

LANE = 128
TOKEN_ROWS = D_MODEL // LANE
F_PAD = LANE
N_PROJ = 7 * D_MODEL + F_PAD
TM_PROJ = 512
TS_CONV = 256
CONV_HALO = 32
CONV_ROWS = 32
SUBLANES = 8
CONV_SUB = 128
CONV_WIN = 256
CONV_SHROWS = CONV_SUB + CONV_HALO
TS_GLA = 512
TM_MIX = 512
BM = 256
N_BLOCKS = N_TOK * TOP_K // BM + N_EXPERTS
N_SLOTS = N_BLOCKS * BM
N_ASSIGN = N_TOK * TOP_K
TC_COMB = 256
COMB_UNROLL = 4
TD_DISP = 512
DISP_UNROLL = 8
VMEM_LIMIT = 56 * 1024 * 1024


def _sigmoid(x):
    return 1.0 / (1.0 + jnp.exp(-x))


def _to_token_tiles(ref, val):
    n = val.shape[0]
    for c in range(TOKEN_ROWS):
        ref[pl.ds(c, n, stride=TOKEN_ROWS), :] = val[:, c * LANE:(c + 1) * LANE]


def _from_token_tiles(ref, n):
    return jnp.concatenate(
        [ref[pl.ds(c, n, stride=TOKEN_ROWS), :] for c in range(TOKEN_ROWS)], axis=1)


def _layer_norm(z, g, b):
    mu = jnp.mean(z, axis=-1, keepdims=True)
    zc = z - mu
    var = jnp.mean(zc * zc, axis=-1, keepdims=True)
    return zc * lax.rsqrt(var + LN_EPS) * g + b


def _inproj_kernel(x_ref, w_ref, b_ref, wgu_ref, bgu_ref,
                   u_ref, q_ref, k_ref, v_ref, rs_ref, sga_ref, sgb_ref, la_ref):
    xb = x_ref[...].astype(BF16)
    half = D_MODEL // 2

    def seg(c0, n):
        return (jnp.dot(xb, w_ref[:, c0:c0 + n], preferred_element_type=F32)
                + b_ref[:, c0:c0 + n])

    for j in range(2):
        a = seg(j * half, half)
        g = seg(D_MODEL + j * half, half)
        u_ref[:, j * half:(j + 1) * half] = (a * _sigmoid(g)).astype(BF16)
    q_ref[...] = (seg(2 * D_MODEL, GLA_DK) * (GLA_HK ** -0.5)).astype(BF16)
    k_ref[...] = seg(2 * D_MODEL + GLA_DK, GLA_DK).astype(BF16)
    for j in range(2):
        v_ref[:, j * half:(j + 1) * half] = seg(3 * D_MODEL + j * half, half).astype(BF16)
    for j in range(2):
        r = seg(4 * D_MODEL + j * half, half)
        rs_ref[:, j * half:(j + 1) * half] = (r * _sigmoid(r)).astype(BF16)
    for j in range(2):
        sga_ref[:, j * half:(j + 1) * half] = _sigmoid(
            seg(5 * D_MODEL + j * half, half)).astype(BF16)
    for j in range(2):
        sgb_ref[:, j * half:(j + 1) * half] = _sigmoid(
            seg(6 * D_MODEL + j * half, half)).astype(BF16)
    f = seg(7 * D_MODEL, F_PAD)
    z = jnp.dot(f.astype(BF16), wgu_ref[...], preferred_element_type=F32) + bgu_ref[...]
    la_ref[...] = (jnp.minimum(z, 0.0) - jnp.log(1.0 + jnp.exp(-jnp.abs(z)))) * (1.0 / GATE_TAU)


def _inproj(x2, w_p, b_p, wgu_p, bgu):
    tm = TM_PROJ
    row = lambda i: (i, 0)
    fixed = lambda i: (0, 0)
    tok = lambda n, dt: jax.ShapeDtypeStruct((N_TOK, n), dt)
    return pl.pallas_call(
        _inproj_kernel,
        grid=(N_TOK // tm,),
        in_specs=[
            pl.BlockSpec((tm, D_MODEL), row),
            pl.BlockSpec((D_MODEL, N_PROJ), fixed, pipeline_mode=pl.Buffered(1)),
            pl.BlockSpec((1, N_PROJ), fixed),
            pl.BlockSpec((F_PAD, GLA_DK), fixed),
            pl.BlockSpec((1, GLA_DK), fixed),
        ],
        out_specs=[
            pl.BlockSpec((tm, D_MODEL), row),
            pl.BlockSpec((tm, GLA_DK), row),
            pl.BlockSpec((tm, GLA_DK), row),
            pl.BlockSpec((tm, GLA_DV), row),
            pl.BlockSpec((tm, GLA_DV), row),
            pl.BlockSpec((tm, D_MODEL), row),
            pl.BlockSpec((tm, D_MODEL), row),
            pl.BlockSpec((tm, GLA_DK), row),
        ],
        out_shape=[
            tok(D_MODEL, BF16), tok(GLA_DK, BF16), tok(GLA_DK, BF16), tok(GLA_DV, BF16),
            tok(GLA_DV, BF16), tok(D_MODEL, BF16), tok(D_MODEL, BF16), tok(GLA_DK, F32),
        ],
        compiler_params=pltpu.CompilerParams(
            dimension_semantics=("arbitrary",), vmem_limit_bytes=VMEM_LIMIT),
        name="inproj",
    )(x2, w_p, b_p, wgu_p, bgu)


def _conv_kernel(u_ref, tail_ref, shift_ref, cw_ref, cb_ref, g_ref, b_ref, c_ref,
                 ext_ref, shf_ref):
    s = pl.program_id(1)
    tail = tail_ref[...]
    ext_ref[0:CONV_HALO, :] = jnp.where(s == 0, jnp.zeros_like(tail), tail)
    ext_ref[CONV_HALO:CONV_HALO + TS_CONV, :] = u_ref[...]
    ext_ref[CONV_HALO + TS_CONV:, :] = jnp.zeros(
        (CONV_WIN - CONV_SUB - CONV_HALO, D_MODEL), BF16)
    off = CONV_HALO - (CONV_WIDTH - 1)
    for sb in range(TS_CONV // CONV_SUB):
        base = sb * CONV_SUB
        shf_ref[...] = jnp.dot(shift_ref[...], ext_ref[base:base + CONV_WIN, :],
                               preferred_element_type=F32)
        n_sub = CONV_ROWS // SUBLANES
        for i in range(CONV_SUB // CONV_ROWS):
            r0 = i * CONV_ROWS
            acc = [jnp.zeros((SUBLANES, D_MODEL), F32) for _ in range(n_sub)]
            for j in range(CONV_WIDTH):
                l0 = r0 + off + j
                b = l0 % SUBLANES
                m0 = b * CONV_SHROWS + (l0 - b)
                wj = cw_ref[j * SUBLANES:(j + 1) * SUBLANES, :]
                for k in range(n_sub):
                    rk = m0 + k * SUBLANES
                    acc[k] = acc[k] + shf_ref[rk:rk + SUBLANES, :] * wj
            y = _layer_norm(jnp.concatenate(acc, axis=0) + cb_ref[...], g_ref[...], b_ref[...])
            c_ref[base + r0:base + r0 + CONV_ROWS, :] = (y * _sigmoid(y)).astype(BF16)


def _conv(u, shift, cw_p, cb, g, b):
    ts = TS_CONV
    n_s = SEQ // ts
    per_tile = ts // CONV_HALO
    fixed = lambda bi, si: (0, 0)
    return pl.pallas_call(
        _conv_kernel,
        grid=(BATCH, n_s),
        in_specs=[
            pl.BlockSpec((ts, D_MODEL), lambda bi, si: (bi * n_s + si, 0)),
            pl.BlockSpec((CONV_HALO, D_MODEL),
                         lambda bi, si: (jnp.maximum((bi * n_s + si) * per_tile - 1, 0), 0)),
            pl.BlockSpec((SUBLANES * CONV_SHROWS, CONV_WIN), fixed),
            pl.BlockSpec((CONV_WIDTH * SUBLANES, D_MODEL), fixed),
            pl.BlockSpec((1, D_MODEL), fixed),
            pl.BlockSpec((1, D_MODEL), fixed),
            pl.BlockSpec((1, D_MODEL), fixed),
        ],
        out_specs=pl.BlockSpec((ts, D_MODEL), lambda bi, si: (bi * n_s + si, 0)),
        out_shape=jax.ShapeDtypeStruct((N_TOK, D_MODEL), BF16),
        scratch_shapes=[
            pltpu.VMEM((ts + CONV_WIN - CONV_SUB, D_MODEL), BF16),
            pltpu.VMEM((SUBLANES * CONV_SHROWS, D_MODEL), F32),
        ],
        compiler_params=pltpu.CompilerParams(
            dimension_semantics=("arbitrary", "arbitrary"), vmem_limit_bytes=VMEM_LIMIT),
        name="conv",
    )(u, u, shift, cw_p, cb, g, b)


def _gla_kernel(q_ref, k_ref, v_ref, la_ref, rs_ref, ng_ref, o_ref, st_ref):
    s = pl.program_id(1)

    @pl.when(s == 0)
    def _():
        st_ref[...] = jnp.zeros_like(st_ref)

    rr = lax.broadcasted_iota(jnp.int32, (CHUNK, CHUNK), 0)
    cc = lax.broadcasted_iota(jnp.int32, (CHUNK, CHUNK), 1)
    tri = jnp.where(rr >= cc, 1.0, 0.0).astype(BF16)

    def chunk(c, carry):
        r0 = pl.multiple_of(c * CHUNK, CHUNK)
        rows = pl.ds(r0, CHUNK)
        la = la_ref[rows, :]
        hi = la.astype(BF16)
        lo = (la - hi.astype(F32)).astype(BF16)
        cum = (jnp.dot(tri, hi, preferred_element_type=F32)
               + jnp.dot(tri, lo, preferred_element_type=F32))
        cend = cum[CHUNK - 1:CHUNK, :]
        kd = (k_ref[rows, :].astype(F32) * jnp.exp(cend - cum)).astype(BF16)
        dec = jnp.exp(cend)
        qc = q_ref[rows, :]
        vc = v_ref[rows, :]
        for h in range(GLA_HEADS):
            ks = slice(h * GLA_HK, (h + 1) * GLA_HK)
            vs = slice(h * GLA_HV, (h + 1) * GLA_HV)
            kv_t = lax.dot_general(vc[:, vs], kd[:, ks], (((0,), (0,)), ((), ())),
                                   preferred_element_type=F32)
            st = st_ref[h] * dec[:, ks] + kv_t
            st_ref[h] = st
            o = lax.dot_general(qc[:, ks], st.astype(BF16), (((1,), (1,)), ((), ())),
                                preferred_element_type=F32)
            ms = jnp.mean(o * o, axis=-1, keepdims=True)
            on = o * lax.rsqrt(ms + RMS_EPS) * ng_ref[:, vs]
            o_ref[rows, vs] = (on * rs_ref[rows, vs].astype(F32)).astype(BF16)
        return carry

    lax.fori_loop(0, TS_GLA // CHUNK, chunk, 0)


def _gla(q, k, v, la, rs, ng):
    ts = TS_GLA
    n_s = SEQ // ts
    row = lambda bi, si: (bi * n_s + si, 0)
    return pl.pallas_call(
        _gla_kernel,
        grid=(BATCH, n_s),
        in_specs=[
            pl.BlockSpec((ts, GLA_DK), row),
            pl.BlockSpec((ts, GLA_DK), row),
            pl.BlockSpec((ts, GLA_DV), row),
            pl.BlockSpec((ts, GLA_DK), row),
            pl.BlockSpec((ts, GLA_DV), row),
            pl.BlockSpec((1, GLA_DV), lambda bi, si: (0, 0)),
        ],
        out_specs=pl.BlockSpec((ts, GLA_DV), row),
        out_shape=jax.ShapeDtypeStruct((N_TOK, GLA_DV), BF16),
        scratch_shapes=[pltpu.VMEM((GLA_HEADS, GLA_HV, GLA_HK), F32)],
        compiler_params=pltpu.CompilerParams(
            dimension_semantics=("arbitrary", "arbitrary"), vmem_limit_bytes=VMEM_LIMIT),
        name="gla",
    )(q, k, v, la, rs, ng)


def _mix_kernel(c_ref, g_ref, sga_ref, sgb_ref, x_ref, wco_ref, bco_ref, wgl_ref, wo_ref, bo_ref,
                l1g_ref, l1b_ref, wrh_ref, wrl_ref, br_ref, tri_ref,
                x1_ref, route_ref, cnt_ref, carry_ref):
    i = pl.program_id(0)

    @pl.when(i == 0)
    def _():
        carry_ref[...] = jnp.zeros_like(carry_ref)

    yc = jnp.dot(c_ref[...], wco_ref[...], preferred_element_type=F32) + bco_ref[...]
    yg = jnp.dot(g_ref[...], wgl_ref[...], preferred_element_type=F32)
    merged = sga_ref[...].astype(F32) * yc + sgb_ref[...].astype(F32) * yg
    mix = jnp.dot(merged.astype(BF16), wo_ref[...], preferred_element_type=F32) + bo_ref[...]
    x1 = _layer_norm(DEEPNORM_ALPHA * x_ref[...] + mix, l1g_ref[...], l1b_ref[...])
    _to_token_tiles(x1_ref, x1)

    xh = x1.astype(BF16)
    xl = (x1 - xh.astype(F32)).astype(BF16)
    lg = (jnp.dot(xh, wrh_ref[...], preferred_element_type=F32)
          + jnp.dot(xl, wrh_ref[...], preferred_element_type=F32)
          + jnp.dot(xh, wrl_ref[...], preferred_element_type=F32)) + br_ref[...]

    tm = lg.shape[0]
    lane = lax.broadcasted_iota(jnp.int32, (tm, LANE), 1)
    neg = -jnp.inf
    big = jnp.int32(1 << 20)
    is_g = (lane >= N_EXPERTS) & (lane < N_EXPERTS + N_GROUPS)
    gl = jnp.where(is_g, lg, neg)
    gmax = jnp.max(gl, axis=-1, keepdims=True)
    gsel = jnp.min(jnp.where(gl == gmax, lane, big), axis=-1, keepdims=True) - N_EXPERTS
    gw = 1.0 / jnp.sum(jnp.exp(gl - gmax), axis=-1, keepdims=True)
    in_grp = (lane < N_EXPERTS) & ((lane >> 3) == gsel)
    el = jnp.where(in_grp, lg, neg)
    m1 = jnp.max(el, axis=-1, keepdims=True)
    i1 = jnp.min(jnp.where(el == m1, lane, big), axis=-1, keepdims=True)
    el2 = jnp.where(lane == i1, neg, el)
    m2 = jnp.max(el2, axis=-1, keepdims=True)
    i2 = jnp.min(jnp.where(el2 == m2, lane, big), axis=-1, keepdims=True)
    t = jnp.exp(m2 - m1)
    w1 = gw / (1.0 + t)
    w2 = gw * t / (1.0 + t)

    oh1 = lane == i1
    oh2 = lane == i2
    osum = jnp.where(oh1 | oh2, 1.0, 0.0)
    excl = jnp.dot(tri_ref[...], osum.astype(BF16), preferred_element_type=F32) + carry_ref[...]
    rank1 = jnp.sum(jnp.where(oh1, excl, 0.0), axis=-1, keepdims=True)
    rank2 = jnp.sum(jnp.where(oh2, excl, 0.0), axis=-1, keepdims=True)
    new_cnt = carry_ref[...] + jnp.sum(osum, axis=0, keepdims=True)
    carry_ref[...] = new_cnt
    cnt_ref[...] = new_cnt

    route = jnp.where(lane == 0, i1.astype(F32), 0.0)
    route = jnp.where(lane == 1, i2.astype(F32), route)
    route = jnp.where(lane == 2, w1, route)
    route = jnp.where(lane == 3, w2, route)
    route = jnp.where(lane == 4, rank1, route)
    route = jnp.where(lane == 5, rank2, route)
    route_ref[...] = route


def _mix(c, g, sga, sgb, x2, wco, bco, wgl, wo, bo, l1g, l1b, wrh, wrl, br, tri):
    tm = TM_MIX
    row = lambda i: (i, 0)
    fixed = lambda i: (0, 0)
    mat = pl.BlockSpec((D_MODEL, D_MODEL), fixed, pipeline_mode=pl.Buffered(1))
    vec = pl.BlockSpec((1, D_MODEL), fixed)
    return pl.pallas_call(
        _mix_kernel,
        grid=(N_TOK // tm,),
        in_specs=[
            pl.BlockSpec((tm, D_MODEL), row), pl.BlockSpec((tm, D_MODEL), row),
            pl.BlockSpec((tm, D_MODEL), row), pl.BlockSpec((tm, D_MODEL), row),
            pl.BlockSpec((tm, D_MODEL), row),
            mat, vec, mat, mat, vec, vec, vec,
            pl.BlockSpec((D_MODEL, LANE), fixed), pl.BlockSpec((D_MODEL, LANE), fixed),
            pl.BlockSpec((1, LANE), fixed),
            pl.BlockSpec((tm, tm), fixed),
        ],
        out_specs=[
            pl.BlockSpec((tm * TOKEN_ROWS, LANE), row),
            pl.BlockSpec((tm, LANE), row),
            pl.BlockSpec((1, LANE), fixed),
        ],
        out_shape=[
            jax.ShapeDtypeStruct((N_TOK * TOKEN_ROWS, LANE), F32),
            jax.ShapeDtypeStruct((N_TOK, LANE), F32),
            jax.ShapeDtypeStruct((1, LANE), F32),
        ],
        scratch_shapes=[pltpu.VMEM((1, LANE), F32)],
        compiler_params=pltpu.CompilerParams(
            dimension_semantics=("arbitrary",), vmem_limit_bytes=VMEM_LIMIT),
        name="mix",
    )(c, g, sga, sgb, x2, wco, bco, wgl, wo, bo, l1g, l1b, wrh, wrl, br, tri)


def _dispatch_kernel(e_ref, rank_ref, cnt_ref, x1_ref,
                     xs_ref, dest_ref, be_ref, nused_ref, pstart_ref, zero_ref, sem, zsem):
    i = pl.program_id(0)

    def zero_copy(b):
        rows = BM * TOKEN_ROWS
        return pltpu.make_async_copy(zero_ref, xs_ref.at[pl.ds(b * rows, rows)], zsem)

    @pl.when(i == 0)
    def _():
        zero_ref[...] = jnp.zeros_like(zero_ref)

        def plan(e, bstart):
            nb = (cnt_ref[e] + (BM - 1)) // BM
            pstart_ref[e] = bstart * BM

            def fill(j, carry):
                be_ref[bstart + j] = e
                return carry

            lax.fori_loop(0, nb, fill, 0)

            @pl.when(nb > 0)
            def _():
                zero_copy(bstart + nb - 1).start()

            return bstart + nb

        n_used = lax.fori_loop(0, N_EXPERTS, plan, jnp.int32(0))
        nused_ref[0] = n_used
        last_e = be_ref[jnp.maximum(n_used - 1, 0)]

        def fill_rest(j, carry):
            be_ref[j] = last_e
            zero_copy(j).start()
            return carry

        lax.fori_loop(n_used, N_BLOCKS, fill_rest, 0)

        def drain(e, carry):
            @pl.when(cnt_ref[e] > 0)
            def _():
                zero_copy(0).wait()

            return carry

        lax.fori_loop(0, N_EXPERTS, drain, 0)

        def drain_rest(j, carry):
            zero_copy(0).wait()
            return carry

        lax.fori_loop(n_used, N_BLOCKS, drain_rest, 0)

    base = i * TD_DISP

    def tok_group(tg, carry):
        for uu in range(DISP_UNROLL):
            tt = tg * DISP_UNROLL + uu
            for kk in range(TOP_K):
                a = (base + tt) * TOP_K + kk
                d = pstart_ref[e_ref[a]] + rank_ref[a]
                dest_ref[a] = d
                pltpu.make_async_copy(
                    x1_ref.at[pl.ds(pl.multiple_of(tt * TOKEN_ROWS, TOKEN_ROWS), TOKEN_ROWS)],
                    xs_ref.at[pl.ds(pl.multiple_of(d * TOKEN_ROWS, TOKEN_ROWS), TOKEN_ROWS)],
                    sem).start()
        return carry

    lax.fori_loop(0, TD_DISP // DISP_UNROLL, tok_group, 0)
    for kk in range(TOP_K):
        pltpu.make_async_copy(x1_ref, xs_ref.at[pl.ds(0, TD_DISP * TOKEN_ROWS)], sem).wait()


def _dispatch(e_flat, rank_flat, cnt, x1):
    smem = pl.BlockSpec(memory_space=pltpu.SMEM)
    grid_spec = pltpu.PrefetchScalarGridSpec(
        num_scalar_prefetch=3,
        grid=(N_TOK // TD_DISP,),
        in_specs=[pl.BlockSpec((TD_DISP * TOKEN_ROWS, LANE), lambda i, e, r, c: (i, 0))],
        out_specs=[pl.BlockSpec(memory_space=pl.ANY), smem, smem, smem],
        scratch_shapes=[
            pltpu.SMEM((N_EXPERTS,), jnp.int32),
            pltpu.VMEM((BM * TOKEN_ROWS, LANE), F32),
            pltpu.SemaphoreType.DMA,
            pltpu.SemaphoreType.DMA,
        ],
    )
    return pl.pallas_call(
        _dispatch_kernel,
        grid_spec=grid_spec,
        out_shape=[
            jax.ShapeDtypeStruct((N_SLOTS * TOKEN_ROWS, LANE), F32),
            jax.ShapeDtypeStruct((N_ASSIGN,), jnp.int32),
            jax.ShapeDtypeStruct((N_BLOCKS,), jnp.int32),
            jax.ShapeDtypeStruct((1,), jnp.int32),
        ],
        compiler_params=pltpu.CompilerParams(
            dimension_semantics=("arbitrary",), vmem_limit_bytes=VMEM_LIMIT),
        name="dispatch",
    )(e_flat, rank_flat, cnt, x1)


def _expert_kernel(be_ref, nused_ref, xs_ref, w1_ref, w3_ref, w2_ref, ys_ref,
                   w1b_ref, w3b_ref, w2b_ref):
    i = pl.program_id(0)
    active = i < nused_ref[0]
    changed = (i == 0) | (be_ref[i] != be_ref[jnp.maximum(i - 1, 0)])

    @pl.when(active & changed)
    def _():
        w1b_ref[...] = w1_ref[0].astype(BF16)
        w3b_ref[...] = w3_ref[0].astype(BF16)
        w2b_ref[...] = w2_ref[0].astype(BF16)

    @pl.when(active)
    def _():
        xb = _from_token_tiles(xs_ref, BM).astype(BF16)
        h1 = jnp.dot(xb, w1b_ref[...], preferred_element_type=F32)
        h3 = jnp.dot(xb, w3b_ref[...], preferred_element_type=F32)
        hdn = (h1 * _sigmoid(h1) * h3).astype(BF16)
        _to_token_tiles(ys_ref, jnp.dot(hdn, w2b_ref[...], preferred_element_type=F32))

    @pl.when(jnp.logical_not(active))
    def _():
        ys_ref[...] = jnp.zeros_like(ys_ref)


def _experts(block_e, n_used, xs, w1, w3, w2):
    last = lambda i, be, nu: jnp.minimum(i, nu[0] - 1)
    grid_spec = pltpu.PrefetchScalarGridSpec(
        num_scalar_prefetch=2,
        grid=(N_BLOCKS,),
        in_specs=[
            pl.BlockSpec((BM * TOKEN_ROWS, LANE), lambda i, be, nu: (last(i, be, nu), 0)),
            pl.BlockSpec((1, D_MODEL, D_EXPERT), lambda i, be, nu: (be[i], 0, 0)),
            pl.BlockSpec((1, D_MODEL, D_EXPERT), lambda i, be, nu: (be[i], 0, 0)),
            pl.BlockSpec((1, D_EXPERT, D_MODEL), lambda i, be, nu: (be[i], 0, 0)),
        ],
        out_specs=pl.BlockSpec((BM * TOKEN_ROWS, LANE), lambda i, be, nu: (i, 0)),
        scratch_shapes=[
            pltpu.VMEM((D_MODEL, D_EXPERT), BF16),
            pltpu.VMEM((D_MODEL, D_EXPERT), BF16),
            pltpu.VMEM((D_EXPERT, D_MODEL), BF16),
        ],
    )
    return pl.pallas_call(
        _expert_kernel,
        grid_spec=grid_spec,
        out_shape=jax.ShapeDtypeStruct((N_SLOTS * TOKEN_ROWS, LANE), F32),
        compiler_params=pltpu.CompilerParams(
            dimension_semantics=("arbitrary",), vmem_limit_bytes=VMEM_LIMIT),
        name="experts",
    )(block_e, n_used, xs, w1, w3, w2)


def _combine_kernel(dest_ref, route_ref, x1_ref, ys_ref, g_ref, b_ref, o_ref, y_ref, sem):
    i = pl.program_id(0)
    n_tiles = pl.num_programs(0)
    tile_rows = TC_COMB * TOKEN_ROWS

    def issue(tile, slot):
        base = tile * TC_COMB

        def group(tg, carry):
            for uu in range(COMB_UNROLL):
                t = tg * COMB_UNROLL + uu
                a = (base + t) * TOP_K
                dst = pl.ds(pl.multiple_of(t * TOKEN_ROWS, TOKEN_ROWS), TOKEN_ROWS)
                for kk in range(TOP_K):
                    src = pl.ds(pl.multiple_of(dest_ref[a + kk] * TOKEN_ROWS, TOKEN_ROWS),
                                TOKEN_ROWS)
                    pltpu.make_async_copy(ys_ref.at[src], y_ref.at[slot, kk, dst],
                                          sem.at[slot]).start()
            return carry

        lax.fori_loop(0, TC_COMB // COMB_UNROLL, group, 0)

    @pl.when(i == 0)
    def _():
        issue(0, 0)

    @pl.when(i + 1 < n_tiles)
    def _():
        issue(i + 1, (i + 1) % 2)

    slot = i % 2
    for kk in range(TOP_K):
        pltpu.make_async_copy(ys_ref.at[pl.ds(0, tile_rows)], y_ref.at[slot, kk],
                              sem.at[slot]).wait()
    w0 = route_ref[:, 2:3]
    w1 = route_ref[:, 3:4]
    z = (DEEPNORM_ALPHA * _from_token_tiles(x1_ref, TC_COMB)
         + w0 * _from_token_tiles(y_ref.at[slot, 0], TC_COMB)
         + w1 * _from_token_tiles(y_ref.at[slot, 1], TC_COMB))
    o_ref[...] = _layer_norm(z, g_ref[...], b_ref[...])


def _combine(dest, route, x1, ys, g, b):
    tc = TC_COMB
    grid_spec = pltpu.PrefetchScalarGridSpec(
        num_scalar_prefetch=1,
        grid=(N_TOK // tc,),
        in_specs=[
            pl.BlockSpec((tc, LANE), lambda i, d: (i, 0)),
            pl.BlockSpec((tc * TOKEN_ROWS, LANE), lambda i, d: (i, 0)),
            pl.BlockSpec(memory_space=pl.ANY),
            pl.BlockSpec((1, D_MODEL), lambda i, d: (0, 0)),
            pl.BlockSpec((1, D_MODEL), lambda i, d: (0, 0)),
        ],
        out_specs=pl.BlockSpec((tc, D_MODEL), lambda i, d: (i, 0)),
        scratch_shapes=[
            pltpu.VMEM((2, TOP_K, tc * TOKEN_ROWS, LANE), F32),
            pltpu.SemaphoreType.DMA((2,)),
        ],
    )
    return pl.pallas_call(
        _combine_kernel,
        grid_spec=grid_spec,
        out_shape=jax.ShapeDtypeStruct((N_TOK, D_MODEL), F32),
        compiler_params=pltpu.CompilerParams(
            dimension_semantics=("arbitrary",), vmem_limit_bytes=VMEM_LIMIT),
        name="combine",
    )(dest, route, x1, ys, g, b)


def _split_bf16(w):
    hi = w.astype(BF16)
    lo = (w - hi.astype(F32)).astype(BF16)
    return hi, lo


def kernel(x, w_in, b_in, conv_w, conv_b, conv_ln_g, conv_ln_b, w_conv_out, b_conv_out, w_gate_up, b_gate_up, gla_norm_g, w_gla_out, w_out, b_out, ln1_g, ln1_b, w_router_group, b_router_group, w_router_expert, b_router_expert, w1, w3, w2, ln2_g, ln2_b):
    x2 = x.reshape(N_TOK, D_MODEL)
    row = lambda v: v.reshape(1, -1)
    for l in range(w_in.shape[0]):
        f0 = 2 * D_MODEL + 2 * GLA_DK + 2 * GLA_DV
        f1 = f0 + GATE_RANK
        w_l, b_l = w_in[l], b_in[l]
        w_p = jnp.concatenate(
            [w_l[:, :f0], w_l[:, f1:], w_l[:, f0:f1],
             jnp.zeros((D_MODEL, F_PAD - GATE_RANK), F32)], axis=1).astype(BF16)
        b_p = row(jnp.concatenate(
            [b_l[:f0], b_l[f1:], b_l[f0:f1], jnp.zeros((F_PAD - GATE_RANK,), F32)]))
        wgu_p = jnp.concatenate(
            [w_gate_up[l], jnp.zeros((F_PAD - GATE_RANK, GLA_DK), F32)], axis=0).astype(BF16)
        u, q, k, v, rs, sga, sgb, la = _inproj(x2, w_p, b_p, wgu_p, row(b_gate_up[l]))

        cw_p = jnp.repeat(conv_w[l], SUBLANES, axis=0)
        sr = lax.broadcasted_iota(jnp.int32, (SUBLANES * CONV_SHROWS, CONV_WIN), 0)
        sc = lax.broadcasted_iota(jnp.int32, (SUBLANES * CONV_SHROWS, CONV_WIN), 1)
        shift = (sc == sr % CONV_SHROWS + sr // CONV_SHROWS).astype(BF16)
        c = _conv(u, shift, cw_p, row(conv_b[l]), row(conv_ln_g[l]), row(conv_ln_b[l]))
        g = _gla(q, k, v, la, rs, row(gla_norm_g[l]))

        w_r = jnp.concatenate(
            [w_router_expert[l], w_router_group[l],
             jnp.zeros((D_MODEL, LANE - N_EXPERTS - N_GROUPS), F32)], axis=1)
        b_r = row(jnp.concatenate(
            [b_router_expert[l], b_router_group[l],
             jnp.zeros((LANE - N_EXPERTS - N_GROUPS,), F32)]))
        wrh, wrl = _split_bf16(w_r)
        ri = lax.broadcasted_iota(jnp.int32, (TM_MIX, TM_MIX), 0)
        ci = lax.broadcasted_iota(jnp.int32, (TM_MIX, TM_MIX), 1)
        tri = (ri > ci).astype(BF16)
        x1, route, cnt = _mix(
            c, g, sga, sgb, x2, w_conv_out[l].astype(BF16), row(b_conv_out[l]),
            w_gla_out[l].astype(BF16), w_out[l].astype(BF16), row(b_out[l]),
            row(ln1_g[l]), row(ln1_b[l]), wrh, wrl, b_r, tri)

        e_flat = route[:, 0:2].astype(jnp.int32).reshape(N_ASSIGN)
        rank_flat = route[:, 4:6].astype(jnp.int32).reshape(N_ASSIGN)
        cnt_i = cnt[0, :N_EXPERTS].astype(jnp.int32)
        xs, dest, block_e, n_used = _dispatch(e_flat, rank_flat, cnt_i, x1)
        ys = _experts(block_e, n_used, xs, w1[l], w3[l], w2[l])
        x2 = _combine(dest, route, x1, ys, row(ln2_g[l]), row(ln2_b[l]))
    return x2.reshape(x.shape)
```

```python
import jax
import jax.numpy as jnp
from jax import lax
from jax.experimental import pallas as pl
from jax.experimental.pallas import tpu as pltpu

F32 = jnp.float32
BF16 = jnp.bfloat16

D_MODEL = 1024
BATCH = 8
SEQ = 2048
N_TOK = BATCH * SEQ
CHUNK = 64
CONV_WIDTH = 31
GLA_HEADS = 4
GLA_DK = 512
GLA_DV = 1024
GLA_HK = 128
GLA_HV = 256
GATE_RANK = 16
GATE_TAU = 16.0
N_GROUPS = 8
EXPERTS_PER_GROUP = 8
N_EXPERTS = 64
TOP_K = 2
D_EXPERT = 512
LN_EPS = 1e-5
RMS_EPS = 1e-6
DEEPNORM_ALPHA = 2.0 ** 0.25

LANE = 128
TOKEN_ROWS = D_MODEL // LANE
F_PAD = LANE
N_PROJ_A = 2 * D_MODEL + 2 * GLA_DK + 2 * GLA_DV
N_PROJ = N_PROJ_A + 2 * D_MODEL + F_PAD
TM_PROJ = 512
TS_CONV = 256
CONV_HALO = 32
CONV_ROWS = 32
SUBLANES = 8
CONV_SUB = 128
CONV_WIN = 256
CONV_SHROWS = CONV_SUB + CONV_HALO
TS_GLA = 256
NB_GLA = 4
TM_MIX = 512
META_ROWS = 8
BM = 256
N_BLOCKS = N_TOK * TOP_K // BM + N_EXPERTS
N_SLOTS = N_BLOCKS * BM
TC_COMB = 256
COMB_UNROLL = 4
TD_DISP = 512
DISP_UNROLL = 8
VMEM_LIMIT = 56 * 1024 * 1024


def _sigmoid(x):
    return 1.0 / (1.0 + jnp.exp(-x))


def _to_token_tiles(ref, val):
    n = val.shape[0]
    for c in range(TOKEN_ROWS):
        ref[pl.ds(c, n, stride=TOKEN_ROWS), :] = val[:, c * LANE:(c + 1) * LANE]


def _from_token_tiles(ref, n):
    return jnp.concatenate(
        [ref[pl.ds(c, n, stride=TOKEN_ROWS), :] for c in range(TOKEN_ROWS)], axis=1)


def _layer_norm(z, g, b):
    mu = jnp.mean(z, axis=-1, keepdims=True)
    zc = z - mu
    var = jnp.mean(zc * zc, axis=-1, keepdims=True)
    return zc * lax.rsqrt(var + LN_EPS) * g + b


def _inproj_kernel(x_ref, wa_ref, wb_ref, wf_ref, b_ref, wgu_ref, bgu_ref,
                   u_ref, q_ref, k_ref, v_ref, rs_ref, sga_ref, sgb_ref, la_ref):
    xb = x_ref[...].astype(BF16)
    half = D_MODEL // 2

    def seg(c0, n):
        if c0 < N_PROJ_A:
            w = wa_ref[:, c0:c0 + n]
        elif c0 < N_PROJ_A + 2 * D_MODEL:
            w = wb_ref[:, c0 - N_PROJ_A:c0 - N_PROJ_A + n]
        else:
            w = wf_ref[...]
        return jnp.dot(xb, w, preferred_element_type=F32) + b_ref[:, c0:c0 + n]

    for j in range(2):
        a = seg(j * half, half)
        g = seg(D_MODEL + j * half, half)
        u_ref[:, j * half:(j + 1) * half] = (a * _sigmoid(g)).astype(BF16)
    q_ref[...] = (seg(2 * D_MODEL, GLA_DK) * (GLA_HK ** -0.5)).astype(BF16)
    k_ref[...] = seg(2 * D_MODEL + GLA_DK, GLA_DK).astype(BF16)
    for j in range(2):
        v_ref[:, j * half:(j + 1) * half] = seg(3 * D_MODEL + j * half, half).astype(BF16)
    for j in range(2):
        r = seg(4 * D_MODEL + j * half, half)
        rs_ref[:, j * half:(j + 1) * half] = (r * _sigmoid(r)).astype(BF16)
    for j in range(2):
        sga_ref[:, j * half:(j + 1) * half] = _sigmoid(
            seg(5 * D_MODEL + j * half, half)).astype(BF16)
    for j in range(2):
        sgb_ref[:, j * half:(j + 1) * half] = _sigmoid(
            seg(6 * D_MODEL + j * half, half)).astype(BF16)
    f = seg(7 * D_MODEL, F_PAD)
    z = jnp.dot(f.astype(BF16), wgu_ref[...], preferred_element_type=F32) + bgu_ref[...]
    la_ref[...] = (jnp.minimum(z, 0.0) - jnp.log(1.0 + jnp.exp(-jnp.abs(z)))) * (1.0 / GATE_TAU)


def _inproj(x2, w_a, w_b, w_f, b_p, wgu_p, bgu):
    tm = TM_PROJ
    row = lambda i: (i, 0)
    fixed = lambda i: (0, 0)
    tok = lambda n, dt: jax.ShapeDtypeStruct((N_TOK, n), dt)
    return pl.pallas_call(
        _inproj_kernel,
        grid=(N_TOK // tm,),
        in_specs=[
            pl.BlockSpec((tm, D_MODEL), row),
            pl.BlockSpec((D_MODEL, N_PROJ_A), fixed, pipeline_mode=pl.Buffered(1)),
            pl.BlockSpec((D_MODEL, 2 * D_MODEL), fixed, pipeline_mode=pl.Buffered(1)),
            pl.BlockSpec((D_MODEL, F_PAD), fixed),
            pl.BlockSpec((1, N_PROJ), fixed),
            pl.BlockSpec((F_PAD, GLA_DK), fixed),
            pl.BlockSpec((1, GLA_DK), fixed),
        ],
        out_specs=[
            pl.BlockSpec((tm, D_MODEL), row),
            pl.BlockSpec((tm, GLA_DK), row),
            pl.BlockSpec((tm, GLA_DK), row),
            pl.BlockSpec((tm, GLA_DV), row),
            pl.BlockSpec((tm, GLA_DV), row),
            pl.BlockSpec((tm, D_MODEL), row),
            pl.BlockSpec((tm, D_MODEL), row),
            pl.BlockSpec((tm, GLA_DK), row),
        ],
        out_shape=[
            tok(D_MODEL, BF16), tok(GLA_DK, BF16), tok(GLA_DK, BF16), tok(GLA_DV, BF16),
            tok(GLA_DV, BF16), tok(D_MODEL, BF16), tok(D_MODEL, BF16), tok(GLA_DK, F32),
        ],
        compiler_params=pltpu.CompilerParams(
            dimension_semantics=("arbitrary",), vmem_limit_bytes=VMEM_LIMIT),
        name="inproj",
    )(x2, w_a, w_b, w_f, b_p, wgu_p, bgu)


def _conv_kernel(u_ref, tail_ref, shift_ref, cw_ref, cb_ref, g_ref, b_ref, c_ref,
                 ext_ref, shf_ref):
    s = pl.program_id(1)
    tail = tail_ref[...]
    ext_ref[0:CONV_HALO, :] = jnp.where(s == 0, jnp.zeros_like(tail), tail)
    ext_ref[CONV_HALO:CONV_HALO + TS_CONV, :] = u_ref[...]
    ext_ref[CONV_HALO + TS_CONV:, :] = jnp.zeros(
        (CONV_WIN - CONV_SUB - CONV_HALO, D_MODEL), BF16)
    off = CONV_HALO - (CONV_WIDTH - 1)
    for sb in range(TS_CONV // CONV_SUB):
        base = sb * CONV_SUB
        shf_ref[...] = jnp.dot(shift_ref[...], ext_ref[base:base + CONV_WIN, :],
                               preferred_element_type=F32)
        n_sub = CONV_ROWS // SUBLANES
        for i in range(CONV_SUB // CONV_ROWS):
            r0 = i * CONV_ROWS
            acc = [jnp.zeros((SUBLANES, D_MODEL), F32) for _ in range(n_sub)]
            for j in range(CONV_WIDTH):
                l0 = r0 + off + j
                b = l0 % SUBLANES
                m0 = b * CONV_SHROWS + (l0 - b)
                wj = cw_ref[j * SUBLANES:(j + 1) * SUBLANES, :]
                for k in range(n_sub):
                    rk = m0 + k * SUBLANES
                    acc[k] = acc[k] + shf_ref[rk:rk + SUBLANES, :] * wj
            y = _layer_norm(jnp.concatenate(acc, axis=0) + cb_ref[...], g_ref[...], b_ref[...])
            c_ref[base + r0:base + r0 + CONV_ROWS, :] = (y * _sigmoid(y)).astype(BF16)


def _conv(u, shift, cw_p, cb, g, b):
    ts = TS_CONV
    n_s = SEQ // ts
    per_tile = ts // CONV_HALO
    fixed = lambda bi, si: (0, 0)
    return pl.pallas_call(
        _conv_kernel,
        grid=(BATCH, n_s),
        in_specs=[
            pl.BlockSpec((ts, D_MODEL), lambda bi, si: (bi * n_s + si, 0)),
            pl.BlockSpec((CONV_HALO, D_MODEL),
                         lambda bi, si: (jnp.maximum((bi * n_s + si) * per_tile - 1, 0), 0)),
            pl.BlockSpec((SUBLANES * CONV_SHROWS, CONV_WIN), fixed),
            pl.BlockSpec((CONV_WIDTH * SUBLANES, D_MODEL), fixed),
            pl.BlockSpec((1, D_MODEL), fixed),
            pl.BlockSpec((1, D_MODEL), fixed),
            pl.BlockSpec((1, D_MODEL), fixed),
        ],
        out_specs=pl.BlockSpec((ts, D_MODEL), lambda bi, si: (bi * n_s + si, 0)),
        out_shape=jax.ShapeDtypeStruct((N_TOK, D_MODEL), BF16),
        scratch_shapes=[
            pltpu.VMEM((ts + CONV_WIN - CONV_SUB, D_MODEL), BF16),
            pltpu.VMEM((SUBLANES * CONV_SHROWS, D_MODEL), F32),
        ],
        compiler_params=pltpu.CompilerParams(
            dimension_semantics=("arbitrary", "arbitrary"), vmem_limit_bytes=VMEM_LIMIT),
        name="conv",
    )(u, u, shift, cw_p, cb, g, b)


def _gla_kernel(q_ref, k_ref, v_ref, la_ref, rs_ref, ng_ref, o_ref, st_ref):
    s = pl.program_id(1)

    @pl.when(s == 0)
    def _():
        st_ref[...] = jnp.zeros_like(st_ref)

    rr = lax.broadcasted_iota(jnp.int32, (CHUNK, CHUNK), 0)
    cc = lax.broadcasted_iota(jnp.int32, (CHUNK, CHUNK), 1)
    tri = jnp.where(rr >= cc, 1.0, 0.0).astype(BF16)

    def chunk(c, carry):
        r0 = pl.multiple_of(c * CHUNK, CHUNK)
        rows = pl.ds(r0, CHUNK)
        for nb in range(NB_GLA):
            la = la_ref[nb, rows, :]
            hi = la.astype(BF16)
            lo = (la - hi.astype(F32)).astype(BF16)
            cum = (jnp.dot(tri, hi, preferred_element_type=F32)
                   + jnp.dot(tri, lo, preferred_element_type=F32))
            cend = cum[CHUNK - 1:CHUNK, :]
            kd = (k_ref[nb, rows, :].astype(F32) * jnp.exp(cend - cum)).astype(BF16)
            dec = jnp.exp(cend)
            qc = q_ref[nb, rows, :]
            vc = v_ref[nb, rows, :]
            for h in range(GLA_HEADS):
                ks = slice(h * GLA_HK, (h + 1) * GLA_HK)
                vs = slice(h * GLA_HV, (h + 1) * GLA_HV)
                kv_t = lax.dot_general(vc[:, vs], kd[:, ks], (((0,), (0,)), ((), ())),
                                       preferred_element_type=F32)
                st = st_ref[nb * GLA_HEADS + h] * dec[:, ks] + kv_t
                st_ref[nb * GLA_HEADS + h] = st
                o = lax.dot_general(qc[:, ks], st.astype(BF16), (((1,), (1,)), ((), ())),
                                    preferred_element_type=F32)
                ms = jnp.mean(o * o, axis=-1, keepdims=True)
                on = o * lax.rsqrt(ms + RMS_EPS) * ng_ref[:, vs]
                o_ref[nb, rows, vs] = (on * rs_ref[nb, rows, vs].astype(F32)).astype(BF16)
        return carry

    lax.fori_loop(0, TS_GLA // CHUNK, chunk, 0)


def _gla(q, k, v, la, rs, ng):
    ts = TS_GLA
    blk = lambda n: pl.BlockSpec((NB_GLA, ts, n), lambda bi, si: (bi, si, 0))
    seq = lambda a: a.reshape(BATCH, SEQ, a.shape[-1])
    out = pl.pallas_call(
        _gla_kernel,
        grid=(BATCH // NB_GLA, SEQ // ts),
        in_specs=[blk(GLA_DK), blk(GLA_DK), blk(GLA_DV), blk(GLA_DK), blk(GLA_DV),
                  pl.BlockSpec((1, GLA_DV), lambda bi, si: (0, 0))],
        out_specs=blk(GLA_DV),
        out_shape=jax.ShapeDtypeStruct((BATCH, SEQ, GLA_DV), BF16),
        scratch_shapes=[pltpu.VMEM((NB_GLA * GLA_HEADS, GLA_HV, GLA_HK), F32)],
        compiler_params=pltpu.CompilerParams(
            dimension_semantics=("arbitrary", "arbitrary"), vmem_limit_bytes=VMEM_LIMIT),
        name="gla",
    )(seq(q), seq(k), seq(v), seq(la), seq(rs), ng)
    return out.reshape(N_TOK, GLA_DV)


def _mix_kernel(c_ref, g_ref, sga_ref, sgb_ref, x_ref, wco_ref, bco_ref, wgl_ref, wo_ref, bo_ref,
                l1g_ref, l1b_ref, wrh_ref, wrl_ref, br_ref, tri_ref,
                x1_ref, route_ref, meta_ref, cnt_ref, carry_ref):
    i = pl.program_id(0)

    @pl.when(i == 0)
    def _():
        carry_ref[...] = jnp.zeros_like(carry_ref)

    yc = jnp.dot(c_ref[...], wco_ref[...], preferred_element_type=F32) + bco_ref[...]
    yg = jnp.dot(g_ref[...], wgl_ref[...], preferred_element_type=F32)
    merged = sga_ref[...].astype(F32) * yc + sgb_ref[...].astype(F32) * yg
    mix = jnp.dot(merged.astype(BF16), wo_ref[...], preferred_element_type=F32) + bo_ref[...]
    x1 = _layer_norm(DEEPNORM_ALPHA * x_ref[...] + mix, l1g_ref[...], l1b_ref[...])
    _to_token_tiles(x1_ref, x1)

    xh = x1.astype(BF16)
    xl = (x1 - xh.astype(F32)).astype(BF16)
    lg = (jnp.dot(xh, wrh_ref[...], preferred_element_type=F32)
          + jnp.dot(xl, wrh_ref[...], preferred_element_type=F32)
          + jnp.dot(xh, wrl_ref[...], preferred_element_type=F32)) + br_ref[...]

    tm = lg.shape[0]
    lane = lax.broadcasted_iota(jnp.int32, (tm, LANE), 1)
    neg = -jnp.inf
    big = jnp.int32(1 << 20)
    is_g = (lane >= N_EXPERTS) & (lane < N_EXPERTS + N_GROUPS)
    gl = jnp.where(is_g, lg, neg)
    gmax = jnp.max(gl, axis=-1, keepdims=True)
    gsel = jnp.min(jnp.where(gl == gmax, lane, big), axis=-1, keepdims=True) - N_EXPERTS
    gw = 1.0 / jnp.sum(jnp.exp(gl - gmax), axis=-1, keepdims=True)
    in_grp = (lane < N_EXPERTS) & ((lane >> 3) == gsel)
    el = jnp.where(in_grp, lg, neg)
    m1 = jnp.max(el, axis=-1, keepdims=True)
    i1 = jnp.min(jnp.where(el == m1, lane, big), axis=-1, keepdims=True)
    el2 = jnp.where(lane == i1, neg, el)
    m2 = jnp.max(el2, axis=-1, keepdims=True)
    i2 = jnp.min(jnp.where(el2 == m2, lane, big), axis=-1, keepdims=True)
    t = jnp.exp(m2 - m1)
    w1 = gw / (1.0 + t)
    w2 = gw * t / (1.0 + t)

    oh1 = lane == i1
    oh2 = lane == i2
    osum = jnp.where(oh1 | oh2, 1.0, 0.0)
    excl = jnp.dot(tri_ref[...], osum.astype(BF16), preferred_element_type=F32) + carry_ref[...]
    rank1 = jnp.sum(jnp.where(oh1, excl, 0.0), axis=-1, keepdims=True)
    rank2 = jnp.sum(jnp.where(oh2, excl, 0.0), axis=-1, keepdims=True)
    new_cnt = carry_ref[...] + jnp.sum(osum, axis=0, keepdims=True)
    carry_ref[...] = new_cnt
    cnt_ref[...] = new_cnt

    route = jnp.where(lane == 0, i1.astype(F32), 0.0)
    route = jnp.where(lane == 1, i2.astype(F32), route)
    route = jnp.where(lane == 2, rank1, route)
    route = jnp.where(lane == 3, rank2, route)
    route = jnp.where(lane == 4, w1, route)
    route = jnp.where(lane == 5, w2, route)
    route_ref[...] = route
    meta_ref[...] = route.T[0:META_ROWS, :].astype(jnp.int32)


def _mix(c, g, sga, sgb, x2, wco, bco, wgl, wo, bo, l1g, l1b, wrh, wrl, br, tri):
    tm = TM_MIX
    row = lambda i: (i, 0)
    fixed = lambda i: (0, 0)
    mat = pl.BlockSpec((D_MODEL, D_MODEL), fixed, pipeline_mode=pl.Buffered(1))
    vec = pl.BlockSpec((1, D_MODEL), fixed)
    return pl.pallas_call(
        _mix_kernel,
        grid=(N_TOK // tm,),
        in_specs=[
            pl.BlockSpec((tm, D_MODEL), row), pl.BlockSpec((tm, D_MODEL), row),
            pl.BlockSpec((tm, D_MODEL), row), pl.BlockSpec((tm, D_MODEL), row),
            pl.BlockSpec((tm, D_MODEL), row),
            mat, vec, mat, mat, vec, vec, vec,
            pl.BlockSpec((D_MODEL, LANE), fixed), pl.BlockSpec((D_MODEL, LANE), fixed),
            pl.BlockSpec((1, LANE), fixed),
            pl.BlockSpec((tm, tm), fixed),
        ],
        out_specs=[
            pl.BlockSpec((tm * TOKEN_ROWS, LANE), row),
            pl.BlockSpec((tm, LANE), row),
            pl.BlockSpec((META_ROWS, tm), lambda i: (0, i)),
            pl.BlockSpec((1, LANE), fixed),
        ],
        out_shape=[
            jax.ShapeDtypeStruct((N_TOK * TOKEN_ROWS, LANE), F32),
            jax.ShapeDtypeStruct((N_TOK, LANE), F32),
            jax.ShapeDtypeStruct((META_ROWS, N_TOK), jnp.int32),
            jax.ShapeDtypeStruct((1, LANE), F32),
        ],
        scratch_shapes=[pltpu.VMEM((1, LANE), F32)],
        compiler_params=pltpu.CompilerParams(
            dimension_semantics=("arbitrary",), vmem_limit_bytes=VMEM_LIMIT),
        name="mix",
    )(c, g, sga, sgb, x2, wco, bco, wgl, wo, bo, l1g, l1b, wrh, wrl, br, tri)


def _dispatch_kernel(e0_ref, e1_ref, r0_ref, r1_ref, cnt_ref, x1_ref,
                     xs_ref, d0_ref, d1_ref, ord_ref, oe_ref, nums_ref,
                     pstart_ref, zero_ref, sem, zsem):
    i = pl.program_id(0)

    def zero_copy(b):
        rows = BM * TOKEN_ROWS
        return pltpu.make_async_copy(zero_ref, xs_ref.at[pl.ds(b * rows, rows)], zsem)

    @pl.when(i == 0)
    def _():
        zero_ref[...] = jnp.zeros_like(zero_ref)

        def plan(e, carry):
            bstart, n_ord = carry
            nb = (cnt_ref[e] + (BM - 1)) // BM
            pstart_ref[e] = bstart * BM

            def fill(j, c):
                ord_ref[bstart + j] = n_ord
                return c

            lax.fori_loop(0, nb, fill, 0)

            @pl.when(nb > 0)
            def _():
                oe_ref[n_ord] = e
                zero_copy(bstart + nb - 1).start()

            return bstart + nb, n_ord + jnp.where(nb > 0, 1, 0)

        n_used, n_ord = lax.fori_loop(0, N_EXPERTS, plan, (jnp.int32(0), jnp.int32(0)))
        nums_ref[0] = n_used
        nums_ref[1] = n_ord

        def fill_rest(j, c):
            ord_ref[j] = n_ord - 1
            zero_copy(j).start()
            return c

        lax.fori_loop(n_used, N_BLOCKS, fill_rest, 0)

        def fill_oe(j, c):
            oe_ref[j] = 0
            return c

        lax.fori_loop(n_ord, N_EXPERTS, fill_oe, 0)

        def drain(j, c):
            zero_copy(0).wait()
            return c

        lax.fori_loop(0, n_ord + (N_BLOCKS - n_used), drain, 0)

    base = i * TD_DISP

    def tok_group(tg, carry):
        for uu in range(DISP_UNROLL):
            tt = tg * DISP_UNROLL + uu
            t = base + tt
            src = x1_ref.at[pl.ds(pl.multiple_of(tt * TOKEN_ROWS, TOKEN_ROWS), TOKEN_ROWS)]
            for e_ref, r_ref, d_ref in ((e0_ref, r0_ref, d0_ref), (e1_ref, r1_ref, d1_ref)):
                d = pstart_ref[e_ref[t]] + r_ref[t]
                d_ref[t] = d
                pltpu.make_async_copy(
                    src, xs_ref.at[pl.ds(pl.multiple_of(d * TOKEN_ROWS, TOKEN_ROWS), TOKEN_ROWS)],
                    sem).start()
        return carry

    lax.fori_loop(0, TD_DISP // DISP_UNROLL, tok_group, 0)
    for kk in range(TOP_K):
        pltpu.make_async_copy(x1_ref, xs_ref.at[pl.ds(0, TD_DISP * TOKEN_ROWS)], sem).wait()


def _dispatch(e0, e1, r0, r1, cnt, x1):
    smem = pl.BlockSpec(memory_space=pltpu.SMEM)
    grid_spec = pltpu.PrefetchScalarGridSpec(
        num_scalar_prefetch=5,
        grid=(N_TOK // TD_DISP,),
        in_specs=[pl.BlockSpec((TD_DISP * TOKEN_ROWS, LANE), lambda i, *_: (i, 0))],
        out_specs=[pl.BlockSpec(memory_space=pl.ANY), smem, smem, smem, smem, smem],
        scratch_shapes=[
            pltpu.SMEM((N_EXPERTS,), jnp.int32),
            pltpu.VMEM((BM * TOKEN_ROWS, LANE), F32),
            pltpu.SemaphoreType.DMA,
            pltpu.SemaphoreType.DMA,
        ],
    )
    return pl.pallas_call(
        _dispatch_kernel,
        grid_spec=grid_spec,
        out_shape=[
            jax.ShapeDtypeStruct((N_SLOTS * TOKEN_ROWS, LANE), F32),
            jax.ShapeDtypeStruct((N_TOK,), jnp.int32),
            jax.ShapeDtypeStruct((N_TOK,), jnp.int32),
            jax.ShapeDtypeStruct((N_BLOCKS,), jnp.int32),
            jax.ShapeDtypeStruct((N_EXPERTS,), jnp.int32),
            jax.ShapeDtypeStruct((2,), jnp.int32),
        ],
        compiler_params=pltpu.CompilerParams(
            dimension_semantics=("arbitrary",), vmem_limit_bytes=VMEM_LIMIT),
        name="dispatch",
    )(e0, e1, r0, r1, cnt, x1)


def _expert_kernel(ord_ref, oe_ref, nums_ref, xs_ref, w1_hbm, w3_hbm, w2_hbm, ys_ref,
                   w1f_ref, w3f_ref, w2f_ref, w1b_ref, w3b_ref, w2b_ref, sem):
    i = pl.program_id(0)
    n_used = nums_ref[0]
    n_ord = nums_ref[1]
    active = i < n_used
    k = ord_ref[i]
    first = (i == 0) | (k != ord_ref[jnp.maximum(i - 1, 0)])

    def weight_copies(kk):
        e = oe_ref[kk]
        slot = kk % 2
        return [pltpu.make_async_copy(w_hbm.at[e], wf_ref.at[slot], sem.at[slot])
                for w_hbm, wf_ref in ((w1_hbm, w1f_ref), (w3_hbm, w3f_ref), (w2_hbm, w2f_ref))]

    @pl.when(active & first)
    def _():
        @pl.when(i == 0)
        def _():
            for cp in weight_copies(k):
                cp.start()

        @pl.when(k + 1 < n_ord)
        def _():
            for cp in weight_copies(k + 1):
                cp.start()

        for cp in weight_copies(k):
            cp.wait()
        slot = k % 2
        w1b_ref[...] = w1f_ref[slot].astype(BF16)
        w3b_ref[...] = w3f_ref[slot].astype(BF16)
        w2b_ref[...] = w2f_ref[slot].astype(BF16)

    @pl.when(active)
    def _():
        xb = _from_token_tiles(xs_ref, BM).astype(BF16)
        h1 = jnp.dot(xb, w1b_ref[...], preferred_element_type=F32)
        h3 = jnp.dot(xb, w3b_ref[...], preferred_element_type=F32)
        hdn = (h1 * _sigmoid(h1) * h3).astype(BF16)
        _to_token_tiles(ys_ref, jnp.dot(hdn, w2b_ref[...], preferred_element_type=F32))

    @pl.when(jnp.logical_not(active))
    def _():
        ys_ref[...] = jnp.zeros_like(ys_ref)


def _experts(blk_ord, ord_e, nums, xs, w1, w3, w2):
    last = lambda i, nums_ref: jnp.maximum(jnp.minimum(i, nums_ref[0] - 1), 0)
    hbm = pl.BlockSpec(memory_space=pl.ANY)
    grid_spec = pltpu.PrefetchScalarGridSpec(
        num_scalar_prefetch=3,
        grid=(N_BLOCKS,),
        in_specs=[
            pl.BlockSpec((BM * TOKEN_ROWS, LANE), lambda i, o, oe, nu: (last(i, nu), 0)),
            hbm, hbm, hbm,
        ],
        out_specs=pl.BlockSpec((BM * TOKEN_ROWS, LANE), lambda i, o, oe, nu: (i, 0)),
        scratch_shapes=[
            pltpu.VMEM((2, D_MODEL, D_EXPERT), F32),
            pltpu.VMEM((2, D_MODEL, D_EXPERT), F32),
            pltpu.VMEM((2, D_EXPERT, D_MODEL), F32),
            pltpu.VMEM((D_MODEL, D_EXPERT), BF16),
            pltpu.VMEM((D_MODEL, D_EXPERT), BF16),
            pltpu.VMEM((D_EXPERT, D_MODEL), BF16),
            pltpu.SemaphoreType.DMA((2,)),
        ],
    )
    return pl.pallas_call(
        _expert_kernel,
        grid_spec=grid_spec,
        out_shape=jax.ShapeDtypeStruct((N_SLOTS * TOKEN_ROWS, LANE), F32),
        compiler_params=pltpu.CompilerParams(
            dimension_semantics=("arbitrary",), vmem_limit_bytes=VMEM_LIMIT),
        name="experts",
    )(blk_ord, ord_e, nums, xs, w1, w3, w2)


def _combine_kernel(d0_ref, d1_ref, route_ref, x1_ref, ys_ref, g_ref, b_ref, o_ref, y_ref, sem):
    i = pl.program_id(0)
    n_tiles = pl.num_programs(0)
    tile_rows = TC_COMB * TOKEN_ROWS

    def issue(tile, slot):
        base = tile * TC_COMB

        def group(tg, carry):
            for uu in range(COMB_UNROLL):
                t = tg * COMB_UNROLL + uu
                dst = pl.ds(pl.multiple_of(t * TOKEN_ROWS, TOKEN_ROWS), TOKEN_ROWS)
                for kk, d_ref in enumerate((d0_ref, d1_ref)):
                    src = pl.ds(pl.multiple_of(d_ref[base + t] * TOKEN_ROWS, TOKEN_ROWS),
                                TOKEN_ROWS)
                    pltpu.make_async_copy(ys_ref.at[src], y_ref.at[slot, kk, dst],
                                          sem.at[slot]).start()
            return carry

        lax.fori_loop(0, TC_COMB // COMB_UNROLL, group, 0)

    @pl.when(i == 0)
    def _():
        issue(0, 0)

    @pl.when(i + 1 < n_tiles)
    def _():
        issue(i + 1, (i + 1) % 2)

    slot = i % 2
    for kk in range(TOP_K):
        pltpu.make_async_copy(ys_ref.at[pl.ds(0, tile_rows)], y_ref.at[slot, kk],
                              sem.at[slot]).wait()
    w0 = route_ref[:, 4:5]
    w1 = route_ref[:, 5:6]
    z = (DEEPNORM_ALPHA * _from_token_tiles(x1_ref, TC_COMB)
         + w0 * _from_token_tiles(y_ref.at[slot, 0], TC_COMB)
         + w1 * _from_token_tiles(y_ref.at[slot, 1], TC_COMB))
    o_ref[...] = _layer_norm(z, g_ref[...], b_ref[...])


def _combine(d0, d1, route, x1, ys, g, b):
    tc = TC_COMB
    grid_spec = pltpu.PrefetchScalarGridSpec(
        num_scalar_prefetch=2,
        grid=(N_TOK // tc,),
        in_specs=[
            pl.BlockSpec((tc, LANE), lambda i, *_: (i, 0)),
            pl.BlockSpec((tc * TOKEN_ROWS, LANE), lambda i, *_: (i, 0)),
            pl.BlockSpec(memory_space=pl.ANY),
            pl.BlockSpec((1, D_MODEL), lambda i, *_: (0, 0)),
            pl.BlockSpec((1, D_MODEL), lambda i, *_: (0, 0)),
        ],
        out_specs=pl.BlockSpec((tc, D_MODEL), lambda i, *_: (i, 0)),
        scratch_shapes=[
            pltpu.VMEM((2, TOP_K, tc * TOKEN_ROWS, LANE), F32),
            pltpu.SemaphoreType.DMA((2,)),
        ],
    )
    return pl.pallas_call(
        _combine_kernel,
        grid_spec=grid_spec,
        out_shape=jax.ShapeDtypeStruct((N_TOK, D_MODEL), F32),
        compiler_params=pltpu.CompilerParams(
            dimension_semantics=("arbitrary",), vmem_limit_bytes=VMEM_LIMIT),
        name="combine",
    )(d0, d1, route, x1, ys, g, b)


def _split_bf16(w):
    hi = w.astype(BF16)
    lo = (w - hi.astype(F32)).astype(BF16)
    return hi, lo


def kernel(x, w_in, b_in, conv_w, conv_b, conv_ln_g, conv_ln_b, w_conv_out, b_conv_out, w_gate_up, b_gate_up, gla_norm_g, w_gla_out, w_out, b_out, ln1_g, ln1_b, w_router_group, b_router_group, w_router_expert, b_router_expert, w1, w3, w2, ln2_g, ln2_b):
    x2 = x.reshape(N_TOK, D_MODEL)
    row = lambda v: v.reshape(1, -1)
    for l in range(w_in.shape[0]):
        f0 = N_PROJ_A
        f1 = f0 + GATE_RANK
        w_l, b_l = w_in[l], b_in[l]
        w_a = w_l[:, :f0].astype(BF16)
        w_b = w_l[:, f1:].astype(BF16)
        w_f = jnp.pad(w_l[:, f0:f1], ((0, 0), (0, F_PAD - GATE_RANK))).astype(BF16)
        b_p = row(jnp.concatenate(
            [b_l[:f0], b_l[f1:], b_l[f0:f1], jnp.zeros((F_PAD - GATE_RANK,), F32)]))
        wgu_p = jnp.concatenate(
            [w_gate_up[l], jnp.zeros((F_PAD - GATE_RANK, GLA_DK), F32)], axis=0).astype(BF16)
        u, q, k, v, rs, sga, sgb, la = _inproj(x2, w_a, w_b, w_f, b_p, wgu_p, row(b_gate_up[l]))

        cw_p = jnp.repeat(conv_w[l], SUBLANES, axis=0)
        sr = lax.broadcasted_iota(jnp.int32, (SUBLANES * CONV_SHROWS, CONV_WIN), 0)
        sc = lax.broadcasted_iota(jnp.int32, (SUBLANES * CONV_SHROWS, CONV_WIN), 1)
        shift = (sc == sr % CONV_SHROWS + sr // CONV_SHROWS).astype(BF16)
        c = _conv(u, shift, cw_p, row(conv_b[l]), row(conv_ln_g[l]), row(conv_ln_b[l]))
        g = _gla(q, k, v, la, rs, row(gla_norm_g[l]))

        w_r = jnp.concatenate(
            [w_router_expert[l], w_router_group[l],
             jnp.zeros((D_MODEL, LANE - N_EXPERTS - N_GROUPS), F32)], axis=1)
        b_r = row(jnp.concatenate(
            [b_router_expert[l], b_router_group[l],
             jnp.zeros((LANE - N_EXPERTS - N_GROUPS,), F32)]))
        wrh, wrl = _split_bf16(w_r)
        ri = lax.broadcasted_iota(jnp.int32, (TM_MIX, TM_MIX), 0)
        ci = lax.broadcasted_iota(jnp.int32, (TM_MIX, TM_MIX), 1)
        tri = (ri > ci).astype(BF16)
        x1, route, meta, cnt = _mix(
            c, g, sga, sgb, x2, w_conv_out[l].astype(BF16), row(b_conv_out[l]),
            w_gla_out[l].astype(BF16), w_out[l].astype(BF16), row(b_out[l]),
            row(ln1_g[l]), row(ln1_b[l]), wrh, wrl, b_r, tri)

        cnt_i = cnt[0, :N_EXPERTS].astype(jnp.int32)
        xs, d0, d1, blk_ord, ord_e, nums = _dispatch(
            meta[0], meta[1], meta[2], meta[3], cnt_i, x1)
        ys = _experts(blk_ord, ord_e, nums, xs, w1[l], w3[l], w2[l])
        x2 = _combine(d0, d1, route, x1, ys, row(ln2_g[l]), row(ln2_b[l]))
    return x2.reshape(x.shape)
```

```python
import jax
import jax.numpy as jnp
from jax import lax
from jax.experimental import pallas as pl
from jax.experimental.pallas import tpu as pltpu

F32 = jnp.float32
BF16 = jnp.bfloat16

D_MODEL = 1024
BATCH = 8
SEQ = 2048
N_TOK = BATCH * SEQ
CHUNK = 64
CONV_WIDTH = 31
GLA_HEADS = 4
GLA_DK = 512
GLA_DV = 1024
GLA_HK = 128
GLA_HV = 256
GATE_RANK = 16
GATE_TAU = 16.0
N_GROUPS = 8
EXPERTS_PER_GROUP = 8
N_EXPERTS = 64
TOP_K = 2
D_EXPERT = 512
LN_EPS = 1e-5
RMS_EPS = 1e-6
DEEPNORM_ALPHA = 2.0 ** 0.25

LANE = 128
TOKEN_ROWS = D_MODEL // LANE
F_PAD = LANE
N_PROJ_A = 2 * D_MODEL + 2 * GLA_DK + 2 * GLA_DV
N_PROJ = N_PROJ_A + 2 * D_MODEL + F_PAD
TM_PROJ = 512
CONV_HALO = 32
CONV_ROWS = 32
SUBLANES = 8
CONV_SUB = 128
CONV_WIN = 256
CONV_SHROWS = CONV_SUB + CONV_HALO
TS_GLA = 256
NB_GLA = 4
TM_MIX = 512
META_ROWS = 8
BM = 256
N_BLOCKS = N_TOK * TOP_K // BM + N_EXPERTS
N_SLOTS = N_BLOCKS * BM
TC_COMB = 256
COMB_UNROLL = 4
TD_DISP = 2048
DISP_UNROLL = 8
VMEM_LIMIT = 56 * 1024 * 1024


def _sigmoid(x):
    return 1.0 / (1.0 + jnp.exp(-x))


def _to_token_tiles(ref, val):
    n = val.shape[0]
    for c in range(TOKEN_ROWS):
        ref[pl.ds(c, n, stride=TOKEN_ROWS), :] = val[:, c * LANE:(c + 1) * LANE]


def _from_token_tiles(ref, n):
    return jnp.concatenate(
        [ref[pl.ds(c, n, stride=TOKEN_ROWS), :] for c in range(TOKEN_ROWS)], axis=1)


def _layer_norm(z, g, b):
    mu = jnp.mean(z, axis=-1, keepdims=True)
    zc = z - mu
    var = jnp.mean(zc * zc, axis=-1, keepdims=True)
    return zc * lax.rsqrt(var + LN_EPS) * g + b


def _conv_block(ext_ref, shf_ref, shift_ref, cw_ref, cb_ref, g_ref, b_ref, c_ref, base):
    off = CONV_HALO - (CONV_WIDTH - 1)
    shf_ref[...] = jnp.dot(shift_ref[...], ext_ref[base:base + CONV_WIN, :],
                           preferred_element_type=F32)
    n_sub = CONV_ROWS // SUBLANES
    for i in range(CONV_SUB // CONV_ROWS):
        r0 = i * CONV_ROWS
        acc = [jnp.zeros((SUBLANES, D_MODEL), F32) for _ in range(n_sub)]
        for j in range(CONV_WIDTH):
            l0 = r0 + off + j
            b = l0 % SUBLANES
            m0 = b * CONV_SHROWS + (l0 - b)
            wj = cw_ref[j * SUBLANES:(j + 1) * SUBLANES, :]
            for k in range(n_sub):
                rk = m0 + k * SUBLANES
                acc[k] = acc[k] + shf_ref[rk:rk + SUBLANES, :] * wj
        y = _layer_norm(jnp.concatenate(acc, axis=0) + cb_ref[...], g_ref[...], b_ref[...])
        c_ref[base + r0:base + r0 + CONV_ROWS, :] = (y * _sigmoid(y)).astype(BF16)


def _inproj_kernel(x_ref, wa_ref, wb_ref, wf_ref, b_ref, wgu_ref, bgu_ref,
                   shift_ref, cw_ref, cb_ref, cg_ref, cbeta_ref,
                   c_ref, q_ref, k_ref, v_ref, rs_ref, sga_ref, sgb_ref, la_ref,
                   ext_ref, shf_ref):
    i = pl.program_id(0)
    xb = x_ref[...].astype(BF16)
    half = D_MODEL // 2

    def seg(c0, n):
        if c0 < N_PROJ_A:
            w = wa_ref[:, c0:c0 + n]
        elif c0 < N_PROJ_A + 2 * D_MODEL:
            w = wb_ref[:, c0 - N_PROJ_A:c0 - N_PROJ_A + n]
        else:
            w = wf_ref[...]
        return jnp.dot(xb, w, preferred_element_type=F32) + b_ref[:, c0:c0 + n]

    @pl.when(i == 0)
    def _():
        ext_ref[...] = jnp.zeros_like(ext_ref)

    prev_tail = ext_ref[TM_PROJ:TM_PROJ + CONV_HALO, :]
    ext_ref[0:CONV_HALO, :] = jnp.where(
        i % (SEQ // TM_PROJ) == 0, jnp.zeros_like(prev_tail), prev_tail)
    for j in range(2):
        a = seg(j * half, half)
        g = seg(D_MODEL + j * half, half)
        ext_ref[CONV_HALO:CONV_HALO + TM_PROJ, j * half:(j + 1) * half] = (
            a * _sigmoid(g)).astype(BF16)

    def conv_block(sb):
        _conv_block(ext_ref, shf_ref.at[sb % 2], shift_ref, cw_ref, cb_ref, cg_ref, cbeta_ref,
                    c_ref, sb * CONV_SUB)

    conv_block(0)
    q_ref[...] = (seg(2 * D_MODEL, GLA_DK) * (GLA_HK ** -0.5)).astype(BF16)
    k_ref[...] = seg(2 * D_MODEL + GLA_DK, GLA_DK).astype(BF16)
    for j in range(2):
        v_ref[:, j * half:(j + 1) * half] = seg(3 * D_MODEL + j * half, half).astype(BF16)
    conv_block(1)
    for j in range(2):
        r = seg(4 * D_MODEL + j * half, half)
        rs_ref[:, j * half:(j + 1) * half] = (r * _sigmoid(r)).astype(BF16)
    for j in range(2):
        sga_ref[:, j * half:(j + 1) * half] = _sigmoid(
            seg(5 * D_MODEL + j * half, half)).astype(BF16)
    conv_block(2)
    for j in range(2):
        sgb_ref[:, j * half:(j + 1) * half] = _sigmoid(
            seg(6 * D_MODEL + j * half, half)).astype(BF16)
    f = seg(7 * D_MODEL, F_PAD)
    z = jnp.dot(f.astype(BF16), wgu_ref[...], preferred_element_type=F32) + bgu_ref[...]
    la_ref[...] = (jnp.minimum(z, 0.0) - jnp.log(1.0 + jnp.exp(-jnp.abs(z)))) * (1.0 / GATE_TAU)
    conv_block(3)


def _inproj(x2, w_a, w_b, w_f, b_p, wgu_p, bgu, shift, cw_p, cb, cg, cbeta):
    tm = TM_PROJ
    row = lambda i: (i, 0)
    fixed = lambda i: (0, 0)
    tok = lambda n, dt: jax.ShapeDtypeStruct((N_TOK, n), dt)
    vec = pl.BlockSpec((1, D_MODEL), fixed)
    return pl.pallas_call(
        _inproj_kernel,
        grid=(N_TOK // tm,),
        in_specs=[
            pl.BlockSpec((tm, D_MODEL), row),
            pl.BlockSpec((D_MODEL, N_PROJ_A), fixed, pipeline_mode=pl.Buffered(1)),
            pl.BlockSpec((D_MODEL, 2 * D_MODEL), fixed, pipeline_mode=pl.Buffered(1)),
            pl.BlockSpec((D_MODEL, F_PAD), fixed),
            pl.BlockSpec((1, N_PROJ), fixed),
            pl.BlockSpec((F_PAD, GLA_DK), fixed),
            pl.BlockSpec((1, GLA_DK), fixed),
            pl.BlockSpec((SUBLANES * CONV_SHROWS, CONV_WIN), fixed),
            pl.BlockSpec((CONV_WIDTH * SUBLANES, D_MODEL), fixed),
            vec, vec, vec,
        ],
        out_specs=[
            pl.BlockSpec((tm, D_MODEL), row),
            pl.BlockSpec((tm, GLA_DK), row),
            pl.BlockSpec((tm, GLA_DK), row),
            pl.BlockSpec((tm, GLA_DV), row),
            pl.BlockSpec((tm, GLA_DV), row),
            pl.BlockSpec((tm, D_MODEL), row),
            pl.BlockSpec((tm, D_MODEL), row),
            pl.BlockSpec((tm, GLA_DK), row),
        ],
        out_shape=[
            tok(D_MODEL, BF16), tok(GLA_DK, BF16), tok(GLA_DK, BF16), tok(GLA_DV, BF16),
            tok(GLA_DV, BF16), tok(D_MODEL, BF16), tok(D_MODEL, BF16), tok(GLA_DK, F32),
        ],
        scratch_shapes=[
            pltpu.VMEM((tm + CONV_WIN - CONV_SUB, D_MODEL), BF16),
            pltpu.VMEM((2, SUBLANES * CONV_SHROWS, D_MODEL), F32),
        ],
        compiler_params=pltpu.CompilerParams(
            dimension_semantics=("arbitrary",), vmem_limit_bytes=VMEM_LIMIT),
        name="inproj",
    )(x2, w_a, w_b, w_f, b_p, wgu_p, bgu, shift, cw_p, cb, cg, cbeta)


def _gla_kernel(q_ref, k_ref, v_ref, la_ref, rs_ref, ng_ref, o_ref, st_ref):
    s = pl.program_id(1)

    @pl.when(s == 0)
    def _():
        st_ref[...] = jnp.zeros_like(st_ref)

    rr = lax.broadcasted_iota(jnp.int32, (CHUNK, CHUNK), 0)
    cc = lax.broadcasted_iota(jnp.int32, (CHUNK, CHUNK), 1)
    tri = jnp.where(rr >= cc, 1.0, 0.0).astype(BF16)

    def chunk(c, carry):
        r0 = pl.multiple_of(c * CHUNK, CHUNK)
        rows = pl.ds(r0, CHUNK)
        for nb in range(NB_GLA):
            la = la_ref[nb, rows, :]
            hi = la.astype(BF16)
            lo = (la - hi.astype(F32)).astype(BF16)
            cum = (jnp.dot(tri, hi, preferred_element_type=F32)
                   + jnp.dot(tri, lo, preferred_element_type=F32))
            cend = cum[CHUNK - 1:CHUNK, :]
            kd = (k_ref[nb, rows, :].astype(F32) * jnp.exp(cend - cum)).astype(BF16)
            dec = jnp.exp(cend)
            qc = q_ref[nb, rows, :]
            vc = v_ref[nb, rows, :]
            for h in range(GLA_HEADS):
                ks = slice(h * GLA_HK, (h + 1) * GLA_HK)
                vs = slice(h * GLA_HV, (h + 1) * GLA_HV)
                kv_t = lax.dot_general(vc[:, vs], kd[:, ks], (((0,), (0,)), ((), ())),
                                       preferred_element_type=F32)
                st = st_ref[nb * GLA_HEADS + h] * dec[:, ks] + kv_t
                st_ref[nb * GLA_HEADS + h] = st
                o = lax.dot_general(qc[:, ks], st.astype(BF16), (((1,), (1,)), ((), ())),
                                    preferred_element_type=F32)
                ms = jnp.mean(o * o, axis=-1, keepdims=True)
                on = o * lax.rsqrt(ms + RMS_EPS) * ng_ref[:, vs]
                o_ref[nb, rows, vs] = (on * rs_ref[nb, rows, vs].astype(F32)).astype(BF16)
        return carry

    lax.fori_loop(0, TS_GLA // CHUNK, chunk, 0)


def _gla(q, k, v, la, rs, ng):
    ts = TS_GLA
    blk = lambda n: pl.BlockSpec((NB_GLA, ts, n), lambda bi, si: (bi, si, 0))
    seq = lambda a: a.reshape(BATCH, SEQ, a.shape[-1])
    out = pl.pallas_call(
        _gla_kernel,
        grid=(BATCH // NB_GLA, SEQ // ts),
        in_specs=[blk(GLA_DK), blk(GLA_DK), blk(GLA_DV), blk(GLA_DK), blk(GLA_DV),
                  pl.BlockSpec((1, GLA_DV), lambda bi, si: (0, 0))],
        out_specs=blk(GLA_DV),
        out_shape=jax.ShapeDtypeStruct((BATCH, SEQ, GLA_DV), BF16),
        scratch_shapes=[pltpu.VMEM((NB_GLA * GLA_HEADS, GLA_HV, GLA_HK), F32)],
        compiler_params=pltpu.CompilerParams(
            dimension_semantics=("arbitrary", "arbitrary"), vmem_limit_bytes=VMEM_LIMIT),
        name="gla",
    )(seq(q), seq(k), seq(v), seq(la), seq(rs), ng)
    return out.reshape(N_TOK, GLA_DV)


def _mix_kernel(c_ref, g_ref, sga_ref, sgb_ref, x_ref, wco_ref, bco_ref, wgl_ref, wo_ref, bo_ref,
                l1g_ref, l1b_ref, wrh_ref, wrl_ref, br_ref, tri_ref,
                x1_ref, route_ref, meta_ref, cnt_ref, carry_ref):
    i = pl.program_id(0)

    @pl.when(i == 0)
    def _():
        carry_ref[...] = jnp.zeros_like(carry_ref)

    yc = jnp.dot(c_ref[...], wco_ref[...], preferred_element_type=F32) + bco_ref[...]
    yg = jnp.dot(g_ref[...], wgl_ref[...], preferred_element_type=F32)
    merged = sga_ref[...].astype(F32) * yc + sgb_ref[...].astype(F32) * yg
    mix = jnp.dot(merged.astype(BF16), wo_ref[...], preferred_element_type=F32) + bo_ref[...]
    x1 = _layer_norm(DEEPNORM_ALPHA * x_ref[...] + mix, l1g_ref[...], l1b_ref[...])
    _to_token_tiles(x1_ref, x1)

    xh = x1.astype(BF16)
    xl = (x1 - xh.astype(F32)).astype(BF16)
    lg = (jnp.dot(xh, wrh_ref[...], preferred_element_type=F32)
          + jnp.dot(xl, wrh_ref[...], preferred_element_type=F32)
          + jnp.dot(xh, wrl_ref[...], preferred_element_type=F32)) + br_ref[...]

    tm = lg.shape[0]
    lane = lax.broadcasted_iota(jnp.int32, (tm, LANE), 1)
    neg = -jnp.inf
    big = jnp.int32(1 << 20)
    is_g = (lane >= N_EXPERTS) & (lane < N_EXPERTS + N_GROUPS)
    gl = jnp.where(is_g, lg, neg)
    gmax = jnp.max(gl, axis=-1, keepdims=True)
    gsel = jnp.min(jnp.where(gl == gmax, lane, big), axis=-1, keepdims=True) - N_EXPERTS
    gw = 1.0 / jnp.sum(jnp.exp(gl - gmax), axis=-1, keepdims=True)
    in_grp = (lane < N_EXPERTS) & ((lane >> 3) == gsel)
    el = jnp.where(in_grp, lg, neg)
    m1 = jnp.max(el, axis=-1, keepdims=True)
    i1 = jnp.min(jnp.where(el == m1, lane, big), axis=-1, keepdims=True)
    el2 = jnp.where(lane == i1, neg, el)
    m2 = jnp.max(el2, axis=-1, keepdims=True)
    i2 = jnp.min(jnp.where(el2 == m2, lane, big), axis=-1, keepdims=True)
    t = jnp.exp(m2 - m1)
    w1 = gw / (1.0 + t)
    w2 = gw * t / (1.0 + t)

    oh1 = lane == i1
    oh2 = lane == i2
    osum = jnp.where(oh1 | oh2, 1.0, 0.0)
    excl = jnp.dot(tri_ref[...], osum.astype(BF16), preferred_element_type=F32) + carry_ref[...]
    rank1 = jnp.sum(jnp.where(oh1, excl, 0.0), axis=-1, keepdims=True)
    rank2 = jnp.sum(jnp.where(oh2, excl, 0.0), axis=-1, keepdims=True)
    new_cnt = carry_ref[...] + jnp.sum(osum, axis=0, keepdims=True)
    carry_ref[...] = new_cnt
    cnt_ref[...] = new_cnt

    route = jnp.where(lane == 0, i1.astype(F32), 0.0)
    route = jnp.where(lane == 1, i2.astype(F32), route)
    route = jnp.where(lane == 2, rank1, route)
    route = jnp.where(lane == 3, rank2, route)
    route = jnp.where(lane == 4, w1, route)
    route = jnp.where(lane == 5, w2, route)
    route_ref[...] = route
    meta_ref[...] = route.T[0:META_ROWS, :].astype(jnp.int32)


def _mix(c, g, sga, sgb, x2, wco, bco, wgl, wo, bo, l1g, l1b, wrh, wrl, br, tri):
    tm = TM_MIX
    row = lambda i: (i, 0)
    fixed = lambda i: (0, 0)
    mat = pl.BlockSpec((D_MODEL, D_MODEL), fixed, pipeline_mode=pl.Buffered(1))
    vec = pl.BlockSpec((1, D_MODEL), fixed)
    return pl.pallas_call(
        _mix_kernel,
        grid=(N_TOK // tm,),
        in_specs=[
            pl.BlockSpec((tm, D_MODEL), row), pl.BlockSpec((tm, D_MODEL), row),
            pl.BlockSpec((tm, D_MODEL), row), pl.BlockSpec((tm, D_MODEL), row),
            pl.BlockSpec((tm, D_MODEL), row),
            mat, vec, mat, mat, vec, vec, vec,
            pl.BlockSpec((D_MODEL, LANE), fixed), pl.BlockSpec((D_MODEL, LANE), fixed),
            pl.BlockSpec((1, LANE), fixed),
            pl.BlockSpec((tm, tm), fixed),
        ],
        out_specs=[
            pl.BlockSpec((tm * TOKEN_ROWS, LANE), row),
            pl.BlockSpec((tm, LANE), row),
            pl.BlockSpec((META_ROWS, tm), lambda i: (0, i)),
            pl.BlockSpec((1, LANE), fixed),
        ],
        out_shape=[
            jax.ShapeDtypeStruct((N_TOK * TOKEN_ROWS, LANE), F32),
            jax.ShapeDtypeStruct((N_TOK, LANE), F32),
            jax.ShapeDtypeStruct((META_ROWS, N_TOK), jnp.int32),
            jax.ShapeDtypeStruct((1, LANE), F32),
        ],
        scratch_shapes=[pltpu.VMEM((1, LANE), F32)],
        compiler_params=pltpu.CompilerParams(
            dimension_semantics=("arbitrary",), vmem_limit_bytes=VMEM_LIMIT),
        name="mix",
    )(c, g, sga, sgb, x2, wco, bco, wgl, wo, bo, l1g, l1b, wrh, wrl, br, tri)


def _dispatch_kernel(e0_ref, e1_ref, r0_ref, r1_ref, cnt_ref, x1_ref,
                     xs_ref, d0_ref, d1_ref, ord_ref, oe_ref, nums_ref,
                     pstart_ref, zero_ref, sem, zsem):
    i = pl.program_id(0)

    def zero_copy(b):
        rows = BM * TOKEN_ROWS
        return pltpu.make_async_copy(zero_ref, xs_ref.at[pl.ds(b * rows, rows)], zsem)

    @pl.when(i == 0)
    def _():
        zero_ref[...] = jnp.zeros_like(zero_ref)

        def plan(e, carry):
            bstart, n_ord = carry
            nb = (cnt_ref[e] + (BM - 1)) // BM
            pstart_ref[e] = bstart * BM

            def fill(j, c):
                ord_ref[bstart + j] = n_ord
                return c

            lax.fori_loop(0, nb, fill, 0)

            @pl.when(nb > 0)
            def _():
                oe_ref[n_ord] = e
                zero_copy(bstart + nb - 1).start()

            return bstart + nb, n_ord + jnp.where(nb > 0, 1, 0)

        n_used, n_ord = lax.fori_loop(0, N_EXPERTS, plan, (jnp.int32(0), jnp.int32(0)))
        nums_ref[0] = n_used
        nums_ref[1] = n_ord

        def fill_rest(j, c):
            ord_ref[j] = n_ord - 1
            zero_copy(j).start()
            return c

        lax.fori_loop(n_used, N_BLOCKS, fill_rest, 0)

        def fill_oe(j, c):
            oe_ref[j] = 0
            return c

        lax.fori_loop(n_ord, N_EXPERTS, fill_oe, 0)

        def drain(j, c):
            zero_copy(0).wait()
            return c

        lax.fori_loop(0, n_ord + (N_BLOCKS - n_used), drain, 0)

    base = i * TD_DISP

    def tok_group(tg, carry):
        for uu in range(DISP_UNROLL):
            tt = tg * DISP_UNROLL + uu
            t = base + tt
            src = x1_ref.at[pl.ds(pl.multiple_of(tt * TOKEN_ROWS, TOKEN_ROWS), TOKEN_ROWS)]
            for e_ref, r_ref, d_ref in ((e0_ref, r0_ref, d0_ref), (e1_ref, r1_ref, d1_ref)):
                d = pstart_ref[e_ref[t]] + r_ref[t]
                d_ref[t] = d
                pltpu.make_async_copy(
                    src, xs_ref.at[pl.ds(pl.multiple_of(d * TOKEN_ROWS, TOKEN_ROWS), TOKEN_ROWS)],
                    sem).start()
        return carry

    lax.fori_loop(0, TD_DISP // DISP_UNROLL, tok_group, 0)
    for kk in range(TOP_K):
        pltpu.make_async_copy(x1_ref, xs_ref.at[pl.ds(0, TD_DISP * TOKEN_ROWS)], sem).wait()


def _dispatch(e0, e1, r0, r1, cnt, x1):
    smem = pl.BlockSpec(memory_space=pltpu.SMEM)
    grid_spec = pltpu.PrefetchScalarGridSpec(
        num_scalar_prefetch=5,
        grid=(N_TOK // TD_DISP,),
        in_specs=[pl.BlockSpec((TD_DISP * TOKEN_ROWS, LANE), lambda i, *_: (i, 0))],
        out_specs=[pl.BlockSpec(memory_space=pl.ANY), smem, smem, smem, smem, smem],
        scratch_shapes=[
            pltpu.SMEM((N_EXPERTS,), jnp.int32),
            pltpu.VMEM((BM * TOKEN_ROWS, LANE), F32),
            pltpu.SemaphoreType.DMA,
            pltpu.SemaphoreType.DMA,
        ],
    )
    return pl.pallas_call(
        _dispatch_kernel,
        grid_spec=grid_spec,
        out_shape=[
            jax.ShapeDtypeStruct((N_SLOTS * TOKEN_ROWS, LANE), F32),
            jax.ShapeDtypeStruct((N_TOK,), jnp.int32),
            jax.ShapeDtypeStruct((N_TOK,), jnp.int32),
            jax.ShapeDtypeStruct((N_BLOCKS,), jnp.int32),
            jax.ShapeDtypeStruct((N_EXPERTS,), jnp.int32),
            jax.ShapeDtypeStruct((2,), jnp.int32),
        ],
        compiler_params=pltpu.CompilerParams(
            dimension_semantics=("arbitrary",), vmem_limit_bytes=VMEM_LIMIT),
        name="dispatch",
    )(e0, e1, r0, r1, cnt, x1)


def _expert_kernel(ord_ref, oe_ref, nums_ref, xs_ref, w1_hbm, w3_hbm, w2_hbm, ys_ref,
                   w1f_ref, w3f_ref, w2f_ref, w1b_ref, w3b_ref, w2b_ref, sem):
    i = pl.program_id(0)
    n_used = nums_ref[0]
    n_ord = nums_ref[1]
    active = i < n_used
    k = ord_ref[i]
    first = (i == 0) | (k != ord_ref[jnp.maximum(i - 1, 0)])

    def weight_copies(kk):
        e = oe_ref[kk]
        slot = kk % 2
        return [pltpu.make_async_copy(w_hbm.at[e], wf_ref.at[slot], sem.at[slot])
                for w_hbm, wf_ref in ((w1_hbm, w1f_ref), (w3_hbm, w3f_ref), (w2_hbm, w2f_ref))]

    @pl.when(active & first)
    def _():
        @pl.when(i == 0)
        def _():
            for cp in weight_copies(k):
                cp.start()

        @pl.when(k + 1 < n_ord)
        def _():
            for cp in weight_copies(k + 1):
                cp.start()

        for cp in weight_copies(k):
            cp.wait()
        slot = k % 2
        w1b_ref[...] = w1f_ref[slot].astype(BF16)
        w3b_ref[...] = w3f_ref[slot].astype(BF16)
        w2b_ref[...] = w2f_ref[slot].astype(BF16)

    @pl.when(active)
    def _():
        xb = _from_token_tiles(xs_ref, BM).astype(BF16)
        h1 = jnp.dot(xb, w1b_ref[...], preferred_element_type=F32)
        h3 = jnp.dot(xb, w3b_ref[...], preferred_element_type=F32)
        hdn = (h1 * _sigmoid(h1) * h3).astype(BF16)
        _to_token_tiles(ys_ref, jnp.dot(hdn, w2b_ref[...], preferred_element_type=F32))

    @pl.when(jnp.logical_not(active))
    def _():
        ys_ref[...] = jnp.zeros_like(ys_ref)


def _experts(blk_ord, ord_e, nums, xs, w1, w3, w2):
    last = lambda i, nums_ref: jnp.maximum(jnp.minimum(i, nums_ref[0] - 1), 0)
    hbm = pl.BlockSpec(memory_space=pl.ANY)
    grid_spec = pltpu.PrefetchScalarGridSpec(
        num_scalar_prefetch=3,
        grid=(N_BLOCKS,),
        in_specs=[
            pl.BlockSpec((BM * TOKEN_ROWS, LANE), lambda i, o, oe, nu: (last(i, nu), 0)),
            hbm, hbm, hbm,
        ],
        out_specs=pl.BlockSpec((BM * TOKEN_ROWS, LANE), lambda i, o, oe, nu: (i, 0)),
        scratch_shapes=[
            pltpu.VMEM((2, D_MODEL, D_EXPERT), F32),
            pltpu.VMEM((2, D_MODEL, D_EXPERT), F32),
            pltpu.VMEM((2, D_EXPERT, D_MODEL), F32),
            pltpu.VMEM((D_MODEL, D_EXPERT), BF16),
            pltpu.VMEM((D_MODEL, D_EXPERT), BF16),
            pltpu.VMEM((D_EXPERT, D_MODEL), BF16),
            pltpu.SemaphoreType.DMA((2,)),
        ],
    )
    return pl.pallas_call(
        _expert_kernel,
        grid_spec=grid_spec,
        out_shape=jax.ShapeDtypeStruct((N_SLOTS * TOKEN_ROWS, LANE), F32),
        compiler_params=pltpu.CompilerParams(
            dimension_semantics=("arbitrary",), vmem_limit_bytes=VMEM_LIMIT),
        name="experts",
    )(blk_ord, ord_e, nums, xs, w1, w3, w2)


def _combine_kernel(d0_ref, d1_ref, route_ref, x1_ref, ys_ref, g_ref, b_ref, o_ref, y_ref, sem):
    i = pl.program_id(0)
    n_tiles = pl.num_programs(0)
    tile_rows = TC_COMB * TOKEN_ROWS

    def issue(tile, slot):
        base = tile * TC_COMB

        def group(tg, carry):
            for uu in range(COMB_UNROLL):
                t = tg * COMB_UNROLL + uu
                dst = pl.ds(pl.multiple_of(t * TOKEN_ROWS, TOKEN_ROWS), TOKEN_ROWS)
                for kk, d_ref in enumerate((d0_ref, d1_ref)):
                    src = pl.ds(pl.multiple_of(d_ref[base + t] * TOKEN_ROWS, TOKEN_ROWS),
                                TOKEN_ROWS)
                    pltpu.make_async_copy(ys_ref.at[src], y_ref.at[slot, kk, dst],
                                          sem.at[slot]).start()
            return carry

        lax.fori_loop(0, TC_COMB // COMB_UNROLL, group, 0)

    @pl.when(i == 0)
    def _():
        issue(0, 0)

    @pl.when(i + 1 < n_tiles)
    def _():
        issue(i + 1, (i + 1) % 2)

    slot = i % 2
    for kk in range(TOP_K):
        pltpu.make_async_copy(ys_ref.at[pl.ds(0, tile_rows)], y_ref.at[slot, kk],
                              sem.at[slot]).wait()
    w0 = route_ref[:, 4:5]
    w1 = route_ref[:, 5:6]
    z = (DEEPNORM_ALPHA * _from_token_tiles(x1_ref, TC_COMB)
         + w0 * _from_token_tiles(y_ref.at[slot, 0], TC_COMB)
         + w1 * _from_token_tiles(y_ref.at[slot, 1], TC_COMB))
    o_ref[...] = _layer_norm(z, g_ref[...], b_ref[...])


def _combine(d0, d1, route, x1, ys, g, b):
    tc = TC_COMB
    grid_spec = pltpu.PrefetchScalarGridSpec(
        num_scalar_prefetch=2,
        grid=(N_TOK // tc,),
        in_specs=[
            pl.BlockSpec((tc, LANE), lambda i, *_: (i, 0)),
            pl.BlockSpec((tc * TOKEN_ROWS, LANE), lambda i, *_: (i, 0)),
            pl.BlockSpec(memory_space=pl.ANY),
            pl.BlockSpec((1, D_MODEL), lambda i, *_: (0, 0)),
            pl.BlockSpec((1, D_MODEL), lambda i, *_: (0, 0)),
        ],
        out_specs=pl.BlockSpec((tc, D_MODEL), lambda i, *_: (i, 0)),
        scratch_shapes=[
            pltpu.VMEM((2, TOP_K, tc * TOKEN_ROWS, LANE), F32),
            pltpu.SemaphoreType.DMA((2,)),
        ],
    )
    return pl.pallas_call(
        _combine_kernel,
        grid_spec=grid_spec,
        out_shape=jax.ShapeDtypeStruct((N_TOK, D_MODEL), F32),
        compiler_params=pltpu.CompilerParams(
            dimension_semantics=("arbitrary",), vmem_limit_bytes=VMEM_LIMIT),
        name="combine",
    )(d0, d1, route, x1, ys, g, b)


def _split_bf16(w):
    hi = w.astype(BF16)
    lo = (w - hi.astype(F32)).astype(BF16)
    return hi, lo


def kernel(x, w_in, b_in, conv_w, conv_b, conv_ln_g, conv_ln_b, w_conv_out, b_conv_out, w_gate_up, b_gate_up, gla_norm_g, w_gla_out, w_out, b_out, ln1_g, ln1_b, w_router_group, b_router_group, w_router_expert, b_router_expert, w1, w3, w2, ln2_g, ln2_b):
    x2 = x.reshape(N_TOK, D_MODEL)
    row = lambda v: v.reshape(1, -1)
    for l in range(w_in.shape[0]):
        f0 = N_PROJ_A
        f1 = f0 + GATE_RANK
        w_l, b_l = w_in[l], b_in[l]
        w_a = w_l[:, :f0].astype(BF16)
        w_b = w_l[:, f1:].astype(BF16)
        w_f = jnp.pad(w_l[:, f0:f1], ((0, 0), (0, F_PAD - GATE_RANK))).astype(BF16)
        b_p = row(jnp.concatenate(
            [b_l[:f0], b_l[f1:], b_l[f0:f1], jnp.zeros((F_PAD - GATE_RANK,), F32)]))
        wgu_p = jnp.concatenate(
            [w_gate_up[l], jnp.zeros((F_PAD - GATE_RANK, GLA_DK), F32)], axis=0).astype(BF16)
        cw_p = jnp.repeat(conv_w[l], SUBLANES, axis=0)
        sr = lax.broadcasted_iota(jnp.int32, (SUBLANES * CONV_SHROWS, CONV_WIN), 0)
        sc = lax.broadcasted_iota(jnp.int32, (SUBLANES * CONV_SHROWS, CONV_WIN), 1)
        shift = (sc == sr % CONV_SHROWS + sr // CONV_SHROWS).astype(BF16)
        c, q, k, v, rs, sga, sgb, la = _inproj(
            x2, w_a, w_b, w_f, b_p, wgu_p, row(b_gate_up[l]), shift, cw_p,
            row(conv_b[l]), row(conv_ln_g[l]), row(conv_ln_b[l]))
        g = _gla(q, k, v, la, rs, row(gla_norm_g[l]))

        w_r = jnp.concatenate(
            [w_router_expert[l], w_router_group[l],
             jnp.zeros((D_MODEL, LANE - N_EXPERTS - N_GROUPS), F32)], axis=1)
        b_r = row(jnp.concatenate(
            [b_router_expert[l], b_router_group[l],
             jnp.zeros((LANE - N_EXPERTS - N_GROUPS,), F32)]))
        wrh, wrl = _split_bf16(w_r)
        ri = lax.broadcasted_iota(jnp.int32, (TM_MIX, TM_MIX), 0)
        ci = lax.broadcasted_iota(jnp.int32, (TM_MIX, TM_MIX), 1)
        tri = (ri > ci).astype(BF16)
        x1, route, meta, cnt = _mix(
            c, g, sga, sgb, x2, w_conv_out[l].astype(BF16), row(b_conv_out[l]),
            w_gla_out[l].astype(BF16), w_out[l].astype(BF16), row(b_out[l]),
            row(ln1_g[l]), row(ln1_b[l]), wrh, wrl, b_r, tri)

        cnt_i = cnt[0, :N_EXPERTS].astype(jnp.int32)
        xs, d0, d1, blk_ord, ord_e, nums = _dispatch(
            meta[0], meta[1], meta[2], meta[3], cnt_i, x1)
        ys = _experts(blk_ord, ord_e, nums, xs, w1[l], w3[l], w2[l])
        x2 = _combine(d0, d1, route, x1, ys, row(ln2_g[l]), row(ln2_b[l]))
    return x2.reshape(x.shape)
```

```python
import jax
import jax.numpy as jnp
from jax import lax
from jax.experimental import pallas as pl
from jax.experimental.pallas import tpu as pltpu

F32 = jnp.float32
BF16 = jnp.bfloat16

D_MODEL = 1024
BATCH = 8
SEQ = 2048
N_TOK = BATCH * SEQ
CHUNK = 64
CONV_WIDTH = 31
GLA_HEADS = 4
GLA_DK = 512
GLA_DV = 1024
GLA_HK = 128
GLA_HV = 256
GATE_RANK = 16
GATE_TAU = 16.0
N_GROUPS = 8
EXPERTS_PER_GROUP = 8
N_EXPERTS = 64
TOP_K = 2
D_EXPERT = 512
LN_EPS = 1e-5
RMS_EPS = 1e-6
DEEPNORM_ALPHA = 2.0 ** 0.25

LANE = 128
LANE_BITS = LANE.bit_length() - 1
TOKEN_ROWS = D_MODEL // LANE
F_PAD = LANE
N_PROJ_A = 2 * D_MODEL + 2 * GLA_DK + 2 * GLA_DV
N_PROJ = N_PROJ_A + 2 * D_MODEL + F_PAD
TM_PROJ = 512
CONV_HALO = 32
CONV_ROWS = 32
SUBLANES = 8
CONV_SUB = 128
CONV_WIN = 256
CONV_SHROWS = CONV_SUB + CONV_HALO
TS_GLA = 256
NB_GLA = 4
TM_MIX = 512
META_ROWS = 8
BM = 256
N_BLOCKS = N_TOK * TOP_K // BM + N_EXPERTS
N_SLOTS = N_BLOCKS * BM
TC_COMB = 256
COMB_ROWS = 128
TD_DISP = 2048
DISP_UNROLL = LANE
VMEM_LIMIT = 56 * 1024 * 1024


def _sigmoid(x):
    return 1.0 / (1.0 + jnp.exp(-x))


def _to_token_tiles(ref, val, t0=0):
    n = val.shape[0]
    for c in range(TOKEN_ROWS):
        ref[pl.ds(t0 * TOKEN_ROWS + c, n, stride=TOKEN_ROWS), :] = val[:, c * LANE:(c + 1) * LANE]


def _from_token_tiles(ref, n, t0=0):
    return jnp.concatenate(
        [ref[pl.ds(t0 * TOKEN_ROWS + c, n, stride=TOKEN_ROWS), :] for c in range(TOKEN_ROWS)],
        axis=1)


def _layer_norm(z, g, b):
    mu = jnp.mean(z, axis=-1, keepdims=True)
    zc = z - mu
    var = jnp.mean(zc * zc, axis=-1, keepdims=True)
    return zc * lax.rsqrt(var + LN_EPS) * g + b


def _conv_block(ext_ref, shf_ref, shift_ref, cw_ref, cb_ref, g_ref, b_ref, c_ref, base):
    off = CONV_HALO - (CONV_WIDTH - 1)
    shf_ref[...] = jnp.dot(shift_ref[...], ext_ref[base:base + CONV_WIN, :],
                           preferred_element_type=F32)
    n_sub = CONV_ROWS // SUBLANES
    for i in range(CONV_SUB // CONV_ROWS):
        r0 = i * CONV_ROWS
        acc = [jnp.zeros((SUBLANES, D_MODEL), F32) for _ in range(n_sub)]
        for j in range(CONV_WIDTH):
            l0 = r0 + off + j
            b = l0 % SUBLANES
            m0 = b * CONV_SHROWS + (l0 - b)
            wj = cw_ref[j * SUBLANES:(j + 1) * SUBLANES, :]
            for k in range(n_sub):
                rk = m0 + k * SUBLANES
                acc[k] = acc[k] + shf_ref[rk:rk + SUBLANES, :] * wj
        y = _layer_norm(jnp.concatenate(acc, axis=0) + cb_ref[...], g_ref[...], b_ref[...])
        c_ref[base + r0:base + r0 + CONV_ROWS, :] = (y * _sigmoid(y)).astype(BF16)


def _inproj_kernel(x_ref, wa_ref, wb_ref, wf_ref, b_ref, wgu_ref, bgu_ref,
                   shift_ref, cw_ref, cb_ref, cg_ref, cbeta_ref,
                   c_ref, q_ref, k_ref, v_ref, rs_ref, sga_ref, sgb_ref, la_ref,
                   ext_ref, shf_ref):
    i = pl.program_id(0)
    xb = x_ref[...].astype(BF16)
    half = D_MODEL // 2

    def seg(c0, n):
        if c0 < N_PROJ_A:
            w = wa_ref[:, c0:c0 + n]
        elif c0 < N_PROJ_A + 2 * D_MODEL:
            w = wb_ref[:, c0 - N_PROJ_A:c0 - N_PROJ_A + n]
        else:
            w = wf_ref[...]
        return jnp.dot(xb, w, preferred_element_type=F32) + b_ref[:, c0:c0 + n]

    @pl.when(i == 0)
    def _():
        ext_ref[...] = jnp.zeros_like(ext_ref)

    prev_tail = ext_ref[TM_PROJ:TM_PROJ + CONV_HALO, :]
    ext_ref[0:CONV_HALO, :] = jnp.where(
        i % (SEQ // TM_PROJ) == 0, jnp.zeros_like(prev_tail), prev_tail)
    for j in range(2):
        a = seg(j * half, half)
        g = seg(D_MODEL + j * half, half)
        ext_ref[CONV_HALO:CONV_HALO + TM_PROJ, j * half:(j + 1) * half] = (
            a * _sigmoid(g)).astype(BF16)

    def conv_block(sb):
        _conv_block(ext_ref, shf_ref.at[sb % 2], shift_ref, cw_ref, cb_ref, cg_ref, cbeta_ref,
                    c_ref, sb * CONV_SUB)

    conv_block(0)
    q_ref[...] = (seg(2 * D_MODEL, GLA_DK) * (GLA_HK ** -0.5)).astype(BF16)
    k_ref[...] = seg(2 * D_MODEL + GLA_DK, GLA_DK).astype(BF16)
    for j in range(2):
        v_ref[:, j * half:(j + 1) * half] = seg(3 * D_MODEL + j * half, half).astype(BF16)
    conv_block(1)
    for j in range(2):
        r = seg(4 * D_MODEL + j * half, half)
        rs_ref[:, j * half:(j + 1) * half] = (r * _sigmoid(r)).astype(BF16)
    for j in range(2):
        sga_ref[:, j * half:(j + 1) * half] = _sigmoid(
            seg(5 * D_MODEL + j * half, half)).astype(BF16)
    conv_block(2)
    for j in range(2):
        sgb_ref[:, j * half:(j + 1) * half] = _sigmoid(
            seg(6 * D_MODEL + j * half, half)).astype(BF16)
    f = seg(7 * D_MODEL, F_PAD)
    z = jnp.dot(f.astype(BF16), wgu_ref[...], preferred_element_type=F32) + bgu_ref[...]
    la_ref[...] = (jnp.minimum(z, 0.0) - jnp.log(1.0 + jnp.exp(-jnp.abs(z)))) * (1.0 / GATE_TAU)
    conv_block(3)


def _inproj(x2, w_a, w_b, w_f, b_p, wgu_p, bgu, shift, cw_p, cb, cg, cbeta):
    tm = TM_PROJ
    row = lambda i: (i, 0)
    fixed = lambda i: (0, 0)
    tok = lambda n, dt: jax.ShapeDtypeStruct((N_TOK, n), dt)
    vec = pl.BlockSpec((1, D_MODEL), fixed)
    return pl.pallas_call(
        _inproj_kernel,
        grid=(N_TOK // tm,),
        in_specs=[
            pl.BlockSpec((tm, D_MODEL), row),
            pl.BlockSpec((D_MODEL, N_PROJ_A), fixed, pipeline_mode=pl.Buffered(1)),
            pl.BlockSpec((D_MODEL, 2 * D_MODEL), fixed, pipeline_mode=pl.Buffered(1)),
            pl.BlockSpec((D_MODEL, F_PAD), fixed),
            pl.BlockSpec((1, N_PROJ), fixed),
            pl.BlockSpec((F_PAD, GLA_DK), fixed),
            pl.BlockSpec((1, GLA_DK), fixed),
            pl.BlockSpec((SUBLANES * CONV_SHROWS, CONV_WIN), fixed),
            pl.BlockSpec((CONV_WIDTH * SUBLANES, D_MODEL), fixed),
            vec, vec, vec,
        ],
        out_specs=[
            pl.BlockSpec((tm, D_MODEL), row),
            pl.BlockSpec((tm, GLA_DK), row),
            pl.BlockSpec((tm, GLA_DK), row),
            pl.BlockSpec((tm, GLA_DV), row),
            pl.BlockSpec((tm, GLA_DV), row),
            pl.BlockSpec((tm, D_MODEL), row),
            pl.BlockSpec((tm, D_MODEL), row),
            pl.BlockSpec((tm, GLA_DK), row),
        ],
        out_shape=[
            tok(D_MODEL, BF16), tok(GLA_DK, BF16), tok(GLA_DK, BF16), tok(GLA_DV, BF16),
            tok(GLA_DV, BF16), tok(D_MODEL, BF16), tok(D_MODEL, BF16), tok(GLA_DK, F32),
        ],
        scratch_shapes=[
            pltpu.VMEM((tm + CONV_WIN - CONV_SUB, D_MODEL), BF16),
            pltpu.VMEM((2, SUBLANES * CONV_SHROWS, D_MODEL), F32),
        ],
        compiler_params=pltpu.CompilerParams(
            dimension_semantics=("arbitrary",), vmem_limit_bytes=VMEM_LIMIT),
        name="inproj",
    )(x2, w_a, w_b, w_f, b_p, wgu_p, bgu, shift, cw_p, cb, cg, cbeta)


def _gla_kernel(q_ref, k_ref, v_ref, la_ref, rs_ref, ng_ref, o_ref,
                st_ref, kv_ref, sb_ref, dec_ref):
    s = pl.program_id(1)

    @pl.when(s == 0)
    def _():
        st_ref[...] = jnp.zeros_like(st_ref)

    rr = lax.broadcasted_iota(jnp.int32, (CHUNK, CHUNK), 0)
    cc = lax.broadcasted_iota(jnp.int32, (CHUNK, CHUNK), 1)
    tri = jnp.where(rr >= cc, 1.0, 0.0).astype(BF16)
    n_chunks = TS_GLA // CHUNK
    heads = [(slice(h * GLA_HK, (h + 1) * GLA_HK), slice(h * GLA_HV, (h + 1) * GLA_HV))
             for h in range(GLA_HEADS)]

    for nb in range(NB_GLA):
        for c in range(n_chunks):
            rows = slice(c * CHUNK, (c + 1) * CHUNK)
            la = la_ref[nb, rows, :]
            hi = la.astype(BF16)
            lo = (la - hi.astype(F32)).astype(BF16)
            cum = (jnp.dot(tri, hi, preferred_element_type=F32)
                   + jnp.dot(tri, lo, preferred_element_type=F32))
            cend = cum[CHUNK - 1:CHUNK, :]
            kd = (k_ref[nb, rows, :].astype(F32) * jnp.exp(cend - cum)).astype(BF16)
            idx = nb * n_chunks + c
            dec_ref[idx * SUBLANES:(idx + 1) * SUBLANES, :] = jnp.broadcast_to(
                jnp.exp(cend), (SUBLANES, GLA_DK))
            vc = v_ref[nb, rows, :]
            for h, (ks, vs) in enumerate(heads):
                kv_ref[idx * GLA_HEADS + h] = lax.dot_general(
                    vc[:, vs], kd[:, ks], (((0,), (0,)), ((), ())), preferred_element_type=F32)

    for nb in range(NB_GLA):
        for h, (ks, vs) in enumerate(heads):
            st = st_ref[nb * GLA_HEADS + h]
            for c in range(n_chunks):
                idx = nb * n_chunks + c
                st = st * dec_ref[idx * SUBLANES:idx * SUBLANES + 1, ks] + kv_ref[
                    idx * GLA_HEADS + h]
                sb_ref[idx * GLA_HEADS + h] = st.astype(BF16)
            st_ref[nb * GLA_HEADS + h] = st

    for nb in range(NB_GLA):
        for c in range(n_chunks):
            rows = slice(c * CHUNK, (c + 1) * CHUNK)
            idx = nb * n_chunks + c
            qc = q_ref[nb, rows, :]
            for h, (ks, vs) in enumerate(heads):
                o = lax.dot_general(qc[:, ks], sb_ref[idx * GLA_HEADS + h],
                                    (((1,), (1,)), ((), ())), preferred_element_type=F32)
                ms = jnp.mean(o * o, axis=-1, keepdims=True)
                on = o * lax.rsqrt(ms + RMS_EPS) * ng_ref[:, vs]
                o_ref[nb, rows, vs] = (on * rs_ref[nb, rows, vs].astype(F32)).astype(BF16)


def _gla(q, k, v, la, rs, ng):
    ts = TS_GLA
    blk = lambda n: pl.BlockSpec((NB_GLA, ts, n), lambda bi, si: (bi, si, 0))
    seq = lambda a: a.reshape(BATCH, SEQ, a.shape[-1])
    out = pl.pallas_call(
        _gla_kernel,
        grid=(BATCH // NB_GLA, SEQ // ts),
        in_specs=[blk(GLA_DK), blk(GLA_DK), blk(GLA_DV), blk(GLA_DK), blk(GLA_DV),
                  pl.BlockSpec((1, GLA_DV), lambda bi, si: (0, 0))],
        out_specs=blk(GLA_DV),
        out_shape=jax.ShapeDtypeStruct((BATCH, SEQ, GLA_DV), BF16),
        scratch_shapes=[
            pltpu.VMEM((NB_GLA * GLA_HEADS, GLA_HV, GLA_HK), F32),
            pltpu.VMEM((NB_GLA * (ts // CHUNK) * GLA_HEADS, GLA_HV, GLA_HK), F32),
            pltpu.VMEM((NB_GLA * (ts // CHUNK) * GLA_HEADS, GLA_HV, GLA_HK), BF16),
            pltpu.VMEM((NB_GLA * (ts // CHUNK) * SUBLANES, GLA_DK), F32),
        ],
        compiler_params=pltpu.CompilerParams(
            dimension_semantics=("arbitrary", "arbitrary"), vmem_limit_bytes=VMEM_LIMIT),
        name="gla",
    )(seq(q), seq(k), seq(v), seq(la), seq(rs), ng)
    return out.reshape(N_TOK, GLA_DV)


def _mix_kernel(c_ref, g_ref, sga_ref, sgb_ref, x_ref, wco_ref, bco_ref, wgl_ref, wo_ref, bo_ref,
                l1g_ref, l1b_ref, wrh_ref, wrl_ref, br_ref, tri_ref,
                x1_ref, route_ref, meta_ref, cnt_ref, carry_ref):
    i = pl.program_id(0)

    @pl.when(i == 0)
    def _():
        carry_ref[...] = jnp.zeros_like(carry_ref)

    yc = jnp.dot(c_ref[...], wco_ref[...], preferred_element_type=F32) + bco_ref[...]
    yg = jnp.dot(g_ref[...], wgl_ref[...], preferred_element_type=F32)
    merged = sga_ref[...].astype(F32) * yc + sgb_ref[...].astype(F32) * yg
    mix = jnp.dot(merged.astype(BF16), wo_ref[...], preferred_element_type=F32) + bo_ref[...]
    x1 = _layer_norm(DEEPNORM_ALPHA * x_ref[...] + mix, l1g_ref[...], l1b_ref[...])
    _to_token_tiles(x1_ref, x1)

    xh = x1.astype(BF16)
    xl = (x1 - xh.astype(F32)).astype(BF16)
    lg = (jnp.dot(xh, wrh_ref[...], preferred_element_type=F32)
          + jnp.dot(xl, wrh_ref[...], preferred_element_type=F32)
          + jnp.dot(xh, wrl_ref[...], preferred_element_type=F32)) + br_ref[...]

    tm = lg.shape[0]
    lane = lax.broadcasted_iota(jnp.int32, (tm, LANE), 1)
    neg = -jnp.inf
    big = jnp.int32(1 << 20)
    is_g = (lane >= N_EXPERTS) & (lane < N_EXPERTS + N_GROUPS)
    gl = jnp.where(is_g, lg, neg)
    gmax = jnp.max(gl, axis=-1, keepdims=True)
    gsel = jnp.min(jnp.where(gl == gmax, lane, big), axis=-1, keepdims=True) - N_EXPERTS
    gw = 1.0 / jnp.sum(jnp.exp(gl - gmax), axis=-1, keepdims=True)
    in_grp = (lane < N_EXPERTS) & ((lane >> 3) == gsel)
    el = jnp.where(in_grp, lg, neg)
    m1 = jnp.max(el, axis=-1, keepdims=True)
    i1 = jnp.min(jnp.where(el == m1, lane, big), axis=-1, keepdims=True)
    el2 = jnp.where(lane == i1, neg, el)
    m2 = jnp.max(el2, axis=-1, keepdims=True)
    i2 = jnp.min(jnp.where(el2 == m2, lane, big), axis=-1, keepdims=True)
    t = jnp.exp(m2 - m1)
    w1 = gw / (1.0 + t)
    w2 = gw * t / (1.0 + t)

    oh1 = lane == i1
    oh2 = lane == i2
    osum = jnp.where(oh1 | oh2, 1.0, 0.0)
    excl = jnp.dot(tri_ref[...], osum.astype(BF16), preferred_element_type=F32) + carry_ref[...]
    rank1 = jnp.sum(jnp.where(oh1, excl, 0.0), axis=-1, keepdims=True)
    rank2 = jnp.sum(jnp.where(oh2, excl, 0.0), axis=-1, keepdims=True)
    new_cnt = carry_ref[...] + jnp.sum(osum, axis=0, keepdims=True)
    carry_ref[...] = new_cnt
    cnt_ref[...] = new_cnt

    route = jnp.where(lane == 0, i1.astype(F32), 0.0)
    route = jnp.where(lane == 1, i2.astype(F32), route)
    route = jnp.where(lane == 2, rank1, route)
    route = jnp.where(lane == 3, rank2, route)
    route = jnp.where(lane == 4, w1, route)
    route = jnp.where(lane == 5, w2, route)
    route_ref[...] = route
    meta_ref[...] = route.T[0:META_ROWS, :].astype(jnp.int32)


def _mix(c, g, sga, sgb, x2, wco, bco, wgl, wo, bo, l1g, l1b, wrh, wrl, br, tri):
    tm = TM_MIX
    row = lambda i: (i, 0)
    fixed = lambda i: (0, 0)
    mat = pl.BlockSpec((D_MODEL, D_MODEL), fixed, pipeline_mode=pl.Buffered(1))
    vec = pl.BlockSpec((1, D_MODEL), fixed)
    return pl.pallas_call(
        _mix_kernel,
        grid=(N_TOK // tm,),
        in_specs=[
            pl.BlockSpec((tm, D_MODEL), row), pl.BlockSpec((tm, D_MODEL), row),
            pl.BlockSpec((tm, D_MODEL), row), pl.BlockSpec((tm, D_MODEL), row),
            pl.BlockSpec((tm, D_MODEL), row),
            mat, vec, mat, mat, vec, vec, vec,
            pl.BlockSpec((D_MODEL, LANE), fixed), pl.BlockSpec((D_MODEL, LANE), fixed),
            pl.BlockSpec((1, LANE), fixed),
            pl.BlockSpec((tm, tm), fixed),
        ],
        out_specs=[
            pl.BlockSpec((tm * TOKEN_ROWS, LANE), row),
            pl.BlockSpec((tm, LANE), row),
            pl.BlockSpec((META_ROWS, tm), lambda i: (0, i)),
            pl.BlockSpec((1, LANE), fixed),
        ],
        out_shape=[
            jax.ShapeDtypeStruct((N_TOK * TOKEN_ROWS, LANE), F32),
            jax.ShapeDtypeStruct((N_TOK, LANE), F32),
            jax.ShapeDtypeStruct((META_ROWS, N_TOK), jnp.int32),
            jax.ShapeDtypeStruct((1, LANE), F32),
        ],
        scratch_shapes=[pltpu.VMEM((1, LANE), F32)],
        compiler_params=pltpu.CompilerParams(
            dimension_semantics=("arbitrary",), vmem_limit_bytes=VMEM_LIMIT),
        name="mix",
    )(c, g, sga, sgb, x2, wco, bco, wgl, wo, bo, l1g, l1b, wrh, wrl, br, tri)


def _dispatch_kernel(cnt_ref, e0_ref, e1_ref, r0_ref, r1_ref, x1_ref,
                     xs_ref, d0_ref, d1_ref, ord_ref, oe_ref, nums_ref,
                     pstart_ref, zero_ref, dvm_ref, sem, zsem, dsem):
    i = pl.program_id(0)

    def zero_copy(b):
        rows = BM * TOKEN_ROWS
        return pltpu.make_async_copy(zero_ref, xs_ref.at[pl.ds(b * rows, rows)], zsem)

    @pl.when(i == 0)
    def _():
        zero_ref[...] = jnp.zeros_like(zero_ref)

        def plan(e, carry):
            bstart, n_ord = carry
            nb = (cnt_ref[e] + (BM - 1)) // BM
            pstart_ref[e] = bstart * BM

            def fill(j, c):
                ord_ref[bstart + j] = n_ord
                return c

            lax.fori_loop(0, nb, fill, 0)

            @pl.when(nb > 0)
            def _():
                oe_ref[n_ord] = e
                zero_copy(bstart + nb - 1).start()

            return bstart + nb, n_ord + jnp.where(nb > 0, 1, 0)

        n_used, n_ord = lax.fori_loop(0, N_EXPERTS, plan, (jnp.int32(0), jnp.int32(0)))
        nums_ref[0] = n_used
        nums_ref[1] = n_ord

        def fill_rest(j, c):
            ord_ref[j] = n_ord - 1
            zero_copy(j).start()
            return c

        lax.fori_loop(n_used, N_BLOCKS, fill_rest, 0)

        def fill_oe(j, c):
            oe_ref[j] = 0
            return c

        lax.fori_loop(n_ord, N_EXPERTS, fill_oe, 0)

        for e_ref, r_ref, d_ref, k in ((e0_ref, r0_ref, d0_ref, 0), (e1_ref, r1_ref, d1_ref, 1)):
            ev = e_ref[...]
            dv = r_ref[...]
            for e in range(N_EXPERTS):
                dv = dv + jnp.where(ev == e, pstart_ref[e], 0)
            dvm_ref[k] = dv
            pltpu.make_async_copy(dvm_ref.at[k], d_ref, dsem).start()
        for d_ref, k in ((d0_ref, 0), (d1_ref, 1)):
            pltpu.make_async_copy(dvm_ref.at[k], d_ref, dsem).wait()

        def drain(j, c):
            zero_copy(0).wait()
            return c

        lax.fori_loop(0, n_ord + (N_BLOCKS - n_used), drain, 0)

    base = i * TD_DISP

    def tok_group(tg, carry):
        drow = (base >> LANE_BITS) + tg
        for uu in range(DISP_UNROLL):
            tt = tg * DISP_UNROLL + uu
            src = x1_ref.at[pl.ds(pl.multiple_of(tt * TOKEN_ROWS, TOKEN_ROWS), TOKEN_ROWS)]
            for d_ref in (d0_ref, d1_ref):
                d = d_ref[drow, uu]
                pltpu.make_async_copy(
                    src, xs_ref.at[pl.ds(pl.multiple_of(d * TOKEN_ROWS, TOKEN_ROWS), TOKEN_ROWS)],
                    sem).start()
        return carry

    lax.fori_loop(0, TD_DISP // DISP_UNROLL, tok_group, 0)
    for kk in range(TOP_K):
        pltpu.make_async_copy(x1_ref, xs_ref.at[pl.ds(0, TD_DISP * TOKEN_ROWS)], sem).wait()


def _dispatch(e0, e1, r0, r1, cnt, x1):
    smem = pl.BlockSpec(memory_space=pltpu.SMEM)
    idx = pl.BlockSpec((N_TOK // LANE, LANE), lambda i, c: (0, 0))
    grid_spec = pltpu.PrefetchScalarGridSpec(
        num_scalar_prefetch=1,
        grid=(N_TOK // TD_DISP,),
        in_specs=[idx, idx, idx, idx,
                  pl.BlockSpec((TD_DISP * TOKEN_ROWS, LANE), lambda i, c: (i, 0))],
        out_specs=[pl.BlockSpec(memory_space=pl.ANY), smem, smem, smem, smem, smem],
        scratch_shapes=[
            pltpu.SMEM((N_EXPERTS,), jnp.int32),
            pltpu.VMEM((BM * TOKEN_ROWS, LANE), F32),
            pltpu.VMEM((TOP_K, N_TOK // LANE, LANE), jnp.int32),
            pltpu.SemaphoreType.DMA,
            pltpu.SemaphoreType.DMA,
            pltpu.SemaphoreType.DMA,
        ],
    )
    return pl.pallas_call(
        _dispatch_kernel,
        grid_spec=grid_spec,
        out_shape=[
            jax.ShapeDtypeStruct((N_SLOTS * TOKEN_ROWS, LANE), F32),
            jax.ShapeDtypeStruct((N_TOK // LANE, LANE), jnp.int32),
            jax.ShapeDtypeStruct((N_TOK // LANE, LANE), jnp.int32),
            jax.ShapeDtypeStruct((N_BLOCKS,), jnp.int32),
            jax.ShapeDtypeStruct((N_EXPERTS,), jnp.int32),
            jax.ShapeDtypeStruct((2,), jnp.int32),
        ],
        compiler_params=pltpu.CompilerParams(
            dimension_semantics=("arbitrary",), vmem_limit_bytes=VMEM_LIMIT),
        name="dispatch",
    )(cnt, e0, e1, r0, r1, x1)


def _expert_kernel(ord_ref, oe_ref, nums_ref, xs_ref, w1_hbm, w3_hbm, w2_hbm, ys_ref,
                   w1f_ref, w3f_ref, w2f_ref, w1b_ref, w3b_ref, w2b_ref, sem):
    i = pl.program_id(0)
    n_used = nums_ref[0]
    n_ord = nums_ref[1]
    active = i < n_used
    k = ord_ref[i]
    first = (i == 0) | (k != ord_ref[jnp.maximum(i - 1, 0)])

    def weight_copies(kk):
        e = oe_ref[kk]
        slot = kk % 2
        return [pltpu.make_async_copy(w_hbm.at[e], wf_ref.at[slot], sem.at[slot])
                for w_hbm, wf_ref in ((w1_hbm, w1f_ref), (w3_hbm, w3f_ref), (w2_hbm, w2f_ref))]

    @pl.when(active & first)
    def _():
        @pl.when(i == 0)
        def _():
            for cp in weight_copies(k):
                cp.start()

        @pl.when(k + 1 < n_ord)
        def _():
            for cp in weight_copies(k + 1):
                cp.start()

        for cp in weight_copies(k):
            cp.wait()
        slot = k % 2
        w1b_ref[...] = w1f_ref[slot].astype(BF16)
        w3b_ref[...] = w3f_ref[slot].astype(BF16)
        w2b_ref[...] = w2f_ref[slot].astype(BF16)

    @pl.when(active)
    def _():
        xb = _from_token_tiles(xs_ref, BM).astype(BF16)
        h1 = jnp.dot(xb, w1b_ref[...], preferred_element_type=F32)
        h3 = jnp.dot(xb, w3b_ref[...], preferred_element_type=F32)
        hdn = (h1 * _sigmoid(h1) * h3).astype(BF16)
        _to_token_tiles(ys_ref, jnp.dot(hdn, w2b_ref[...], preferred_element_type=F32))

    @pl.when(jnp.logical_not(active))
    def _():
        ys_ref[...] = jnp.zeros_like(ys_ref)


def _experts(blk_ord, ord_e, nums, xs, w1, w3, w2):
    last = lambda i, nums_ref: jnp.maximum(jnp.minimum(i, nums_ref[0] - 1), 0)
    hbm = pl.BlockSpec(memory_space=pl.ANY)
    grid_spec = pltpu.PrefetchScalarGridSpec(
        num_scalar_prefetch=3,
        grid=(N_BLOCKS,),
        in_specs=[
            pl.BlockSpec((BM * TOKEN_ROWS, LANE), lambda i, o, oe, nu: (last(i, nu), 0)),
            hbm, hbm, hbm,
        ],
        out_specs=pl.BlockSpec((BM * TOKEN_ROWS, LANE), lambda i, o, oe, nu: (i, 0)),
        scratch_shapes=[
            pltpu.VMEM((2, D_MODEL, D_EXPERT), F32),
            pltpu.VMEM((2, D_MODEL, D_EXPERT), F32),
            pltpu.VMEM((2, D_EXPERT, D_MODEL), F32),
            pltpu.VMEM((D_MODEL, D_EXPERT), BF16),
            pltpu.VMEM((D_MODEL, D_EXPERT), BF16),
            pltpu.VMEM((D_EXPERT, D_MODEL), BF16),
            pltpu.SemaphoreType.DMA((2,)),
        ],
    )
    return pl.pallas_call(
        _expert_kernel,
        grid_spec=grid_spec,
        out_shape=jax.ShapeDtypeStruct((N_SLOTS * TOKEN_ROWS, LANE), F32),
        compiler_params=pltpu.CompilerParams(
            dimension_semantics=("arbitrary",), vmem_limit_bytes=VMEM_LIMIT),
        name="experts",
    )(blk_ord, ord_e, nums, xs, w1, w3, w2)


def _combine_kernel(d0_ref, d1_ref, route_ref, x1_ref, ys_ref, g_ref, b_ref, o_ref, y_ref, sem):
    i = pl.program_id(0)
    n_tiles = pl.num_programs(0)
    tile_rows = TC_COMB * TOKEN_ROWS

    def issue(tile, slot, t0, n):
        drow = (tile * TC_COMB + t0) >> LANE_BITS
        for uu in range(n):
            t = t0 + uu
            dst = pl.ds(pl.multiple_of(t * TOKEN_ROWS, TOKEN_ROWS), TOKEN_ROWS)
            for kk, d_ref in enumerate((d0_ref, d1_ref)):
                d = d_ref[drow + uu // LANE, uu % LANE]
                src = pl.ds(pl.multiple_of(d * TOKEN_ROWS, TOKEN_ROWS), TOKEN_ROWS)
                pltpu.make_async_copy(ys_ref.at[src], y_ref.at[slot, kk, dst],
                                      sem.at[slot]).start()

    def finish(slot, t0):
        t0 = pl.multiple_of(t0, COMB_ROWS)
        w0 = route_ref[pl.ds(t0, COMB_ROWS), 4:5]
        w1 = route_ref[pl.ds(t0, COMB_ROWS), 5:6]
        z = (DEEPNORM_ALPHA * _from_token_tiles(x1_ref, COMB_ROWS, t0)
             + w0 * _from_token_tiles(y_ref.at[slot, 0], COMB_ROWS, t0)
             + w1 * _from_token_tiles(y_ref.at[slot, 1], COMB_ROWS, t0))
        o_ref[pl.ds(t0, COMB_ROWS), :] = _layer_norm(z, g_ref[...], b_ref[...])

    @pl.when(i == 0)
    def _():
        def first(gi, carry):
            issue(0, 0, gi * COMB_ROWS, COMB_ROWS)
            return carry

        lax.fori_loop(0, TC_COMB // COMB_ROWS, first, 0)

    slot = i % 2
    for kk in range(TOP_K):
        pltpu.make_async_copy(ys_ref.at[pl.ds(0, tile_rows)], y_ref.at[slot, kk],
                              sem.at[slot]).wait()

    @pl.when(i + 1 < n_tiles)
    def _():
        def both(gi, carry):
            issue(i + 1, 1 - slot, gi * COMB_ROWS, COMB_ROWS)
            finish(slot, gi * COMB_ROWS)
            return carry

        lax.fori_loop(0, TC_COMB // COMB_ROWS, both, 0)

    @pl.when(i + 1 == n_tiles)
    def _():
        def last(gi, carry):
            finish(slot, gi * COMB_ROWS)
            return carry

        lax.fori_loop(0, TC_COMB // COMB_ROWS, last, 0)


def _combine(d0, d1, route, x1, ys, g, b):
    tc = TC_COMB
    grid_spec = pltpu.PrefetchScalarGridSpec(
        num_scalar_prefetch=2,
        grid=(N_TOK // tc,),
        in_specs=[
            pl.BlockSpec((tc, LANE), lambda i, *_: (i, 0)),
            pl.BlockSpec((tc * TOKEN_ROWS, LANE), lambda i, *_: (i, 0)),
            pl.BlockSpec(memory_space=pl.ANY),
            pl.BlockSpec((1, D_MODEL), lambda i, *_: (0, 0)),
            pl.BlockSpec((1, D_MODEL), lambda i, *_: (0, 0)),
        ],
        out_specs=pl.BlockSpec((tc, D_MODEL), lambda i, *_: (i, 0)),
        scratch_shapes=[
            pltpu.VMEM((2, TOP_K, tc * TOKEN_ROWS, LANE), F32),
            pltpu.SemaphoreType.DMA((2,)),
        ],
    )
    return pl.pallas_call(
        _combine_kernel,
        grid_spec=grid_spec,
        out_shape=jax.ShapeDtypeStruct((N_TOK, D_MODEL), F32),
        compiler_params=pltpu.CompilerParams(
            dimension_semantics=("arbitrary",), vmem_limit_bytes=VMEM_LIMIT),
        name="combine",
    )(d0, d1, route, x1, ys, g, b)


def _split_bf16(w):
    hi = w.astype(BF16)
    lo = (w - hi.astype(F32)).astype(BF16)
    return hi, lo


def kernel(x, w_in, b_in, conv_w, conv_b, conv_ln_g, conv_ln_b, w_conv_out, b_conv_out, w_gate_up, b_gate_up, gla_norm_g, w_gla_out, w_out, b_out, ln1_g, ln1_b, w_router_group, b_router_group, w_router_expert, b_router_expert, w1, w3, w2, ln2_g, ln2_b):
    x2 = x.reshape(N_TOK, D_MODEL)
    row = lambda v: v.reshape(1, -1)
    for l in range(w_in.shape[0]):
        f0 = N_PROJ_A
        f1 = f0 + GATE_RANK
        w_l, b_l = w_in[l], b_in[l]
        w_a = w_l[:, :f0].astype(BF16)
        w_b = w_l[:, f1:].astype(BF16)
        w_f = jnp.pad(w_l[:, f0:f1], ((0, 0), (0, F_PAD - GATE_RANK))).astype(BF16)
        b_p = row(jnp.concatenate(
            [b_l[:f0], b_l[f1:], b_l[f0:f1], jnp.zeros((F_PAD - GATE_RANK,), F32)]))
        wgu_p = jnp.concatenate(
            [w_gate_up[l], jnp.zeros((F_PAD - GATE_RANK, GLA_DK), F32)], axis=0).astype(BF16)
        cw_p = jnp.repeat(conv_w[l], SUBLANES, axis=0)
        sr = lax.broadcasted_iota(jnp.int32, (SUBLANES * CONV_SHROWS, CONV_WIN), 0)
        sc = lax.broadcasted_iota(jnp.int32, (SUBLANES * CONV_SHROWS, CONV_WIN), 1)
        shift = (sc == sr % CONV_SHROWS + sr // CONV_SHROWS).astype(BF16)
        c, q, k, v, rs, sga, sgb, la = _inproj(
            x2, w_a, w_b, w_f, b_p, wgu_p, row(b_gate_up[l]), shift, cw_p,
            row(conv_b[l]), row(conv_ln_g[l]), row(conv_ln_b[l]))
        g = _gla(q, k, v, la, rs, row(gla_norm_g[l]))

        w_r = jnp.concatenate(
            [w_router_expert[l], w_router_group[l],
             jnp.zeros((D_MODEL, LANE - N_EXPERTS - N_GROUPS), F32)], axis=1)
        b_r = row(jnp.concatenate(
            [b_router_expert[l], b_router_group[l],
             jnp.zeros((LANE - N_EXPERTS - N_GROUPS,), F32)]))
        wrh, wrl = _split_bf16(w_r)
        ri = lax.broadcasted_iota(jnp.int32, (TM_MIX, TM_MIX), 0)
        ci = lax.broadcasted_iota(jnp.int32, (TM_MIX, TM_MIX), 1)
        tri = (ri > ci).astype(BF16)
        x1, route, meta, cnt = _mix(
            c, g, sga, sgb, x2, w_conv_out[l].astype(BF16), row(b_conv_out[l]),
            w_gla_out[l].astype(BF16), w_out[l].astype(BF16), row(b_out[l]),
            row(ln1_g[l]), row(ln1_b[l]), wrh, wrl, b_r, tri)

        cnt_i = cnt[0, :N_EXPERTS].astype(jnp.int32)
        tok_grid = lambda m: m.reshape(N_TOK // LANE, LANE)
        xs, d0, d1, blk_ord, ord_e, nums = _dispatch(
            tok_grid(meta[0]), tok_grid(meta[1]), tok_grid(meta[2]), tok_grid(meta[3]), cnt_i, x1)
        ys = _experts(blk_ord, ord_e, nums, xs, w1[l], w3[l], w2[l])
        x2 = _combine(d0, d1, route, x1, ys, row(ln2_g[l]), row(ln2_b[l]))
    return x2.reshape(x.shape)
```

```python
import jax
import jax.numpy as jnp
from jax import lax
from jax.experimental import pallas as pl
from jax.experimental.pallas import tpu as pltpu

F32 = jnp.float32
BF16 = jnp.bfloat16

D_MODEL = 1024
BATCH = 8
SEQ = 2048
N_TOK = BATCH * SEQ
CHUNK = 64
CONV_WIDTH = 31
GLA_HEADS = 4
GLA_DK = 512
GLA_DV = 1024
GLA_HK = 128
GLA_HV = 256
GATE_RANK = 16
GATE_TAU = 16.0
N_GROUPS = 8
EXPERTS_PER_GROUP = 8
N_EXPERTS = 64
TOP_K = 2
D_EXPERT = 512
LN_EPS = 1e-5
RMS_EPS = 1e-6
DEEPNORM_ALPHA = 2.0 ** 0.25

LANE = 128
LANE_BITS = LANE.bit_length() - 1
TOKEN_ROWS = D_MODEL // LANE
F_PAD = LANE
N_PROJ_A = 2 * D_MODEL + 2 * GLA_DK + 2 * GLA_DV
N_PROJ = N_PROJ_A + 2 * D_MODEL + F_PAD
TM_PROJ = 512
PROJ_PIECE = 256
CONV_HALO = 32
CONV_ROWS = 32
SUBLANES = 8
CONV_SUB = 128
CONV_SHROWS = CONV_SUB + CONV_HALO
CONV_WIN = CONV_SHROWS + 16
TS_GLA = 256
NB_GLA = 4
TM_MIX = 512
META_ROWS = 8
BM = 256
N_BLOCKS = N_TOK * TOP_K // BM + N_EXPERTS
N_SLOTS = N_BLOCKS * BM
TC_COMB = 256
COMB_ROWS = 128
TD_DISP = 2048
DISP_UNROLL = LANE
VMEM_LIMIT = 56 * 1024 * 1024


def _sigmoid(x):
    return 1.0 / (1.0 + jnp.exp(-x))


def _to_token_tiles(ref, val, t0=0):
    n = val.shape[0]
    for c in range(TOKEN_ROWS):
        ref[pl.ds(t0 * TOKEN_ROWS + c, n, stride=TOKEN_ROWS), :] = val[:, c * LANE:(c + 1) * LANE]


def _from_token_tiles(ref, n, t0=0):
    return jnp.concatenate(
        [ref[pl.ds(t0 * TOKEN_ROWS + c, n, stride=TOKEN_ROWS), :] for c in range(TOKEN_ROWS)],
        axis=1)


def _layer_norm(z, g, b):
    mu = jnp.mean(z, axis=-1, keepdims=True)
    zc = z - mu
    var = jnp.mean(zc * zc, axis=-1, keepdims=True)
    return zc * lax.rsqrt(var + LN_EPS) * g + b


def _conv_shift(ext_ref, shf_ref, base):
    rows = CONV_WIN
    win = ext_ref[base:base + rows, :].astype(F32)
    for b in range(SUBLANES):
        sh = win if b == 0 else pltpu.roll(win, rows - b, axis=0)
        shf_ref[b * CONV_SHROWS:(b + 1) * CONV_SHROWS, :] = sh[0:CONV_SHROWS, :]


def _conv_rows(shf_ref, cw_ref, cb_ref, g_ref, b_ref, c_ref, base, r0):
    off = CONV_HALO - (CONV_WIDTH - 1)
    n_sub = CONV_ROWS // SUBLANES
    acc = [jnp.zeros((SUBLANES, D_MODEL), F32) for _ in range(n_sub)]
    for j in range(CONV_WIDTH):
        l0 = r0 + off + j
        b = l0 % SUBLANES
        m0 = b * CONV_SHROWS + (l0 - b)
        wj = cw_ref[j * SUBLANES:(j + 1) * SUBLANES, :]
        for k in range(n_sub):
            rk = m0 + k * SUBLANES
            acc[k] = acc[k] + shf_ref[rk:rk + SUBLANES, :] * wj
    y = _layer_norm(jnp.concatenate(acc, axis=0) + cb_ref[...], g_ref[...], b_ref[...])
    c_ref[base + r0:base + r0 + CONV_ROWS, :] = (y * _sigmoid(y)).astype(BF16)


def _inproj_kernel(x_ref, wa_ref, wb_ref, wf_ref, b_ref, wgu_ref, bgu_ref,
                   cw_ref, cb_ref, cg_ref, cbeta_ref,
                   c_ref, q_ref, k_ref, v_ref, rs_ref, sga_ref, sgb_ref, la_ref,
                   ext_ref, shf_ref, xb_ref):
    i = pl.program_id(0)
    xb_ref[...] = x_ref[...].astype(BF16)
    half = D_MODEL // 2

    def seg(c0, n):
        if c0 < N_PROJ_A:
            w = wa_ref[:, c0:c0 + n]
        elif c0 < N_PROJ_A + 2 * D_MODEL:
            w = wb_ref[:, c0 - N_PROJ_A:c0 - N_PROJ_A + n]
        else:
            w = wf_ref[...]
        return jnp.dot(xb_ref[...], w, preferred_element_type=F32) + b_ref[:, c0:c0 + n]

    @pl.when(i == 0)
    def _():
        ext_ref[...] = jnp.zeros_like(ext_ref)

    prev_tail = ext_ref[TM_PROJ:TM_PROJ + CONV_HALO, :]
    ext_ref[0:CONV_HALO, :] = jnp.where(
        i % (SEQ // TM_PROJ) == 0, jnp.zeros_like(prev_tail), prev_tail)
    for j in range(2):
        a = seg(j * half, half)
        g = seg(D_MODEL + j * half, half)
        ext_ref[CONV_HALO:CONV_HALO + TM_PROJ, j * half:(j + 1) * half] = (
            a * _sigmoid(g)).astype(BF16)

    def piece(ref, col, c0, fn):
        def run():
            ref[:, col:col + PROJ_PIECE] = fn(seg(c0 + col, PROJ_PIECE)).astype(BF16)
        return run

    mxu_pieces = []
    for ref, c0, width, fn in (
            (q_ref, 2 * D_MODEL, GLA_DK, lambda h: h * (GLA_HK ** -0.5)),
            (k_ref, 2 * D_MODEL + GLA_DK, GLA_DK, lambda h: h),
            (v_ref, 3 * D_MODEL, GLA_DV, lambda h: h),
            (rs_ref, 4 * D_MODEL, GLA_DV, lambda h: h * _sigmoid(h)),
            (sga_ref, 5 * D_MODEL, D_MODEL, _sigmoid),
            (sgb_ref, 6 * D_MODEL, D_MODEL, _sigmoid)):
        mxu_pieces += [piece(ref, col, c0, fn) for col in range(0, width, PROJ_PIECE)]

    def forget_piece():
        f = seg(7 * D_MODEL, F_PAD)
        z = jnp.dot(f.astype(BF16), wgu_ref[...], preferred_element_type=F32) + bgu_ref[...]
        la_ref[...] = ((jnp.minimum(z, 0.0) - jnp.log(1.0 + jnp.exp(-jnp.abs(z))))
                       * (1.0 / GATE_TAU))

    mxu_pieces.append(forget_piece)

    valu_pieces = []
    for sb in range(TM_PROJ // CONV_SUB):
        shf = shf_ref.at[sb % 2]
        for r0 in range(0, CONV_SUB, CONV_ROWS):
            def run(sb=sb, shf=shf, r0=r0):
                if r0 == 0:
                    _conv_shift(ext_ref, shf, sb * CONV_SUB)
                _conv_rows(shf, cw_ref, cb_ref, cg_ref, cbeta_ref, c_ref, sb * CONV_SUB, r0)
            valu_pieces.append(run)

    n_v, n_m = len(valu_pieces), len(mxu_pieces)
    done = 0
    for vi, vrun in enumerate(valu_pieces):
        vrun()
        upto = (vi + 1) * n_m // n_v
        for mrun in mxu_pieces[done:upto]:
            mrun()
        done = upto


def _inproj(x2, w_a, w_b, w_f, b_p, wgu_p, bgu, cw_p, cb, cg, cbeta):
    tm = TM_PROJ
    row = lambda i: (i, 0)
    fixed = lambda i: (0, 0)
    tok = lambda n, dt: jax.ShapeDtypeStruct((N_TOK, n), dt)
    vec = pl.BlockSpec((1, D_MODEL), fixed)
    return pl.pallas_call(
        _inproj_kernel,
        grid=(N_TOK // tm,),
        in_specs=[
            pl.BlockSpec((tm, D_MODEL), row),
            pl.BlockSpec((D_MODEL, N_PROJ_A), fixed, pipeline_mode=pl.Buffered(1)),
            pl.BlockSpec((D_MODEL, 2 * D_MODEL), fixed, pipeline_mode=pl.Buffered(1)),
            pl.BlockSpec((D_MODEL, F_PAD), fixed),
            pl.BlockSpec((1, N_PROJ), fixed),
            pl.BlockSpec((F_PAD, GLA_DK), fixed),
            pl.BlockSpec((1, GLA_DK), fixed),
            pl.BlockSpec((CONV_WIDTH * SUBLANES, D_MODEL), fixed),
            vec, vec, vec,
        ],
        out_specs=[
            pl.BlockSpec((tm, D_MODEL), row),
            pl.BlockSpec((tm, GLA_DK), row),
            pl.BlockSpec((tm, GLA_DK), row),
            pl.BlockSpec((tm, GLA_DV), row),
            pl.BlockSpec((tm, GLA_DV), row),
            pl.BlockSpec((tm, D_MODEL), row),
            pl.BlockSpec((tm, D_MODEL), row),
            pl.BlockSpec((tm, GLA_DK), row),
        ],
        out_shape=[
            tok(D_MODEL, BF16), tok(GLA_DK, BF16), tok(GLA_DK, BF16), tok(GLA_DV, BF16),
            tok(GLA_DV, BF16), tok(D_MODEL, BF16), tok(D_MODEL, BF16), tok(GLA_DK, F32),
        ],
        scratch_shapes=[
            pltpu.VMEM((tm + CONV_WIN - CONV_SUB, D_MODEL), BF16),
            pltpu.VMEM((2, SUBLANES * CONV_SHROWS, D_MODEL), F32),
            pltpu.VMEM((tm, D_MODEL), BF16),
        ],
        compiler_params=pltpu.CompilerParams(
            dimension_semantics=("arbitrary",), vmem_limit_bytes=VMEM_LIMIT),
        name="inproj",
    )(x2, w_a, w_b, w_f, b_p, wgu_p, bgu, cw_p, cb, cg, cbeta)


def _gla_kernel(q_ref, k_ref, v_ref, la_ref, rs_ref, ng_ref, o_ref,
                st_ref, kv_ref, sb_ref, dec_ref):
    s = pl.program_id(1)

    @pl.when(s == 0)
    def _():
        st_ref[...] = jnp.zeros_like(st_ref)

    rr = lax.broadcasted_iota(jnp.int32, (CHUNK, CHUNK), 0)
    cc = lax.broadcasted_iota(jnp.int32, (CHUNK, CHUNK), 1)
    tri = jnp.where(rr >= cc, 1.0, 0.0).astype(BF16)
    n_chunks = TS_GLA // CHUNK
    heads = [(slice(h * GLA_HK, (h + 1) * GLA_HK), slice(h * GLA_HV, (h + 1) * GLA_HV))
             for h in range(GLA_HEADS)]

    for nb in range(NB_GLA):
        for c in range(n_chunks):
            rows = slice(c * CHUNK, (c + 1) * CHUNK)
            la = la_ref[nb, rows, :]
            hi = la.astype(BF16)
            lo = (la - hi.astype(F32)).astype(BF16)
            cum = (jnp.dot(tri, hi, preferred_element_type=F32)
                   + jnp.dot(tri, lo, preferred_element_type=F32))
            cend = cum[CHUNK - 1:CHUNK, :]
            kd = (k_ref[nb, rows, :].astype(F32) * jnp.exp(cend - cum)).astype(BF16)
            idx = nb * n_chunks + c
            dec_ref[idx * SUBLANES:(idx + 1) * SUBLANES, :] = jnp.broadcast_to(
                jnp.exp(cend), (SUBLANES, GLA_DK))
            vc = v_ref[nb, rows, :]
            for h, (ks, vs) in enumerate(heads):
                kv_ref[idx * GLA_HEADS + h] = lax.dot_general(
                    vc[:, vs], kd[:, ks], (((0,), (0,)), ((), ())), preferred_element_type=F32)

    for nb in range(NB_GLA):
        for h, (ks, vs) in enumerate(heads):
            st = st_ref[nb * GLA_HEADS + h]
            for c in range(n_chunks):
                idx = nb * n_chunks + c
                st = st * dec_ref[idx * SUBLANES:idx * SUBLANES + 1, ks] + kv_ref[
                    idx * GLA_HEADS + h]
                sb_ref[idx * GLA_HEADS + h] = st.astype(BF16)
            st_ref[nb * GLA_HEADS + h] = st

    for nb in range(NB_GLA):
        for c in range(n_chunks):
            rows = slice(c * CHUNK, (c + 1) * CHUNK)
            idx = nb * n_chunks + c
            qc = q_ref[nb, rows, :]
            for h, (ks, vs) in enumerate(heads):
                o = lax.dot_general(qc[:, ks], sb_ref[idx * GLA_HEADS + h],
                                    (((1,), (1,)), ((), ())), preferred_element_type=F32)
                ms = jnp.mean(o * o, axis=-1, keepdims=True)
                on = o * lax.rsqrt(ms + RMS_EPS) * ng_ref[:, vs]
                o_ref[nb, rows, vs] = (on * rs_ref[nb, rows, vs].astype(F32)).astype(BF16)


def _gla(q, k, v, la, rs, ng):
    ts = TS_GLA
    blk = lambda n: pl.BlockSpec((NB_GLA, ts, n), lambda bi, si: (bi, si, 0))
    seq = lambda a: a.reshape(BATCH, SEQ, a.shape[-1])
    out = pl.pallas_call(
        _gla_kernel,
        grid=(BATCH // NB_GLA, SEQ // ts),
        in_specs=[blk(GLA_DK), blk(GLA_DK), blk(GLA_DV), blk(GLA_DK), blk(GLA_DV),
                  pl.BlockSpec((1, GLA_DV), lambda bi, si: (0, 0))],
        out_specs=blk(GLA_DV),
        out_shape=jax.ShapeDtypeStruct((BATCH, SEQ, GLA_DV), BF16),
        scratch_shapes=[
            pltpu.VMEM((NB_GLA * GLA_HEADS, GLA_HV, GLA_HK), F32),
            pltpu.VMEM((NB_GLA * (ts // CHUNK) * GLA_HEADS, GLA_HV, GLA_HK), F32),
            pltpu.VMEM((NB_GLA * (ts // CHUNK) * GLA_HEADS, GLA_HV, GLA_HK), BF16),
            pltpu.VMEM((NB_GLA * (ts // CHUNK) * SUBLANES, GLA_DK), F32),
        ],
        compiler_params=pltpu.CompilerParams(
            dimension_semantics=("arbitrary", "arbitrary"), vmem_limit_bytes=VMEM_LIMIT),
        name="gla",
    )(seq(q), seq(k), seq(v), seq(la), seq(rs), ng)
    return out.reshape(N_TOK, GLA_DV)


def _mix_kernel(c_ref, g_ref, sga_ref, sgb_ref, x_ref, wco_ref, bco_ref, wgl_ref, wo_ref, bo_ref,
                l1g_ref, l1b_ref, wrh_ref, wrl_ref, br_ref, tri_ref,
                x1_ref, route_ref, meta_ref, cnt_ref, carry_ref):
    i = pl.program_id(0)

    @pl.when(i == 0)
    def _():
        carry_ref[...] = jnp.zeros_like(carry_ref)

    yc = jnp.dot(c_ref[...], wco_ref[...], preferred_element_type=F32) + bco_ref[...]
    yg = jnp.dot(g_ref[...], wgl_ref[...], preferred_element_type=F32)
    merged = sga_ref[...].astype(F32) * yc + sgb_ref[...].astype(F32) * yg
    mix = jnp.dot(merged.astype(BF16), wo_ref[...], preferred_element_type=F32) + bo_ref[...]
    x1 = _layer_norm(DEEPNORM_ALPHA * x_ref[...] + mix, l1g_ref[...], l1b_ref[...])
    _to_token_tiles(x1_ref, x1)

    xh = x1.astype(BF16)
    xl = (x1 - xh.astype(F32)).astype(BF16)
    lg = (jnp.dot(xh, wrh_ref[...], preferred_element_type=F32)
          + jnp.dot(xl, wrh_ref[...], preferred_element_type=F32)
          + jnp.dot(xh, wrl_ref[...], preferred_element_type=F32)) + br_ref[...]

    tm = lg.shape[0]
    lane = lax.broadcasted_iota(jnp.int32, (tm, LANE), 1)
    neg = -jnp.inf
    big = jnp.int32(1 << 20)
    is_g = (lane >= N_EXPERTS) & (lane < N_EXPERTS + N_GROUPS)
    gl = jnp.where(is_g, lg, neg)
    gmax = jnp.max(gl, axis=-1, keepdims=True)
    gsel = jnp.min(jnp.where(gl == gmax, lane, big), axis=-1, keepdims=True) - N_EXPERTS
    gw = 1.0 / jnp.sum(jnp.exp(gl - gmax), axis=-1, keepdims=True)
    in_grp = (lane < N_EXPERTS) & ((lane >> 3) == gsel)
    el = jnp.where(in_grp, lg, neg)
    m1 = jnp.max(el, axis=-1, keepdims=True)
    i1 = jnp.min(jnp.where(el == m1, lane, big), axis=-1, keepdims=True)
    el2 = jnp.where(lane == i1, neg, el)
    m2 = jnp.max(el2, axis=-1, keepdims=True)
    i2 = jnp.min(jnp.where(el2 == m2, lane, big), axis=-1, keepdims=True)
    t = jnp.exp(m2 - m1)
    w1 = gw / (1.0 + t)
    w2 = gw * t / (1.0 + t)

    oh1 = lane == i1
    oh2 = lane == i2
    osum = jnp.where(oh1 | oh2, 1.0, 0.0)
    excl = jnp.dot(tri_ref[...], osum.astype(BF16), preferred_element_type=F32) + carry_ref[...]
    rank1 = jnp.sum(jnp.where(oh1, excl, 0.0), axis=-1, keepdims=True)
    rank2 = jnp.sum(jnp.where(oh2, excl, 0.0), axis=-1, keepdims=True)
    new_cnt = carry_ref[...] + jnp.sum(osum, axis=0, keepdims=True)
    carry_ref[...] = new_cnt
    cnt_ref[...] = new_cnt

    route = jnp.where(lane == 0, i1.astype(F32), 0.0)
    route = jnp.where(lane == 1, i2.astype(F32), route)
    route = jnp.where(lane == 2, rank1, route)
    route = jnp.where(lane == 3, rank2, route)
    route = jnp.where(lane == 4, w1, route)
    route = jnp.where(lane == 5, w2, route)
    route_ref[...] = route
    meta_ref[...] = route.T[0:META_ROWS, :].astype(jnp.int32)


def _mix(c, g, sga, sgb, x2, wco, bco, wgl, wo, bo, l1g, l1b, wrh, wrl, br, tri):
    tm = TM_MIX
    row = lambda i: (i, 0)
    fixed = lambda i: (0, 0)
    mat = pl.BlockSpec((D_MODEL, D_MODEL), fixed, pipeline_mode=pl.Buffered(1))
    vec = pl.BlockSpec((1, D_MODEL), fixed)
    return pl.pallas_call(
        _mix_kernel,
        grid=(N_TOK // tm,),
        in_specs=[
            pl.BlockSpec((tm, D_MODEL), row), pl.BlockSpec((tm, D_MODEL), row),
            pl.BlockSpec((tm, D_MODEL), row), pl.BlockSpec((tm, D_MODEL), row),
            pl.BlockSpec((tm, D_MODEL), row),
            mat, vec, mat, mat, vec, vec, vec,
            pl.BlockSpec((D_MODEL, LANE), fixed), pl.BlockSpec((D_MODEL, LANE), fixed),
            pl.BlockSpec((1, LANE), fixed),
            pl.BlockSpec((tm, tm), fixed),
        ],
        out_specs=[
            pl.BlockSpec((tm * TOKEN_ROWS, LANE), row),
            pl.BlockSpec((tm, LANE), row),
            pl.BlockSpec((META_ROWS, tm), lambda i: (0, i)),
            pl.BlockSpec((1, LANE), fixed),
        ],
        out_shape=[
            jax.ShapeDtypeStruct((N_TOK * TOKEN_ROWS, LANE), F32),
            jax.ShapeDtypeStruct((N_TOK, LANE), F32),
            jax.ShapeDtypeStruct((META_ROWS, N_TOK), jnp.int32),
            jax.ShapeDtypeStruct((1, LANE), F32),
        ],
        scratch_shapes=[pltpu.VMEM((1, LANE), F32)],
        compiler_params=pltpu.CompilerParams(
            dimension_semantics=("arbitrary",), vmem_limit_bytes=VMEM_LIMIT),
        name="mix",
    )(c, g, sga, sgb, x2, wco, bco, wgl, wo, bo, l1g, l1b, wrh, wrl, br, tri)


def _dispatch_kernel(cnt_ref, e0_ref, e1_ref, r0_ref, r1_ref, x1_ref,
                     xs_ref, d0_ref, d1_ref, ord_ref, oe_ref, nums_ref,
                     pstart_ref, zero_ref, dvm_ref, sem, zsem, dsem):
    i = pl.program_id(0)

    def zero_copy(b):
        rows = BM * TOKEN_ROWS
        return pltpu.make_async_copy(zero_ref, xs_ref.at[pl.ds(b * rows, rows)], zsem)

    @pl.when(i == 0)
    def _():
        zero_ref[...] = jnp.zeros_like(zero_ref)

        def plan(e, carry):
            bstart, n_ord = carry
            nb = (cnt_ref[e] + (BM - 1)) // BM
            pstart_ref[e] = bstart * BM

            def fill(j, c):
                ord_ref[bstart + j] = n_ord
                return c

            lax.fori_loop(0, nb, fill, 0)

            @pl.when(nb > 0)
            def _():
                oe_ref[n_ord] = e
                zero_copy(bstart + nb - 1).start()

            return bstart + nb, n_ord + jnp.where(nb > 0, 1, 0)

        n_used, n_ord = lax.fori_loop(0, N_EXPERTS, plan, (jnp.int32(0), jnp.int32(0)))
        nums_ref[0] = n_used
        nums_ref[1] = n_ord

        def fill_rest(j, c):
            ord_ref[j] = n_ord - 1
            zero_copy(j).start()
            return c

        lax.fori_loop(n_used, N_BLOCKS, fill_rest, 0)

        def fill_oe(j, c):
            oe_ref[j] = 0
            return c

        lax.fori_loop(n_ord, N_EXPERTS, fill_oe, 0)

        for e_ref, r_ref, d_ref, k in ((e0_ref, r0_ref, d0_ref, 0), (e1_ref, r1_ref, d1_ref, 1)):
            ev = e_ref[...]
            dv = r_ref[...]
            for e in range(N_EXPERTS):
                dv = dv + jnp.where(ev == e, pstart_ref[e], 0)
            dvm_ref[k] = dv
            pltpu.make_async_copy(dvm_ref.at[k], d_ref, dsem).start()
        for d_ref, k in ((d0_ref, 0), (d1_ref, 1)):
            pltpu.make_async_copy(dvm_ref.at[k], d_ref, dsem).wait()

        def drain(j, c):
            zero_copy(0).wait()
            return c

        lax.fori_loop(0, n_ord + (N_BLOCKS - n_used), drain, 0)

    base = i * TD_DISP

    def tok_group(tg, carry):
        drow = (base >> LANE_BITS) + tg
        for uu in range(DISP_UNROLL):
            tt = tg * DISP_UNROLL + uu
            src = x1_ref.at[pl.ds(pl.multiple_of(tt * TOKEN_ROWS, TOKEN_ROWS), TOKEN_ROWS)]
            for kk, d_ref in enumerate((d0_ref, d1_ref)):
                d = d_ref[drow, uu]
                pltpu.make_async_copy(
                    src, xs_ref.at[pl.ds(pl.multiple_of(d * TOKEN_ROWS, TOKEN_ROWS), TOKEN_ROWS)],
                    sem).start(priority=kk)
        return carry

    lax.fori_loop(0, TD_DISP // DISP_UNROLL, tok_group, 0)
    for kk in range(TOP_K):
        pltpu.make_async_copy(x1_ref, xs_ref.at[pl.ds(0, TD_DISP * TOKEN_ROWS)], sem).wait()


def _dispatch(e0, e1, r0, r1, cnt, x1):
    smem = pl.BlockSpec(memory_space=pltpu.SMEM)
    idx = pl.BlockSpec((N_TOK // LANE, LANE), lambda i, c: (0, 0))
    grid_spec = pltpu.PrefetchScalarGridSpec(
        num_scalar_prefetch=1,
        grid=(N_TOK // TD_DISP,),
        in_specs=[idx, idx, idx, idx,
                  pl.BlockSpec((TD_DISP * TOKEN_ROWS, LANE), lambda i, c: (i, 0))],
        out_specs=[pl.BlockSpec(memory_space=pl.ANY), smem, smem, smem, smem, smem],
        scratch_shapes=[
            pltpu.SMEM((N_EXPERTS,), jnp.int32),
            pltpu.VMEM((BM * TOKEN_ROWS, LANE), F32),
            pltpu.VMEM((TOP_K, N_TOK // LANE, LANE), jnp.int32),
            pltpu.SemaphoreType.DMA,
            pltpu.SemaphoreType.DMA,
            pltpu.SemaphoreType.DMA,
        ],
    )
    return pl.pallas_call(
        _dispatch_kernel,
        grid_spec=grid_spec,
        out_shape=[
            jax.ShapeDtypeStruct((N_SLOTS * TOKEN_ROWS, LANE), F32),
            jax.ShapeDtypeStruct((N_TOK // LANE, LANE), jnp.int32),
            jax.ShapeDtypeStruct((N_TOK // LANE, LANE), jnp.int32),
            jax.ShapeDtypeStruct((N_BLOCKS,), jnp.int32),
            jax.ShapeDtypeStruct((N_EXPERTS,), jnp.int32),
            jax.ShapeDtypeStruct((2,), jnp.int32),
        ],
        compiler_params=pltpu.CompilerParams(
            dimension_semantics=("arbitrary",), vmem_limit_bytes=VMEM_LIMIT),
        name="dispatch",
    )(cnt, e0, e1, r0, r1, x1)


def _expert_kernel(ord_ref, oe_ref, nums_ref, xs_ref, w1_hbm, w3_hbm, w2_hbm, ys_ref,
                   w1f_ref, w3f_ref, w2f_ref, w1b_ref, w3b_ref, w2b_ref, sem):
    i = pl.program_id(0)
    n_used = nums_ref[0]
    n_ord = nums_ref[1]
    active = i < n_used
    k = ord_ref[i]
    first = (i == 0) | (k != ord_ref[jnp.maximum(i - 1, 0)])

    def weight_copies(kk):
        e = oe_ref[kk]
        slot = kk % 2
        return [pltpu.make_async_copy(w_hbm.at[e], wf_ref.at[slot], sem.at[slot])
                for w_hbm, wf_ref in ((w1_hbm, w1f_ref), (w3_hbm, w3f_ref), (w2_hbm, w2f_ref))]

    @pl.when(active & first)
    def _():
        @pl.when(i == 0)
        def _():
            for cp in weight_copies(k):
                cp.start()

        @pl.when(k + 1 < n_ord)
        def _():
            for cp in weight_copies(k + 1):
                cp.start()

        for cp in weight_copies(k):
            cp.wait()
        slot = k % 2
        w1b_ref[...] = w1f_ref[slot].astype(BF16)
        w3b_ref[...] = w3f_ref[slot].astype(BF16)
        w2b_ref[...] = w2f_ref[slot].astype(BF16)

    @pl.when(active)
    def _():
        xb = _from_token_tiles(xs_ref, BM).astype(BF16)
        h1 = jnp.dot(xb, w1b_ref[...], preferred_element_type=F32)
        h3 = jnp.dot(xb, w3b_ref[...], preferred_element_type=F32)
        hdn = (h1 * _sigmoid(h1) * h3).astype(BF16)
        _to_token_tiles(ys_ref, jnp.dot(hdn, w2b_ref[...], preferred_element_type=F32))

    @pl.when(jnp.logical_not(active))
    def _():
        ys_ref[...] = jnp.zeros_like(ys_ref)


def _experts(blk_ord, ord_e, nums, xs, w1, w3, w2):
    last = lambda i, nums_ref: jnp.maximum(jnp.minimum(i, nums_ref[0] - 1), 0)
    hbm = pl.BlockSpec(memory_space=pl.ANY)
    grid_spec = pltpu.PrefetchScalarGridSpec(
        num_scalar_prefetch=3,
        grid=(N_BLOCKS,),
        in_specs=[
            pl.BlockSpec((BM * TOKEN_ROWS, LANE), lambda i, o, oe, nu: (last(i, nu), 0)),
            hbm, hbm, hbm,
        ],
        out_specs=pl.BlockSpec((BM * TOKEN_ROWS, LANE), lambda i, o, oe, nu: (i, 0)),
        scratch_shapes=[
            pltpu.VMEM((2, D_MODEL, D_EXPERT), F32),
            pltpu.VMEM((2, D_MODEL, D_EXPERT), F32),
            pltpu.VMEM((2, D_EXPERT, D_MODEL), F32),
            pltpu.VMEM((D_MODEL, D_EXPERT), BF16),
            pltpu.VMEM((D_MODEL, D_EXPERT), BF16),
            pltpu.VMEM((D_EXPERT, D_MODEL), BF16),
            pltpu.SemaphoreType.DMA((2,)),
        ],
    )
    return pl.pallas_call(
        _expert_kernel,
        grid_spec=grid_spec,
        out_shape=jax.ShapeDtypeStruct((N_SLOTS * TOKEN_ROWS, LANE), F32),
        compiler_params=pltpu.CompilerParams(
            dimension_semantics=("arbitrary",), vmem_limit_bytes=VMEM_LIMIT),
        name="experts",
    )(blk_ord, ord_e, nums, xs, w1, w3, w2)


def _combine_kernel(d0_ref, d1_ref, route_ref, x1_ref, ys_ref, g_ref, b_ref, o_ref, y_ref, sem):
    i = pl.program_id(0)
    n_tiles = pl.num_programs(0)
    tile_rows = TC_COMB * TOKEN_ROWS

    def issue(tile, slot, t0, n):
        drow = (tile * TC_COMB + t0) >> LANE_BITS
        for uu in range(n):
            t = t0 + uu
            dst = pl.ds(pl.multiple_of(t * TOKEN_ROWS, TOKEN_ROWS), TOKEN_ROWS)
            for kk, d_ref in enumerate((d0_ref, d1_ref)):
                d = d_ref[drow + uu // LANE, uu % LANE]
                src = pl.ds(pl.multiple_of(d * TOKEN_ROWS, TOKEN_ROWS), TOKEN_ROWS)
                pltpu.make_async_copy(ys_ref.at[src], y_ref.at[slot, kk, dst],
                                      sem.at[slot]).start(priority=kk)

    def finish(slot, t0):
        t0 = pl.multiple_of(t0, COMB_ROWS)
        w0 = route_ref[pl.ds(t0, COMB_ROWS), 4:5]
        w1 = route_ref[pl.ds(t0, COMB_ROWS), 5:6]
        z = (DEEPNORM_ALPHA * _from_token_tiles(x1_ref, COMB_ROWS, t0)
             + w0 * _from_token_tiles(y_ref.at[slot, 0], COMB_ROWS, t0)
             + w1 * _from_token_tiles(y_ref.at[slot, 1], COMB_ROWS, t0))
        o_ref[pl.ds(t0, COMB_ROWS), :] = _layer_norm(z, g_ref[...], b_ref[...])

    @pl.when(i == 0)
    def _():
        def first(gi, carry):
            issue(0, 0, gi * COMB_ROWS, COMB_ROWS)
            return carry

        lax.fori_loop(0, TC_COMB // COMB_ROWS, first, 0)

    slot = i % 2
    for kk in range(TOP_K):
        pltpu.make_async_copy(ys_ref.at[pl.ds(0, tile_rows)], y_ref.at[slot, kk],
                              sem.at[slot]).wait()

    @pl.when(i + 1 < n_tiles)
    def _():
        def both(gi, carry):
            issue(i + 1, 1 - slot, gi * COMB_ROWS, COMB_ROWS)
            finish(slot, gi * COMB_ROWS)
            return carry

        lax.fori_loop(0, TC_COMB // COMB_ROWS, both, 0)

    @pl.when(i + 1 == n_tiles)
    def _():
        def last(gi, carry):
            finish(slot, gi * COMB_ROWS)
            return carry

        lax.fori_loop(0, TC_COMB // COMB_ROWS, last, 0)


def _combine(d0, d1, route, x1, ys, g, b):
    tc = TC_COMB
    grid_spec = pltpu.PrefetchScalarGridSpec(
        num_scalar_prefetch=2,
        grid=(N_TOK // tc,),
        in_specs=[
            pl.BlockSpec((tc, LANE), lambda i, *_: (i, 0)),
            pl.BlockSpec((tc * TOKEN_ROWS, LANE), lambda i, *_: (i, 0)),
            pl.BlockSpec(memory_space=pl.ANY),
            pl.BlockSpec((1, D_MODEL), lambda i, *_: (0, 0)),
            pl.BlockSpec((1, D_MODEL), lambda i, *_: (0, 0)),
        ],
        out_specs=pl.BlockSpec((tc, D_MODEL), lambda i, *_: (i, 0)),
        scratch_shapes=[
            pltpu.VMEM((2, TOP_K, tc * TOKEN_ROWS, LANE), F32),
            pltpu.SemaphoreType.DMA((2,)),
        ],
    )
    return pl.pallas_call(
        _combine_kernel,
        grid_spec=grid_spec,
        out_shape=jax.ShapeDtypeStruct((N_TOK, D_MODEL), F32),
        compiler_params=pltpu.CompilerParams(
            dimension_semantics=("arbitrary",), vmem_limit_bytes=VMEM_LIMIT),
        name="combine",
    )(d0, d1, route, x1, ys, g, b)


def _split_bf16(w):
    hi = w.astype(BF16)
    lo = (w - hi.astype(F32)).astype(BF16)
    return hi, lo


def kernel(x, w_in, b_in, conv_w, conv_b, conv_ln_g, conv_ln_b, w_conv_out, b_conv_out, w_gate_up, b_gate_up, gla_norm_g, w_gla_out, w_out, b_out, ln1_g, ln1_b, w_router_group, b_router_group, w_router_expert, b_router_expert, w1, w3, w2, ln2_g, ln2_b):
    x2 = x.reshape(N_TOK, D_MODEL)
    row = lambda v: v.reshape(1, -1)
    for l in range(w_in.shape[0]):
        f0 = N_PROJ_A
        f1 = f0 + GATE_RANK
        w_l, b_l = w_in[l], b_in[l]
        w_a = w_l[:, :f0].astype(BF16)
        w_b = w_l[:, f1:].astype(BF16)
        w_f = jnp.pad(w_l[:, f0:f1], ((0, 0), (0, F_PAD - GATE_RANK))).astype(BF16)
        b_p = row(jnp.concatenate(
            [b_l[:f0], b_l[f1:], b_l[f0:f1], jnp.zeros((F_PAD - GATE_RANK,), F32)]))
        wgu_p = jnp.concatenate(
            [w_gate_up[l], jnp.zeros((F_PAD - GATE_RANK, GLA_DK), F32)], axis=0).astype(BF16)
        cw_p = jnp.repeat(conv_w[l], SUBLANES, axis=0)
        c, q, k, v, rs, sga, sgb, la = _inproj(
            x2, w_a, w_b, w_f, b_p, wgu_p, row(b_gate_up[l]), cw_p,
            row(conv_b[l]), row(conv_ln_g[l]), row(conv_ln_b[l]))
        g = _gla(q, k, v, la, rs, row(gla_norm_g[l]))

        w_r = jnp.concatenate(
            [w_router_expert[l], w_router_group[l],
             jnp.zeros((D_MODEL, LANE - N_EXPERTS - N_GROUPS), F32)], axis=1)
        b_r = row(jnp.concatenate(
            [b_router_expert[l], b_router_group[l],
             jnp.zeros((LANE - N_EXPERTS - N_GROUPS,), F32)]))
        wrh, wrl = _split_bf16(w_r)
        ri = lax.broadcasted_iota(jnp.int32, (TM_MIX, TM_MIX), 0)
        ci = lax.broadcasted_iota(jnp.int32, (TM_MIX, TM_MIX), 1)
        tri = (ri > ci).astype(BF16)
        x1, route, meta, cnt = _mix(
            c, g, sga, sgb, x2, w_conv_out[l].astype(BF16), row(b_conv_out[l]),
            w_gla_out[l].astype(BF16), w_out[l].astype(BF16), row(b_out[l]),
            row(ln1_g[l]), row(ln1_b[l]), wrh, wrl, b_r, tri)

        cnt_i = cnt[0, :N_EXPERTS].astype(jnp.int32)
        tok_grid = lambda m: m.reshape(N_TOK // LANE, LANE)
        xs, d0, d1, blk_ord, ord_e, nums = _dispatch(
            tok_grid(meta[0]), tok_grid(meta[1]), tok_grid(meta[2]), tok_grid(meta[3]), cnt_i, x1)
        ys = _experts(blk_ord, ord_e, nums, xs, w1[l], w3[l], w2[l])
        x2 = _combine(d0, d1, route, x1, ys, row(ln2_g[l]), row(ln2_b[l]))
    return x2.reshape(x.shape)
```

```python
import jax
import jax.numpy as jnp
from jax import lax
from jax.experimental import pallas as pl
from jax.experimental.pallas import tpu as pltpu

F32 = jnp.float32
BF16 = jnp.bfloat16

D_MODEL = 1024
BATCH = 8
SEQ = 2048
N_TOK = BATCH * SEQ
CHUNK = 64
CONV_WIDTH = 31
GLA_HEADS = 4
GLA_DK = 512
GLA_DV = 1024
GLA_HK = 128
GLA_HV = 256
GATE_RANK = 16
GATE_TAU = 16.0
N_GROUPS = 8
EXPERTS_PER_GROUP = 8
N_EXPERTS = 64
TOP_K = 2
D_EXPERT = 512
LN_EPS = 1e-5
RMS_EPS = 1e-6
DEEPNORM_ALPHA = 2.0 ** 0.25

LANE = 128
LANE_BITS = LANE.bit_length() - 1
TOKEN_ROWS = D_MODEL // LANE
F_PAD = LANE
N_PROJ_A = 2 * D_MODEL + 2 * GLA_DK + 2 * GLA_DV
N_PROJ = N_PROJ_A + 2 * D_MODEL + F_PAD
TM_PROJ = 512
PROJ_PIECE = 256
CONV_HALO = 32
CONV_ROWS = 32
SUBLANES = 8
CONV_SUB = 128
CONV_SHROWS = CONV_SUB + CONV_HALO
CONV_WIN = CONV_SHROWS + 16
TS_GLA = 256
NB_GLA = 4
TM_MIX = 512
TM_SUB = 256
META_ROWS = 8
BM = 256
W_SLOTS = 3
N_BLOCKS = N_TOK * TOP_K // BM + N_EXPERTS
N_SLOTS = N_BLOCKS * BM
TC_COMB = 256
COMB_ROWS = 128
TD_DISP = 2048
DISP_UNROLL = LANE
VMEM_LIMIT = 56 * 1024 * 1024


def _sigmoid(x):
    return 1.0 / (1.0 + jnp.exp(-x))


def _to_token_tiles(ref, val, t0=0):
    n = val.shape[0]
    for c in range(TOKEN_ROWS):
        ref[pl.ds(t0 * TOKEN_ROWS + c, n, stride=TOKEN_ROWS), :] = val[:, c * LANE:(c + 1) * LANE]


def _from_token_tiles(ref, n, t0=0):
    return jnp.concatenate(
        [ref[pl.ds(t0 * TOKEN_ROWS + c, n, stride=TOKEN_ROWS), :] for c in range(TOKEN_ROWS)],
        axis=1)


def _layer_norm(z, g, b):
    mu = jnp.mean(z, axis=-1, keepdims=True)
    zc = z - mu
    var = jnp.mean(zc * zc, axis=-1, keepdims=True)
    return zc * lax.rsqrt(var + LN_EPS) * g + b


def _conv_shift(ext_ref, shf_ref, base):
    rows = CONV_WIN
    win = ext_ref[base:base + rows, :].astype(F32)
    for b in range(SUBLANES):
        sh = win if b == 0 else pltpu.roll(win, rows - b, axis=0)
        shf_ref[b * CONV_SHROWS:(b + 1) * CONV_SHROWS, :] = sh[0:CONV_SHROWS, :]


def _conv_rows(shf_ref, cw_ref, cb_ref, g_ref, b_ref, c_ref, base, r0):
    off = CONV_HALO - (CONV_WIDTH - 1)
    n_sub = CONV_ROWS // SUBLANES
    acc = [jnp.zeros((SUBLANES, D_MODEL), F32) for _ in range(n_sub)]
    for j in range(CONV_WIDTH):
        l0 = r0 + off + j
        b = l0 % SUBLANES
        m0 = b * CONV_SHROWS + (l0 - b)
        wj = cw_ref[j * SUBLANES:(j + 1) * SUBLANES, :]
        for k in range(n_sub):
            rk = m0 + k * SUBLANES
            acc[k] = acc[k] + shf_ref[rk:rk + SUBLANES, :] * wj
    y = _layer_norm(jnp.concatenate(acc, axis=0) + cb_ref[...], g_ref[...], b_ref[...])
    c_ref[base + r0:base + r0 + CONV_ROWS, :] = (y * _sigmoid(y)).astype(BF16)


def _inproj_kernel(x_ref, wa_ref, wb_ref, wf_ref, b_ref, wgu_ref, bgu_ref,
                   cw_ref, cb_ref, cg_ref, cbeta_ref,
                   c_ref, q_ref, k_ref, v_ref, rs_ref, sga_ref, sgb_ref, la_ref,
                   ext_ref, shf_ref, xb_ref):
    i = pl.program_id(0)
    xb_ref[...] = x_ref[...].astype(BF16)
    half = D_MODEL // 2

    def seg(c0, n):
        if c0 < N_PROJ_A:
            w = wa_ref[:, c0:c0 + n]
        elif c0 < N_PROJ_A + 2 * D_MODEL:
            w = wb_ref[:, c0 - N_PROJ_A:c0 - N_PROJ_A + n]
        else:
            w = wf_ref[...]
        return jnp.dot(xb_ref[...], w, preferred_element_type=F32) + b_ref[:, c0:c0 + n]

    @pl.when(i == 0)
    def _():
        ext_ref[...] = jnp.zeros_like(ext_ref)

    prev_tail = ext_ref[TM_PROJ:TM_PROJ + CONV_HALO, :]
    ext_ref[0:CONV_HALO, :] = jnp.where(
        i % (SEQ // TM_PROJ) == 0, jnp.zeros_like(prev_tail), prev_tail)
    for j in range(2):
        a = seg(j * half, half)
        g = seg(D_MODEL + j * half, half)
        ext_ref[CONV_HALO:CONV_HALO + TM_PROJ, j * half:(j + 1) * half] = (
            a * _sigmoid(g)).astype(BF16)

    def piece(ref, col, c0, fn):
        def run():
            ref[:, col:col + PROJ_PIECE] = fn(seg(c0 + col, PROJ_PIECE)).astype(BF16)
        return run

    mxu_pieces = []
    for ref, c0, width, fn in (
            (q_ref, 2 * D_MODEL, GLA_DK, lambda h: h * (GLA_HK ** -0.5)),
            (k_ref, 2 * D_MODEL + GLA_DK, GLA_DK, lambda h: h),
            (v_ref, 3 * D_MODEL, GLA_DV, lambda h: h),
            (rs_ref, 4 * D_MODEL, GLA_DV, lambda h: h * _sigmoid(h)),
            (sga_ref, 5 * D_MODEL, D_MODEL, _sigmoid),
            (sgb_ref, 6 * D_MODEL, D_MODEL, _sigmoid)):
        mxu_pieces += [piece(ref, col, c0, fn) for col in range(0, width, PROJ_PIECE)]

    def forget_piece():
        f = seg(7 * D_MODEL, F_PAD)
        z = jnp.dot(f.astype(BF16), wgu_ref[...], preferred_element_type=F32) + bgu_ref[...]
        la_ref[...] = ((jnp.minimum(z, 0.0) - jnp.log(1.0 + jnp.exp(-jnp.abs(z))))
                       * (1.0 / GATE_TAU))

    mxu_pieces.append(forget_piece)

    valu_pieces = []
    for sb in range(TM_PROJ // CONV_SUB):
        shf = shf_ref.at[sb % 2]
        for r0 in range(0, CONV_SUB, CONV_ROWS):
            def run(sb=sb, shf=shf, r0=r0):
                if r0 == 0:
                    _conv_shift(ext_ref, shf, sb * CONV_SUB)
                _conv_rows(shf, cw_ref, cb_ref, cg_ref, cbeta_ref, c_ref, sb * CONV_SUB, r0)
            valu_pieces.append(run)

    n_v, n_m = len(valu_pieces), len(mxu_pieces)
    done = 0
    for vi, vrun in enumerate(valu_pieces):
        vrun()
        upto = (vi + 1) * n_m // n_v
        for mrun in mxu_pieces[done:upto]:
            mrun()
        done = upto


def _inproj(x2, w_a, w_b, w_f, b_p, wgu_p, bgu, cw_p, cb, cg, cbeta):
    tm = TM_PROJ
    row = lambda i: (i, 0)
    fixed = lambda i: (0, 0)
    tok = lambda n, dt: jax.ShapeDtypeStruct((N_TOK, n), dt)
    vec = pl.BlockSpec((1, D_MODEL), fixed)
    return pl.pallas_call(
        _inproj_kernel,
        grid=(N_TOK // tm,),
        in_specs=[
            pl.BlockSpec((tm, D_MODEL), row),
            pl.BlockSpec((D_MODEL, N_PROJ_A), fixed, pipeline_mode=pl.Buffered(1)),
            pl.BlockSpec((D_MODEL, 2 * D_MODEL), fixed, pipeline_mode=pl.Buffered(1)),
            pl.BlockSpec((D_MODEL, F_PAD), fixed),
            pl.BlockSpec((1, N_PROJ), fixed),
            pl.BlockSpec((F_PAD, GLA_DK), fixed),
            pl.BlockSpec((1, GLA_DK), fixed),
            pl.BlockSpec((CONV_WIDTH * SUBLANES, D_MODEL), fixed),
            vec, vec, vec,
        ],
        out_specs=[
            pl.BlockSpec((tm, D_MODEL), row),
            pl.BlockSpec((tm, GLA_DK), row),
            pl.BlockSpec((tm, GLA_DK), row),
            pl.BlockSpec((tm, GLA_DV), row),
            pl.BlockSpec((tm, GLA_DV), row),
            pl.BlockSpec((tm, D_MODEL), row),
            pl.BlockSpec((tm, D_MODEL), row),
            pl.BlockSpec((tm, GLA_DK), row),
        ],
        out_shape=[
            tok(D_MODEL, BF16), tok(GLA_DK, BF16), tok(GLA_DK, BF16), tok(GLA_DV, BF16),
            tok(GLA_DV, BF16), tok(D_MODEL, BF16), tok(D_MODEL, BF16), tok(GLA_DK, F32),
        ],
        scratch_shapes=[
            pltpu.VMEM((tm + CONV_WIN - CONV_SUB, D_MODEL), BF16),
            pltpu.VMEM((2, SUBLANES * CONV_SHROWS, D_MODEL), F32),
            pltpu.VMEM((tm, D_MODEL), BF16),
        ],
        compiler_params=pltpu.CompilerParams(
            dimension_semantics=("arbitrary",), vmem_limit_bytes=VMEM_LIMIT),
        name="inproj",
    )(x2, w_a, w_b, w_f, b_p, wgu_p, bgu, cw_p, cb, cg, cbeta)


def _gla_kernel(q_ref, k_ref, v_ref, la_ref, rs_ref, ng_ref, o_ref,
                st_ref, kv_ref, sb_ref, dec_ref):
    s = pl.program_id(1)

    @pl.when(s == 0)
    def _():
        st_ref[...] = jnp.zeros_like(st_ref)

    rr = lax.broadcasted_iota(jnp.int32, (CHUNK, CHUNK), 0)
    cc = lax.broadcasted_iota(jnp.int32, (CHUNK, CHUNK), 1)
    tri = jnp.where(rr >= cc, 1.0, 0.0).astype(BF16)
    n_chunks = TS_GLA // CHUNK
    heads = [(slice(h * GLA_HK, (h + 1) * GLA_HK), slice(h * GLA_HV, (h + 1) * GLA_HV))
             for h in range(GLA_HEADS)]

    for nb in range(NB_GLA):
        for c in range(n_chunks):
            rows = slice(c * CHUNK, (c + 1) * CHUNK)
            la = la_ref[nb, rows, :]
            hi = la.astype(BF16)
            lo = (la - hi.astype(F32)).astype(BF16)
            cum = (jnp.dot(tri, hi, preferred_element_type=F32)
                   + jnp.dot(tri, lo, preferred_element_type=F32))
            cend = cum[CHUNK - 1:CHUNK, :]
            kd = (k_ref[nb, rows, :].astype(F32) * jnp.exp(cend - cum)).astype(BF16)
            idx = nb * n_chunks + c
            dec_ref[idx * SUBLANES:(idx + 1) * SUBLANES, :] = jnp.broadcast_to(
                jnp.exp(cend), (SUBLANES, GLA_DK))
            vc = v_ref[nb, rows, :]
            for h, (ks, vs) in enumerate(heads):
                kv_ref[idx * GLA_HEADS + h] = lax.dot_general(
                    vc[:, vs], kd[:, ks], (((0,), (0,)), ((), ())), preferred_element_type=F32)

    for nb in range(NB_GLA):
        for h, (ks, vs) in enumerate(heads):
            st = st_ref[nb * GLA_HEADS + h]
            for c in range(n_chunks):
                idx = nb * n_chunks + c
                st = st * dec_ref[idx * SUBLANES:idx * SUBLANES + 1, ks] + kv_ref[
                    idx * GLA_HEADS + h]
                sb_ref[idx * GLA_HEADS + h] = st.astype(BF16)
            st_ref[nb * GLA_HEADS + h] = st

    for nb in range(NB_GLA):
        for c in range(n_chunks):
            rows = slice(c * CHUNK, (c + 1) * CHUNK)
            idx = nb * n_chunks + c
            qc = q_ref[nb, rows, :]
            for h, (ks, vs) in enumerate(heads):
                o = lax.dot_general(qc[:, ks], sb_ref[idx * GLA_HEADS + h],
                                    (((1,), (1,)), ((), ())), preferred_element_type=F32)
                ms = jnp.mean(o * o, axis=-1, keepdims=True)
                on = o * lax.rsqrt(ms + RMS_EPS) * ng_ref[:, vs]
                o_ref[nb, rows, vs] = (on * rs_ref[nb, rows, vs].astype(F32)).astype(BF16)


def _gla(q, k, v, la, rs, ng):
    ts = TS_GLA
    blk = lambda n: pl.BlockSpec((NB_GLA, ts, n), lambda bi, si: (bi, si, 0))
    seq = lambda a: a.reshape(BATCH, SEQ, a.shape[-1])
    out = pl.pallas_call(
        _gla_kernel,
        grid=(BATCH // NB_GLA, SEQ // ts),
        in_specs=[blk(GLA_DK), blk(GLA_DK), blk(GLA_DV), blk(GLA_DK), blk(GLA_DV),
                  pl.BlockSpec((1, GLA_DV), lambda bi, si: (0, 0))],
        out_specs=blk(GLA_DV),
        out_shape=jax.ShapeDtypeStruct((BATCH, SEQ, GLA_DV), BF16),
        scratch_shapes=[
            pltpu.VMEM((NB_GLA * GLA_HEADS, GLA_HV, GLA_HK), F32),
            pltpu.VMEM((NB_GLA * (ts // CHUNK) * GLA_HEADS, GLA_HV, GLA_HK), F32),
            pltpu.VMEM((NB_GLA * (ts // CHUNK) * GLA_HEADS, GLA_HV, GLA_HK), BF16),
            pltpu.VMEM((NB_GLA * (ts // CHUNK) * SUBLANES, GLA_DK), F32),
        ],
        compiler_params=pltpu.CompilerParams(
            dimension_semantics=("arbitrary", "arbitrary"), vmem_limit_bytes=VMEM_LIMIT),
        name="gla",
    )(seq(q), seq(k), seq(v), seq(la), seq(rs), ng)
    return out.reshape(N_TOK, GLA_DV)


def _mix_kernel(c_ref, g_ref, sga_ref, sgb_ref, x_ref, wco_ref, bco_ref, wgl_ref, wo_ref, bo_ref,
                l1g_ref, l1b_ref, wrh_ref, wrl_ref, br_ref, tri_ref,
                x1_ref, route_ref, meta_ref, cnt_ref, carry_ref):
    i = pl.program_id(0)

    @pl.when(i == 0)
    def _():
        carry_ref[...] = jnp.zeros_like(carry_ref)

    def rows_of(sub):
        return slice(sub * TM_SUB, (sub + 1) * TM_SUB)

    def branch_projections(sub):
        rows = rows_of(sub)
        yc = jnp.dot(c_ref[rows, :], wco_ref[...], preferred_element_type=F32) + bco_ref[...]
        yg = jnp.dot(g_ref[rows, :], wgl_ref[...], preferred_element_type=F32)
        return yc, yg

    def out_projection(sub, yc, yg):
        rows = rows_of(sub)
        merged = sga_ref[rows, :].astype(F32) * yc + sgb_ref[rows, :].astype(F32) * yg
        return jnp.dot(merged.astype(BF16), wo_ref[...], preferred_element_type=F32) + bo_ref[...]

    def norm_and_logits(sub, mix):
        x1 = _layer_norm(DEEPNORM_ALPHA * x_ref[rows_of(sub), :] + mix, l1g_ref[...], l1b_ref[...])
        _to_token_tiles(x1_ref, x1, sub * TM_SUB)
        xh = x1.astype(BF16)
        xl = (x1 - xh.astype(F32)).astype(BF16)
        return (jnp.dot(xh, wrh_ref[...], preferred_element_type=F32)
                + jnp.dot(xl, wrh_ref[...], preferred_element_type=F32)
                + jnp.dot(xh, wrl_ref[...], preferred_element_type=F32)) + br_ref[...]

    def route(sub, lg):
        rows = rows_of(sub)
        lane = lax.broadcasted_iota(jnp.int32, (TM_SUB, LANE), 1)
        neg = -jnp.inf
        big = jnp.int32(1 << 20)
        is_g = (lane >= N_EXPERTS) & (lane < N_EXPERTS + N_GROUPS)
        gl = jnp.where(is_g, lg, neg)
        gmax = jnp.max(gl, axis=-1, keepdims=True)
        gsel = jnp.min(jnp.where(gl == gmax, lane, big), axis=-1, keepdims=True) - N_EXPERTS
        gw = 1.0 / jnp.sum(jnp.exp(gl - gmax), axis=-1, keepdims=True)
        in_grp = (lane < N_EXPERTS) & ((lane >> 3) == gsel)
        el = jnp.where(in_grp, lg, neg)
        m1 = jnp.max(el, axis=-1, keepdims=True)
        i1 = jnp.min(jnp.where(el == m1, lane, big), axis=-1, keepdims=True)
        el2 = jnp.where(lane == i1, neg, el)
        m2 = jnp.max(el2, axis=-1, keepdims=True)
        i2 = jnp.min(jnp.where(el2 == m2, lane, big), axis=-1, keepdims=True)
        t = jnp.exp(m2 - m1)
        w1 = gw / (1.0 + t)
        w2 = gw * t / (1.0 + t)

        oh1 = lane == i1
        oh2 = lane == i2
        osum = jnp.where(oh1 | oh2, 1.0, 0.0)
        excl = (jnp.dot(tri_ref[...], osum.astype(BF16), preferred_element_type=F32)
                + carry_ref[...])
        rank1 = jnp.sum(jnp.where(oh1, excl, 0.0), axis=-1, keepdims=True)
        rank2 = jnp.sum(jnp.where(oh2, excl, 0.0), axis=-1, keepdims=True)
        carry_ref[...] = carry_ref[...] + jnp.sum(osum, axis=0, keepdims=True)

        out = jnp.where(lane == 0, i1.astype(F32), 0.0)
        out = jnp.where(lane == 1, i2.astype(F32), out)
        out = jnp.where(lane == 2, rank1, out)
        out = jnp.where(lane == 3, rank2, out)
        out = jnp.where(lane == 4, w1, out)
        out = jnp.where(lane == 5, w2, out)
        route_ref[rows, :] = out
        meta_ref[:, rows] = out.T[0:META_ROWS, :].astype(jnp.int32)

    subs = range(TM_MIX // TM_SUB)
    branches = [branch_projections(sub) for sub in subs]
    mixes = [out_projection(sub, *branches[sub]) for sub in subs]
    logits = [norm_and_logits(sub, mixes[sub]) for sub in subs]
    for sub in subs:
        route(sub, logits[sub])
    cnt_ref[...] = carry_ref[...]


def _mix(c, g, sga, sgb, x2, wco, bco, wgl, wo, bo, l1g, l1b, wrh, wrl, br, tri):
    tm = TM_MIX
    row = lambda i: (i, 0)
    fixed = lambda i: (0, 0)
    mat = pl.BlockSpec((D_MODEL, D_MODEL), fixed, pipeline_mode=pl.Buffered(1))
    vec = pl.BlockSpec((1, D_MODEL), fixed)
    return pl.pallas_call(
        _mix_kernel,
        grid=(N_TOK // tm,),
        in_specs=[
            pl.BlockSpec((tm, D_MODEL), row), pl.BlockSpec((tm, D_MODEL), row),
            pl.BlockSpec((tm, D_MODEL), row), pl.BlockSpec((tm, D_MODEL), row),
            pl.BlockSpec((tm, D_MODEL), row),
            mat, vec, mat, mat, vec, vec, vec,
            pl.BlockSpec((D_MODEL, LANE), fixed), pl.BlockSpec((D_MODEL, LANE), fixed),
            pl.BlockSpec((1, LANE), fixed),
            pl.BlockSpec((TM_SUB, TM_SUB), fixed),
        ],
        out_specs=[
            pl.BlockSpec((tm * TOKEN_ROWS, LANE), row),
            pl.BlockSpec((tm, LANE), row),
            pl.BlockSpec((META_ROWS, tm), lambda i: (0, i)),
            pl.BlockSpec((1, LANE), fixed),
        ],
        out_shape=[
            jax.ShapeDtypeStruct((N_TOK * TOKEN_ROWS, LANE), F32),
            jax.ShapeDtypeStruct((N_TOK, LANE), F32),
            jax.ShapeDtypeStruct((META_ROWS, N_TOK), jnp.int32),
            jax.ShapeDtypeStruct((1, LANE), F32),
        ],
        scratch_shapes=[pltpu.VMEM((1, LANE), F32)],
        compiler_params=pltpu.CompilerParams(
            dimension_semantics=("arbitrary",), vmem_limit_bytes=VMEM_LIMIT),
        name="mix",
    )(c, g, sga, sgb, x2, wco, bco, wgl, wo, bo, l1g, l1b, wrh, wrl, br, tri)


def _dispatch_kernel(cnt_ref, e0_ref, e1_ref, r0_ref, r1_ref, x1_ref,
                     xs_ref, d0_ref, d1_ref, ord_ref, oe_ref, nums_ref,
                     pstart_ref, zero_ref, dvm_ref, sem, zsem, dsem):
    i = pl.program_id(0)

    def zero_copy(b):
        rows = BM * TOKEN_ROWS
        return pltpu.make_async_copy(zero_ref, xs_ref.at[pl.ds(b * rows, rows)], zsem)

    @pl.when(i == 0)
    def _():
        zero_ref[...] = jnp.zeros_like(zero_ref)

        def plan(e, carry):
            bstart, n_ord = carry
            nb = (cnt_ref[e] + (BM - 1)) // BM
            pstart_ref[e] = bstart * BM

            def fill(j, c):
                ord_ref[bstart + j] = n_ord
                return c

            lax.fori_loop(0, nb, fill, 0)

            @pl.when(nb > 0)
            def _():
                oe_ref[n_ord] = e
                zero_copy(bstart + nb - 1).start()

            return bstart + nb, n_ord + jnp.where(nb > 0, 1, 0)

        n_used, n_ord = lax.fori_loop(0, N_EXPERTS, plan, (jnp.int32(0), jnp.int32(0)))
        nums_ref[0] = n_used
        nums_ref[1] = n_ord

        def fill_rest(j, c):
            ord_ref[j] = n_ord - 1
            zero_copy(j).start()
            return c

        lax.fori_loop(n_used, N_BLOCKS, fill_rest, 0)

        def fill_oe(j, c):
            oe_ref[j] = 0
            return c

        lax.fori_loop(n_ord, N_EXPERTS, fill_oe, 0)

        for e_ref, r_ref, d_ref, k in ((e0_ref, r0_ref, d0_ref, 0), (e1_ref, r1_ref, d1_ref, 1)):
            ev = e_ref[...]
            dv = r_ref[...]
            for e in range(N_EXPERTS):
                dv = dv + jnp.where(ev == e, pstart_ref[e], 0)
            dvm_ref[k] = dv
            pltpu.make_async_copy(dvm_ref.at[k], d_ref, dsem).start()
        for d_ref, k in ((d0_ref, 0), (d1_ref, 1)):
            pltpu.make_async_copy(dvm_ref.at[k], d_ref, dsem).wait()

        def drain(j, c):
            zero_copy(0).wait()
            return c

        lax.fori_loop(0, n_ord + (N_BLOCKS - n_used), drain, 0)

    base = i * TD_DISP

    def tok_group(tg, carry):
        drow = (base >> LANE_BITS) + tg
        for uu in range(DISP_UNROLL):
            tt = tg * DISP_UNROLL + uu
            src = x1_ref.at[pl.ds(pl.multiple_of(tt * TOKEN_ROWS, TOKEN_ROWS), TOKEN_ROWS)]
            for kk, d_ref in enumerate((d0_ref, d1_ref)):
                d = d_ref[drow, uu]
                pltpu.make_async_copy(
                    src, xs_ref.at[pl.ds(pl.multiple_of(d * TOKEN_ROWS, TOKEN_ROWS), TOKEN_ROWS)],
                    sem).start(priority=kk)
        return carry

    lax.fori_loop(0, TD_DISP // DISP_UNROLL, tok_group, 0)
    for kk in range(TOP_K):
        pltpu.make_async_copy(x1_ref, xs_ref.at[pl.ds(0, TD_DISP * TOKEN_ROWS)], sem).wait()


def _dispatch(e0, e1, r0, r1, cnt, x1):
    smem = pl.BlockSpec(memory_space=pltpu.SMEM)
    idx = pl.BlockSpec((N_TOK // LANE, LANE), lambda i, c: (0, 0))
    grid_spec = pltpu.PrefetchScalarGridSpec(
        num_scalar_prefetch=1,
        grid=(N_TOK // TD_DISP,),
        in_specs=[idx, idx, idx, idx,
                  pl.BlockSpec((TD_DISP * TOKEN_ROWS, LANE), lambda i, c: (i, 0))],
        out_specs=[pl.BlockSpec(memory_space=pl.ANY), smem, smem, smem, smem, smem],
        scratch_shapes=[
            pltpu.SMEM((N_EXPERTS,), jnp.int32),
            pltpu.VMEM((BM * TOKEN_ROWS, LANE), F32),
            pltpu.VMEM((TOP_K, N_TOK // LANE, LANE), jnp.int32),
            pltpu.SemaphoreType.DMA,
            pltpu.SemaphoreType.DMA,
            pltpu.SemaphoreType.DMA,
        ],
    )
    return pl.pallas_call(
        _dispatch_kernel,
        grid_spec=grid_spec,
        out_shape=[
            jax.ShapeDtypeStruct((N_SLOTS * TOKEN_ROWS, LANE), F32),
            jax.ShapeDtypeStruct((N_TOK // LANE, LANE), jnp.int32),
            jax.ShapeDtypeStruct((N_TOK // LANE, LANE), jnp.int32),
            jax.ShapeDtypeStruct((N_BLOCKS,), jnp.int32),
            jax.ShapeDtypeStruct((N_EXPERTS,), jnp.int32),
            jax.ShapeDtypeStruct((2,), jnp.int32),
        ],
        compiler_params=pltpu.CompilerParams(
            dimension_semantics=("arbitrary",), vmem_limit_bytes=VMEM_LIMIT),
        name="dispatch",
    )(cnt, e0, e1, r0, r1, x1)


def _expert_kernel(ord_ref, oe_ref, nums_ref, xs_ref, w1_hbm, w3_hbm, w2_hbm, ys_ref,
                   w1f_ref, w3f_ref, w2f_ref, w1b_ref, w3b_ref, w2b_ref, sem):
    i = pl.program_id(0)
    n_used = nums_ref[0]
    n_ord = nums_ref[1]
    active = i < n_used
    k = ord_ref[i]
    first = (i == 0) | (k != ord_ref[jnp.maximum(i - 1, 0)])

    def weight_copies(kk):
        e = oe_ref[kk]
        slot = kk % W_SLOTS
        return [pltpu.make_async_copy(w_hbm.at[e], wf_ref.at[slot], sem.at[slot])
                for w_hbm, wf_ref in ((w1_hbm, w1f_ref), (w3_hbm, w3f_ref), (w2_hbm, w2f_ref))]

    def start_weights(kk):
        @pl.when(kk < n_ord)
        def _():
            for cp in weight_copies(kk):
                cp.start(priority=1)

    @pl.when(active & first)
    def _():
        @pl.when(i == 0)
        def _():
            for ahead in range(W_SLOTS - 1):
                start_weights(k + ahead)

        start_weights(k + W_SLOTS - 1)
        for cp in weight_copies(k):
            cp.wait()
        slot = k % W_SLOTS
        w1b_ref[...] = w1f_ref[slot].astype(BF16)
        w3b_ref[...] = w3f_ref[slot].astype(BF16)
        w2b_ref[...] = w2f_ref[slot].astype(BF16)

    @pl.when(active)
    def _():
        xb = _from_token_tiles(xs_ref, BM).astype(BF16)
        h1 = jnp.dot(xb, w1b_ref[...], preferred_element_type=F32)
        h3 = jnp.dot(xb, w3b_ref[...], preferred_element_type=F32)
        hdn = (h1 * _sigmoid(h1) * h3).astype(BF16)
        _to_token_tiles(ys_ref, jnp.dot(hdn, w2b_ref[...], preferred_element_type=F32))

    @pl.when(jnp.logical_not(active))
    def _():
        ys_ref[...] = jnp.zeros_like(ys_ref)


def _experts(blk_ord, ord_e, nums, xs, w1, w3, w2):
    last = lambda i, nums_ref: jnp.maximum(jnp.minimum(i, nums_ref[0] - 1), 0)
    hbm = pl.BlockSpec(memory_space=pl.ANY)
    grid_spec = pltpu.PrefetchScalarGridSpec(
        num_scalar_prefetch=3,
        grid=(N_BLOCKS,),
        in_specs=[
            pl.BlockSpec((BM * TOKEN_ROWS, LANE), lambda i, o, oe, nu: (last(i, nu), 0)),
            hbm, hbm, hbm,
        ],
        out_specs=pl.BlockSpec((BM * TOKEN_ROWS, LANE), lambda i, o, oe, nu: (i, 0)),
        scratch_shapes=[
            pltpu.VMEM((W_SLOTS, D_MODEL, D_EXPERT), F32),
            pltpu.VMEM((W_SLOTS, D_MODEL, D_EXPERT), F32),
            pltpu.VMEM((W_SLOTS, D_EXPERT, D_MODEL), F32),
            pltpu.VMEM((D_MODEL, D_EXPERT), BF16),
            pltpu.VMEM((D_MODEL, D_EXPERT), BF16),
            pltpu.VMEM((D_EXPERT, D_MODEL), BF16),
            pltpu.SemaphoreType.DMA((W_SLOTS,)),
        ],
    )
    return pl.pallas_call(
        _expert_kernel,
        grid_spec=grid_spec,
        out_shape=jax.ShapeDtypeStruct((N_SLOTS * TOKEN_ROWS, LANE), F32),
        compiler_params=pltpu.CompilerParams(
            dimension_semantics=("arbitrary",), vmem_limit_bytes=VMEM_LIMIT),
        name="experts",
    )(blk_ord, ord_e, nums, xs, w1, w3, w2)


def _combine_kernel(d0_ref, d1_ref, route_ref, x1_ref, ys_ref, g_ref, b_ref, o_ref, y_ref, sem):
    i = pl.program_id(0)
    n_tiles = pl.num_programs(0)
    tile_rows = TC_COMB * TOKEN_ROWS

    def issue(tile, slot, t0, n):
        drow = (tile * TC_COMB + t0) >> LANE_BITS
        for uu in range(n):
            t = t0 + uu
            dst = pl.ds(pl.multiple_of(t * TOKEN_ROWS, TOKEN_ROWS), TOKEN_ROWS)
            for kk, d_ref in enumerate((d0_ref, d1_ref)):
                d = d_ref[drow + uu // LANE, uu % LANE]
                src = pl.ds(pl.multiple_of(d * TOKEN_ROWS, TOKEN_ROWS), TOKEN_ROWS)
                pltpu.make_async_copy(ys_ref.at[src], y_ref.at[slot, kk, dst],
                                      sem.at[slot]).start(priority=kk)

    def finish(slot, t0):
        t0 = pl.multiple_of(t0, COMB_ROWS)
        w0 = route_ref[pl.ds(t0, COMB_ROWS), 4:5]
        w1 = route_ref[pl.ds(t0, COMB_ROWS), 5:6]
        z = (DEEPNORM_ALPHA * _from_token_tiles(x1_ref, COMB_ROWS, t0)
             + w0 * _from_token_tiles(y_ref.at[slot, 0], COMB_ROWS, t0)
             + w1 * _from_token_tiles(y_ref.at[slot, 1], COMB_ROWS, t0))
        o_ref[pl.ds(t0, COMB_ROWS), :] = _layer_norm(z, g_ref[...], b_ref[...])

    @pl.when(i == 0)
    def _():
        def first(gi, carry):
            issue(0, 0, gi * COMB_ROWS, COMB_ROWS)
            return carry

        lax.fori_loop(0, TC_COMB // COMB_ROWS, first, 0)

    slot = i % 2
    for kk in range(TOP_K):
        pltpu.make_async_copy(ys_ref.at[pl.ds(0, tile_rows)], y_ref.at[slot, kk],
                              sem.at[slot]).wait()

    @pl.when(i + 1 < n_tiles)
    def _():
        def both(gi, carry):
            issue(i + 1, 1 - slot, gi * COMB_ROWS, COMB_ROWS)
            finish(slot, gi * COMB_ROWS)
            return carry

        lax.fori_loop(0, TC_COMB // COMB_ROWS, both, 0)

    @pl.when(i + 1 == n_tiles)
    def _():
        def last(gi, carry):
            finish(slot, gi * COMB_ROWS)
            return carry

        lax.fori_loop(0, TC_COMB // COMB_ROWS, last, 0)


def _combine(d0, d1, route, x1, ys, g, b):
    tc = TC_COMB
    grid_spec = pltpu.PrefetchScalarGridSpec(
        num_scalar_prefetch=2,
        grid=(N_TOK // tc,),
        in_specs=[
            pl.BlockSpec((tc, LANE), lambda i, *_: (i, 0)),
            pl.BlockSpec((tc * TOKEN_ROWS, LANE), lambda i, *_: (i, 0)),
            pl.BlockSpec(memory_space=pl.ANY),
            pl.BlockSpec((1, D_MODEL), lambda i, *_: (0, 0)),
            pl.BlockSpec((1, D_MODEL), lambda i, *_: (0, 0)),
        ],
        out_specs=pl.BlockSpec((tc, D_MODEL), lambda i, *_: (i, 0)),
        scratch_shapes=[
            pltpu.VMEM((2, TOP_K, tc * TOKEN_ROWS, LANE), F32),
            pltpu.SemaphoreType.DMA((2,)),
        ],
    )
    return pl.pallas_call(
        _combine_kernel,
        grid_spec=grid_spec,
        out_shape=jax.ShapeDtypeStruct((N_TOK, D_MODEL), F32),
        compiler_params=pltpu.CompilerParams(
            dimension_semantics=("arbitrary",), vmem_limit_bytes=VMEM_LIMIT),
        name="combine",
    )(d0, d1, route, x1, ys, g, b)


def _split_bf16(w):
    hi = w.astype(BF16)
    lo = (w - hi.astype(F32)).astype(BF16)
    return hi, lo


def kernel(x, w_in, b_in, conv_w, conv_b, conv_ln_g, conv_ln_b, w_conv_out, b_conv_out, w_gate_up, b_gate_up, gla_norm_g, w_gla_out, w_out, b_out, ln1_g, ln1_b, w_router_group, b_router_group, w_router_expert, b_router_expert, w1, w3, w2, ln2_g, ln2_b):
    x2 = x.reshape(N_TOK, D_MODEL)
    row = lambda v: v.reshape(1, -1)
    for l in range(w_in.shape[0]):
        f0 = N_PROJ_A
        f1 = f0 + GATE_RANK
        w_l, b_l = w_in[l], b_in[l]
        w_a = w_l[:, :f0].astype(BF16)
        w_b = w_l[:, f1:].astype(BF16)
        w_f = jnp.pad(w_l[:, f0:f1], ((0, 0), (0, F_PAD - GATE_RANK))).astype(BF16)
        b_p = row(jnp.concatenate(
            [b_l[:f0], b_l[f1:], b_l[f0:f1], jnp.zeros((F_PAD - GATE_RANK,), F32)]))
        wgu_p = jnp.concatenate(
            [w_gate_up[l], jnp.zeros((F_PAD - GATE_RANK, GLA_DK), F32)], axis=0).astype(BF16)
        cw_p = jnp.repeat(conv_w[l], SUBLANES, axis=0)
        c, q, k, v, rs, sga, sgb, la = _inproj(
            x2, w_a, w_b, w_f, b_p, wgu_p, row(b_gate_up[l]), cw_p,
            row(conv_b[l]), row(conv_ln_g[l]), row(conv_ln_b[l]))
        g = _gla(q, k, v, la, rs, row(gla_norm_g[l]))

        w_r = jnp.concatenate(
            [w_router_expert[l], w_router_group[l],
             jnp.zeros((D_MODEL, LANE - N_EXPERTS - N_GROUPS), F32)], axis=1)
        b_r = row(jnp.concatenate(
            [b_router_expert[l], b_router_group[l],
             jnp.zeros((LANE - N_EXPERTS - N_GROUPS,), F32)]))
        wrh, wrl = _split_bf16(w_r)
        ri = lax.broadcasted_iota(jnp.int32, (TM_SUB, TM_SUB), 0)
        ci = lax.broadcasted_iota(jnp.int32, (TM_SUB, TM_SUB), 1)
        tri = (ri > ci).astype(BF16)
        x1, route, meta, cnt = _mix(
            c, g, sga, sgb, x2, w_conv_out[l].astype(BF16), row(b_conv_out[l]),
            w_gla_out[l].astype(BF16), w_out[l].astype(BF16), row(b_out[l]),
            row(ln1_g[l]), row(ln1_b[l]), wrh, wrl, b_r, tri)

        cnt_i = cnt[0, :N_EXPERTS].astype(jnp.int32)
        tok_grid = lambda m: m.reshape(N_TOK // LANE, LANE)
        xs, d0, d1, blk_ord, ord_e, nums = _dispatch(
            tok_grid(meta[0]), tok_grid(meta[1]), tok_grid(meta[2]), tok_grid(meta[3]), cnt_i, x1)
        ys = _experts(blk_ord, ord_e, nums, xs, w1[l], w3[l], w2[l])
        x2 = _combine(d0, d1, route, x1, ys, row(ln2_g[l]), row(ln2_b[l]))
    return x2.reshape(x.shape)
```

```python
import jax
import jax.numpy as jnp
from jax import lax
from jax.experimental import pallas as pl
from jax.experimental.pallas import tpu as pltpu

F32 = jnp.float32
BF16 = jnp.bfloat16

D_MODEL = 1024
BATCH = 8
SEQ = 2048
N_TOK = BATCH * SEQ
CHUNK = 64
CONV_WIDTH = 31
GLA_HEADS = 4
GLA_DK = 512
GLA_DV = 1024
GLA_HK = 128
GLA_HV = 256
GATE_RANK = 16
GATE_TAU = 16.0
N_GROUPS = 8
EXPERTS_PER_GROUP = 8
N_EXPERTS = 64
TOP_K = 2
D_EXPERT = 512
LN_EPS = 1e-5
RMS_EPS = 1e-6
DEEPNORM_ALPHA = 2.0 ** 0.25

LANE = 128
LANE_BITS = LANE.bit_length() - 1
TOKEN_ROWS = D_MODEL // LANE
F_PAD = LANE
N_PROJ_A = 2 * D_MODEL + 2 * GLA_DK + 2 * GLA_DV
N_PROJ = N_PROJ_A + 2 * D_MODEL + F_PAD
TM_PROJ = 512
PROJ_PIECE = 256
CONV_HALO = 32
CONV_ROWS = 32
SUBLANES = 8
CONV_SUB = 128
CONV_SHROWS = CONV_SUB + CONV_HALO
CONV_WIN = CONV_SHROWS + 16
TS_GLA = 256
NB_GLA = 4
TM_MIX = 1024
TM_SUB = 256
META_ROWS = 8
BM = 256
W_SLOTS = 3
N_BLOCKS = N_TOK * TOP_K // BM + N_EXPERTS
N_SLOTS = N_BLOCKS * BM
TC_COMB = 512
COMB_ROWS = 128
TD_DISP = 2048
DISP_UNROLL = LANE
VMEM_LIMIT = 56 * 1024 * 1024


def _sigmoid(x):
    return 1.0 / (1.0 + jnp.exp(-x))


def _to_token_tiles(ref, val, t0=0):
    n = val.shape[0]
    for c in range(TOKEN_ROWS):
        ref[pl.ds(t0 * TOKEN_ROWS + c, n, stride=TOKEN_ROWS), :] = val[:, c * LANE:(c + 1) * LANE]


def _from_token_tiles(ref, n, t0=0):
    return jnp.concatenate(
        [ref[pl.ds(t0 * TOKEN_ROWS + c, n, stride=TOKEN_ROWS), :] for c in range(TOKEN_ROWS)],
        axis=1)


def _layer_norm(z, g, b):
    mu = jnp.mean(z, axis=-1, keepdims=True)
    zc = z - mu
    var = jnp.mean(zc * zc, axis=-1, keepdims=True)
    return zc * lax.rsqrt(var + LN_EPS) * g + b


def _conv_shift(ext_ref, shf_ref, base):
    rows = CONV_WIN
    win = ext_ref[base:base + rows, :].astype(F32)
    for b in range(SUBLANES):
        sh = win if b == 0 else pltpu.roll(win, rows - b, axis=0)
        shf_ref[b * CONV_SHROWS:(b + 1) * CONV_SHROWS, :] = sh[0:CONV_SHROWS, :]


def _conv_rows(shf_ref, cw_ref, cb_ref, g_ref, b_ref, c_ref, base, r0):
    off = CONV_HALO - (CONV_WIDTH - 1)
    n_sub = CONV_ROWS // SUBLANES
    acc = [jnp.zeros((SUBLANES, D_MODEL), F32) for _ in range(n_sub)]
    for j in range(CONV_WIDTH):
        l0 = r0 + off + j
        b = l0 % SUBLANES
        m0 = b * CONV_SHROWS + (l0 - b)
        wj = cw_ref[j * SUBLANES:(j + 1) * SUBLANES, :]
        for k in range(n_sub):
            rk = m0 + k * SUBLANES
            acc[k] = acc[k] + shf_ref[rk:rk + SUBLANES, :] * wj
    y = _layer_norm(jnp.concatenate(acc, axis=0) + cb_ref[...], g_ref[...], b_ref[...])
    c_ref[base + r0:base + r0 + CONV_ROWS, :] = (y * _sigmoid(y)).astype(BF16)


def _inproj_kernel(x_ref, wa_ref, wb_ref, wf_ref, b_ref, wgu_ref, bgu_ref,
                   cw_ref, cb_ref, cg_ref, cbeta_ref,
                   c_ref, q_ref, k_ref, v_ref, rs_ref, sga_ref, sgb_ref, la_ref,
                   ext_ref, shf_ref, xb_ref):
    i = pl.program_id(0)
    xb_ref[...] = x_ref[...].astype(BF16)
    half = D_MODEL // 2

    def seg(c0, n):
        if c0 < N_PROJ_A:
            w = wa_ref[:, c0:c0 + n]
        elif c0 < N_PROJ_A + 2 * D_MODEL:
            w = wb_ref[:, c0 - N_PROJ_A:c0 - N_PROJ_A + n]
        else:
            w = wf_ref[...]
        return jnp.dot(xb_ref[...], w, preferred_element_type=F32) + b_ref[:, c0:c0 + n]

    @pl.when(i == 0)
    def _():
        ext_ref[...] = jnp.zeros_like(ext_ref)

    prev_tail = ext_ref[TM_PROJ:TM_PROJ + CONV_HALO, :]
    ext_ref[0:CONV_HALO, :] = jnp.where(
        i % (SEQ // TM_PROJ) == 0, jnp.zeros_like(prev_tail), prev_tail)
    for j in range(2):
        a = seg(j * half, half)
        g = seg(D_MODEL + j * half, half)
        ext_ref[CONV_HALO:CONV_HALO + TM_PROJ, j * half:(j + 1) * half] = (
            a * _sigmoid(g)).astype(BF16)

    def piece(ref, col, c0, fn):
        def run():
            ref[:, col:col + PROJ_PIECE] = fn(seg(c0 + col, PROJ_PIECE)).astype(BF16)
        return run

    mxu_pieces = []
    for ref, c0, width, fn in (
            (q_ref, 2 * D_MODEL, GLA_DK, lambda h: h * (GLA_HK ** -0.5)),
            (k_ref, 2 * D_MODEL + GLA_DK, GLA_DK, lambda h: h),
            (v_ref, 3 * D_MODEL, GLA_DV, lambda h: h),
            (rs_ref, 4 * D_MODEL, GLA_DV, lambda h: h * _sigmoid(h)),
            (sga_ref, 5 * D_MODEL, D_MODEL, _sigmoid),
            (sgb_ref, 6 * D_MODEL, D_MODEL, _sigmoid)):
        mxu_pieces += [piece(ref, col, c0, fn) for col in range(0, width, PROJ_PIECE)]

    def forget_piece():
        f = seg(7 * D_MODEL, F_PAD)
        z = jnp.dot(f.astype(BF16), wgu_ref[...], preferred_element_type=F32) + bgu_ref[...]
        la_ref[...] = ((jnp.minimum(z, 0.0) - jnp.log(1.0 + jnp.exp(-jnp.abs(z))))
                       * (1.0 / GATE_TAU))

    mxu_pieces.append(forget_piece)

    valu_pieces = []
    for sb in range(TM_PROJ // CONV_SUB):
        shf = shf_ref.at[sb % 2]
        for r0 in range(0, CONV_SUB, CONV_ROWS):
            def run(sb=sb, shf=shf, r0=r0):
                if r0 == 0:
                    _conv_shift(ext_ref, shf, sb * CONV_SUB)
                _conv_rows(shf, cw_ref, cb_ref, cg_ref, cbeta_ref, c_ref, sb * CONV_SUB, r0)
            valu_pieces.append(run)

    n_v, n_m = len(valu_pieces), len(mxu_pieces)
    done = 0
    for vi, vrun in enumerate(valu_pieces):
        vrun()
        upto = (vi + 1) * n_m // n_v
        for mrun in mxu_pieces[done:upto]:
            mrun()
        done = upto


def _inproj(x2, w_a, w_b, w_f, b_p, wgu_p, bgu, cw_p, cb, cg, cbeta):
    tm = TM_PROJ
    row = lambda i: (i, 0)
    fixed = lambda i: (0, 0)
    tok = lambda n, dt: jax.ShapeDtypeStruct((N_TOK, n), dt)
    vec = pl.BlockSpec((1, D_MODEL), fixed)
    return pl.pallas_call(
        _inproj_kernel,
        grid=(N_TOK // tm,),
        in_specs=[
            pl.BlockSpec((tm, D_MODEL), row),
            pl.BlockSpec((D_MODEL, N_PROJ_A), fixed, pipeline_mode=pl.Buffered(1)),
            pl.BlockSpec((D_MODEL, 2 * D_MODEL), fixed, pipeline_mode=pl.Buffered(1)),
            pl.BlockSpec((D_MODEL, F_PAD), fixed),
            pl.BlockSpec((1, N_PROJ), fixed),
            pl.BlockSpec((F_PAD, GLA_DK), fixed),
            pl.BlockSpec((1, GLA_DK), fixed),
            pl.BlockSpec((CONV_WIDTH * SUBLANES, D_MODEL), fixed),
            vec, vec, vec,
        ],
        out_specs=[
            pl.BlockSpec((tm, D_MODEL), row),
            pl.BlockSpec((tm, GLA_DK), row),
            pl.BlockSpec((tm, GLA_DK), row),
            pl.BlockSpec((tm, GLA_DV), row),
            pl.BlockSpec((tm, GLA_DV), row),
            pl.BlockSpec((tm, D_MODEL), row),
            pl.BlockSpec((tm, D_MODEL), row),
            pl.BlockSpec((tm, GLA_DK), row),
        ],
        out_shape=[
            tok(D_MODEL, BF16), tok(GLA_DK, BF16), tok(GLA_DK, BF16), tok(GLA_DV, BF16),
            tok(GLA_DV, BF16), tok(D_MODEL, BF16), tok(D_MODEL, BF16), tok(GLA_DK, F32),
        ],
        scratch_shapes=[
            pltpu.VMEM((tm + CONV_WIN - CONV_SUB, D_MODEL), BF16),
            pltpu.VMEM((2, SUBLANES * CONV_SHROWS, D_MODEL), F32),
            pltpu.VMEM((tm, D_MODEL), BF16),
        ],
        compiler_params=pltpu.CompilerParams(
            dimension_semantics=("arbitrary",), vmem_limit_bytes=VMEM_LIMIT),
        name="inproj",
    )(x2, w_a, w_b, w_f, b_p, wgu_p, bgu, cw_p, cb, cg, cbeta)


def _gla_kernel(q_ref, k_ref, v_ref, la_ref, rs_ref, ng_ref, o_ref,
                st_ref, kv_ref, sb_ref, dec_ref):
    s = pl.program_id(1)

    @pl.when(s == 0)
    def _():
        st_ref[...] = jnp.zeros_like(st_ref)

    rr = lax.broadcasted_iota(jnp.int32, (CHUNK, CHUNK), 0)
    cc = lax.broadcasted_iota(jnp.int32, (CHUNK, CHUNK), 1)
    tri = jnp.where(rr >= cc, 1.0, 0.0).astype(BF16)
    n_chunks = TS_GLA // CHUNK
    heads = [(slice(h * GLA_HK, (h + 1) * GLA_HK), slice(h * GLA_HV, (h + 1) * GLA_HV))
             for h in range(GLA_HEADS)]

    for nb in range(NB_GLA):
        for c in range(n_chunks):
            rows = slice(c * CHUNK, (c + 1) * CHUNK)
            la = la_ref[nb, rows, :]
            hi = la.astype(BF16)
            lo = (la - hi.astype(F32)).astype(BF16)
            cum = (jnp.dot(tri, hi, preferred_element_type=F32)
                   + jnp.dot(tri, lo, preferred_element_type=F32))
            cend = cum[CHUNK - 1:CHUNK, :]
            kd = (k_ref[nb, rows, :].astype(F32) * jnp.exp(cend - cum)).astype(BF16)
            idx = nb * n_chunks + c
            dec_ref[idx * SUBLANES:(idx + 1) * SUBLANES, :] = jnp.broadcast_to(
                jnp.exp(cend), (SUBLANES, GLA_DK))
            vc = v_ref[nb, rows, :]
            for h, (ks, vs) in enumerate(heads):
                kv_ref[idx * GLA_HEADS + h] = lax.dot_general(
                    vc[:, vs], kd[:, ks], (((0,), (0,)), ((), ())), preferred_element_type=F32)

    for nb in range(NB_GLA):
        for h, (ks, vs) in enumerate(heads):
            st = st_ref[nb * GLA_HEADS + h]
            for c in range(n_chunks):
                idx = nb * n_chunks + c
                st = st * dec_ref[idx * SUBLANES:idx * SUBLANES + 1, ks] + kv_ref[
                    idx * GLA_HEADS + h]
                sb_ref[idx * GLA_HEADS + h] = st.astype(BF16)
            st_ref[nb * GLA_HEADS + h] = st

    for nb in range(NB_GLA):
        for c in range(n_chunks):
            rows = slice(c * CHUNK, (c + 1) * CHUNK)
            idx = nb * n_chunks + c
            qc = q_ref[nb, rows, :]
            for h, (ks, vs) in enumerate(heads):
                o = lax.dot_general(qc[:, ks], sb_ref[idx * GLA_HEADS + h],
                                    (((1,), (1,)), ((), ())), preferred_element_type=F32)
                ms = jnp.mean(o * o, axis=-1, keepdims=True)
                on = o * lax.rsqrt(ms + RMS_EPS) * ng_ref[:, vs]
                o_ref[nb, rows, vs] = (on * rs_ref[nb, rows, vs].astype(F32)).astype(BF16)


def _gla(q, k, v, la, rs, ng):
    ts = TS_GLA
    blk = lambda n: pl.BlockSpec((NB_GLA, ts, n), lambda bi, si: (bi, si, 0))
    seq = lambda a: a.reshape(BATCH, SEQ, a.shape[-1])
    out = pl.pallas_call(
        _gla_kernel,
        grid=(BATCH // NB_GLA, SEQ // ts),
        in_specs=[blk(GLA_DK), blk(GLA_DK), blk(GLA_DV), blk(GLA_DK), blk(GLA_DV),
                  pl.BlockSpec((1, GLA_DV), lambda bi, si: (0, 0))],
        out_specs=blk(GLA_DV),
        out_shape=jax.ShapeDtypeStruct((BATCH, SEQ, GLA_DV), BF16),
        scratch_shapes=[
            pltpu.VMEM((NB_GLA * GLA_HEADS, GLA_HV, GLA_HK), F32),
            pltpu.VMEM((NB_GLA * (ts // CHUNK) * GLA_HEADS, GLA_HV, GLA_HK), F32),
            pltpu.VMEM((NB_GLA * (ts // CHUNK) * GLA_HEADS, GLA_HV, GLA_HK), BF16),
            pltpu.VMEM((NB_GLA * (ts // CHUNK) * SUBLANES, GLA_DK), F32),
        ],
        compiler_params=pltpu.CompilerParams(
            dimension_semantics=("arbitrary", "arbitrary"), vmem_limit_bytes=VMEM_LIMIT),
        name="gla",
    )(seq(q), seq(k), seq(v), seq(la), seq(rs), ng)
    return out.reshape(N_TOK, GLA_DV)


def _mix_kernel(c_ref, g_ref, sga_ref, sgb_ref, x_ref, wco_ref, bco_ref, wgl_ref, wo_ref, bo_ref,
                l1g_ref, l1b_ref, wrh_ref, wrl_ref, br_ref, tri_ref,
                x1_ref, route_ref, meta_ref, cnt_ref, carry_ref):
    i = pl.program_id(0)

    @pl.when(i == 0)
    def _():
        carry_ref[...] = jnp.zeros_like(carry_ref)

    def rows_of(sub):
        return slice(sub * TM_SUB, (sub + 1) * TM_SUB)

    def branch_projections(sub):
        rows = rows_of(sub)
        yc = jnp.dot(c_ref[rows, :], wco_ref[...], preferred_element_type=F32) + bco_ref[...]
        yg = jnp.dot(g_ref[rows, :], wgl_ref[...], preferred_element_type=F32)
        return yc, yg

    def out_projection(sub, yc, yg):
        rows = rows_of(sub)
        merged = sga_ref[rows, :].astype(F32) * yc + sgb_ref[rows, :].astype(F32) * yg
        return jnp.dot(merged.astype(BF16), wo_ref[...], preferred_element_type=F32) + bo_ref[...]

    def norm_and_logits(sub, mix):
        x1 = _layer_norm(DEEPNORM_ALPHA * x_ref[rows_of(sub), :] + mix, l1g_ref[...], l1b_ref[...])
        _to_token_tiles(x1_ref, x1, sub * TM_SUB)
        xh = x1.astype(BF16)
        xl = (x1 - xh.astype(F32)).astype(BF16)
        return (jnp.dot(xh, wrh_ref[...], preferred_element_type=F32)
                + jnp.dot(xl, wrh_ref[...], preferred_element_type=F32)
                + jnp.dot(xh, wrl_ref[...], preferred_element_type=F32)) + br_ref[...]

    def route(sub, lg):
        rows = rows_of(sub)
        lane = lax.broadcasted_iota(jnp.int32, (TM_SUB, LANE), 1)
        neg = -jnp.inf
        big = jnp.int32(1 << 20)
        is_g = (lane >= N_EXPERTS) & (lane < N_EXPERTS + N_GROUPS)
        gl = jnp.where(is_g, lg, neg)
        gmax = jnp.max(gl, axis=-1, keepdims=True)
        gsel = jnp.min(jnp.where(gl == gmax, lane, big), axis=-1, keepdims=True) - N_EXPERTS
        gw = 1.0 / jnp.sum(jnp.exp(gl - gmax), axis=-1, keepdims=True)
        in_grp = (lane < N_EXPERTS) & ((lane >> 3) == gsel)
        el = jnp.where(in_grp, lg, neg)
        m1 = jnp.max(el, axis=-1, keepdims=True)
        i1 = jnp.min(jnp.where(el == m1, lane, big), axis=-1, keepdims=True)
        el2 = jnp.where(lane == i1, neg, el)
        m2 = jnp.max(el2, axis=-1, keepdims=True)
        i2 = jnp.min(jnp.where(el2 == m2, lane, big), axis=-1, keepdims=True)
        t = jnp.exp(m2 - m1)
        w1 = gw / (1.0 + t)
        w2 = gw * t / (1.0 + t)

        oh1 = lane == i1
        oh2 = lane == i2
        osum = jnp.where(oh1 | oh2, 1.0, 0.0)
        excl = (jnp.dot(tri_ref[...], osum.astype(BF16), preferred_element_type=F32)
                + carry_ref[...])
        rank1 = jnp.sum(jnp.where(oh1, excl, 0.0), axis=-1, keepdims=True)
        rank2 = jnp.sum(jnp.where(oh2, excl, 0.0), axis=-1, keepdims=True)
        carry_ref[...] = carry_ref[...] + jnp.sum(osum, axis=0, keepdims=True)

        out = jnp.where(lane == 0, i1.astype(F32), 0.0)
        out = jnp.where(lane == 1, i2.astype(F32), out)
        out = jnp.where(lane == 2, rank1, out)
        out = jnp.where(lane == 3, rank2, out)
        out = jnp.where(lane == 4, w1, out)
        out = jnp.where(lane == 5, w2, out)
        route_ref[rows, :] = out
        meta_ref[:, rows] = out.T[0:META_ROWS, :].astype(jnp.int32)

    subs = range(TM_MIX // TM_SUB)
    branches = [branch_projections(sub) for sub in subs]
    mixes = [out_projection(sub, *branches[sub]) for sub in subs]
    logits = [norm_and_logits(sub, mixes[sub]) for sub in subs]
    for sub in subs:
        route(sub, logits[sub])
    cnt_ref[...] = carry_ref[...]


def _mix(c, g, sga, sgb, x2, wco, bco, wgl, wo, bo, l1g, l1b, wrh, wrl, br, tri):
    tm = TM_MIX
    row = lambda i: (i, 0)
    fixed = lambda i: (0, 0)
    mat = pl.BlockSpec((D_MODEL, D_MODEL), fixed, pipeline_mode=pl.Buffered(1))
    vec = pl.BlockSpec((1, D_MODEL), fixed)
    return pl.pallas_call(
        _mix_kernel,
        grid=(N_TOK // tm,),
        in_specs=[
            pl.BlockSpec((tm, D_MODEL), row), pl.BlockSpec((tm, D_MODEL), row),
            pl.BlockSpec((tm, D_MODEL), row), pl.BlockSpec((tm, D_MODEL), row),
            pl.BlockSpec((tm, D_MODEL), row),
            mat, vec, mat, mat, vec, vec, vec,
            pl.BlockSpec((D_MODEL, LANE), fixed), pl.BlockSpec((D_MODEL, LANE), fixed),
            pl.BlockSpec((1, LANE), fixed),
            pl.BlockSpec((TM_SUB, TM_SUB), fixed),
        ],
        out_specs=[
            pl.BlockSpec((tm * TOKEN_ROWS, LANE), row),
            pl.BlockSpec((tm, LANE), row),
            pl.BlockSpec((META_ROWS, tm), lambda i: (0, i)),
            pl.BlockSpec((1, LANE), fixed),
        ],
        out_shape=[
            jax.ShapeDtypeStruct((N_TOK * TOKEN_ROWS, LANE), F32),
            jax.ShapeDtypeStruct((N_TOK, LANE), F32),
            jax.ShapeDtypeStruct((META_ROWS, N_TOK), jnp.int32),
            jax.ShapeDtypeStruct((1, LANE), F32),
        ],
        scratch_shapes=[pltpu.VMEM((1, LANE), F32)],
        compiler_params=pltpu.CompilerParams(
            dimension_semantics=("arbitrary",), vmem_limit_bytes=VMEM_LIMIT),
        name="mix",
    )(c, g, sga, sgb, x2, wco, bco, wgl, wo, bo, l1g, l1b, wrh, wrl, br, tri)


def _dispatch_kernel(cnt_ref, e0_ref, e1_ref, r0_ref, r1_ref, x1_ref,
                     xs_ref, d0_ref, d1_ref, ord_ref, oe_ref, nums_ref,
                     pstart_ref, zero_ref, dvm_ref, sem, zsem, dsem):
    i = pl.program_id(0)

    def zero_copy(b):
        rows = BM * TOKEN_ROWS
        return pltpu.make_async_copy(zero_ref, xs_ref.at[pl.ds(b * rows, rows)], zsem)

    @pl.when(i == 0)
    def _():
        zero_ref[...] = jnp.zeros_like(zero_ref)

        def plan(e, carry):
            bstart, n_ord = carry
            nb = (cnt_ref[e] + (BM - 1)) // BM
            pstart_ref[e] = bstart * BM

            def fill(j, c):
                ord_ref[bstart + j] = n_ord
                return c

            lax.fori_loop(0, nb, fill, 0)

            @pl.when(nb > 0)
            def _():
                oe_ref[n_ord] = e
                zero_copy(bstart + nb - 1).start()

            return bstart + nb, n_ord + jnp.where(nb > 0, 1, 0)

        n_used, n_ord = lax.fori_loop(0, N_EXPERTS, plan, (jnp.int32(0), jnp.int32(0)))
        nums_ref[0] = n_used
        nums_ref[1] = n_ord

        def fill_rest(j, c):
            ord_ref[j] = n_ord - 1
            zero_copy(j).start()
            return c

        lax.fori_loop(n_used, N_BLOCKS, fill_rest, 0)

        def fill_oe(j, c):
            oe_ref[j] = 0
            return c

        lax.fori_loop(n_ord, N_EXPERTS, fill_oe, 0)

        for e_ref, r_ref, d_ref, k in ((e0_ref, r0_ref, d0_ref, 0), (e1_ref, r1_ref, d1_ref, 1)):
            ev = e_ref[...]
            dv = r_ref[...]
            for e in range(N_EXPERTS):
                dv = dv + jnp.where(ev == e, pstart_ref[e], 0)
            dvm_ref[k] = dv
            pltpu.make_async_copy(dvm_ref.at[k], d_ref, dsem).start()
        for d_ref, k in ((d0_ref, 0), (d1_ref, 1)):
            pltpu.make_async_copy(dvm_ref.at[k], d_ref, dsem).wait()

        def drain(j, c):
            zero_copy(0).wait()
            return c

        lax.fori_loop(0, n_ord + (N_BLOCKS - n_used), drain, 0)

    base = i * TD_DISP

    def tok_group(tg, carry):
        drow = (base >> LANE_BITS) + tg
        for uu in range(DISP_UNROLL):
            tt = tg * DISP_UNROLL + uu
            src = x1_ref.at[pl.ds(pl.multiple_of(tt * TOKEN_ROWS, TOKEN_ROWS), TOKEN_ROWS)]
            for kk, d_ref in enumerate((d0_ref, d1_ref)):
                d = d_ref[drow, uu]
                pltpu.make_async_copy(
                    src, xs_ref.at[pl.ds(pl.multiple_of(d * TOKEN_ROWS, TOKEN_ROWS), TOKEN_ROWS)],
                    sem).start(priority=kk)
        return carry

    lax.fori_loop(0, TD_DISP // DISP_UNROLL, tok_group, 0)
    for kk in range(TOP_K):
        pltpu.make_async_copy(x1_ref, xs_ref.at[pl.ds(0, TD_DISP * TOKEN_ROWS)], sem).wait()


def _dispatch(e0, e1, r0, r1, cnt, x1):
    smem = pl.BlockSpec(memory_space=pltpu.SMEM)
    idx = pl.BlockSpec((N_TOK // LANE, LANE), lambda i, c: (0, 0))
    grid_spec = pltpu.PrefetchScalarGridSpec(
        num_scalar_prefetch=1,
        grid=(N_TOK // TD_DISP,),
        in_specs=[idx, idx, idx, idx,
                  pl.BlockSpec((TD_DISP * TOKEN_ROWS, LANE), lambda i, c: (i, 0))],
        out_specs=[pl.BlockSpec(memory_space=pl.ANY), smem, smem, smem, smem, smem],
        scratch_shapes=[
            pltpu.SMEM((N_EXPERTS,), jnp.int32),
            pltpu.VMEM((BM * TOKEN_ROWS, LANE), F32),
            pltpu.VMEM((TOP_K, N_TOK // LANE, LANE), jnp.int32),
            pltpu.SemaphoreType.DMA,
            pltpu.SemaphoreType.DMA,
            pltpu.SemaphoreType.DMA,
        ],
    )
    return pl.pallas_call(
        _dispatch_kernel,
        grid_spec=grid_spec,
        out_shape=[
            jax.ShapeDtypeStruct((N_SLOTS * TOKEN_ROWS, LANE), F32),
            jax.ShapeDtypeStruct((N_TOK // LANE, LANE), jnp.int32),
            jax.ShapeDtypeStruct((N_TOK // LANE, LANE), jnp.int32),
            jax.ShapeDtypeStruct((N_BLOCKS,), jnp.int32),
            jax.ShapeDtypeStruct((N_EXPERTS,), jnp.int32),
            jax.ShapeDtypeStruct((2,), jnp.int32),
        ],
        compiler_params=pltpu.CompilerParams(
            dimension_semantics=("arbitrary",), vmem_limit_bytes=VMEM_LIMIT),
        name="dispatch",
    )(cnt, e0, e1, r0, r1, x1)


def _expert_kernel(ord_ref, oe_ref, nums_ref, xs_ref, w1_hbm, w3_hbm, w2_hbm, ys_ref,
                   w1f_ref, w3f_ref, w2f_ref, w1b_ref, w3b_ref, w2b_ref, sem):
    i = pl.program_id(0)
    n_used = nums_ref[0]
    n_ord = nums_ref[1]
    active = i < n_used
    k = ord_ref[i]
    first = (i == 0) | (k != ord_ref[jnp.maximum(i - 1, 0)])

    def weight_copies(kk):
        e = oe_ref[kk]
        slot = kk % W_SLOTS
        return [pltpu.make_async_copy(w_hbm.at[e], wf_ref.at[slot], sem.at[slot])
                for w_hbm, wf_ref in ((w1_hbm, w1f_ref), (w3_hbm, w3f_ref), (w2_hbm, w2f_ref))]

    def start_weights(kk):
        @pl.when(kk < n_ord)
        def _():
            for cp in weight_copies(kk):
                cp.start(priority=1)

    @pl.when(active & first)
    def _():
        @pl.when(i == 0)
        def _():
            for ahead in range(W_SLOTS - 1):
                start_weights(k + ahead)

        start_weights(k + W_SLOTS - 1)
        for cp in weight_copies(k):
            cp.wait()
        slot = k % W_SLOTS
        w1b_ref[...] = w1f_ref[slot].astype(BF16)
        w3b_ref[...] = w3f_ref[slot].astype(BF16)
        w2b_ref[...] = w2f_ref[slot].astype(BF16)

    @pl.when(active)
    def _():
        xb = _from_token_tiles(xs_ref, BM).astype(BF16)
        h1 = jnp.dot(xb, w1b_ref[...], preferred_element_type=F32)
        h3 = jnp.dot(xb, w3b_ref[...], preferred_element_type=F32)
        hdn = (h1 * _sigmoid(h1) * h3).astype(BF16)
        _to_token_tiles(ys_ref, jnp.dot(hdn, w2b_ref[...], preferred_element_type=F32))

    @pl.when(jnp.logical_not(active))
    def _():
        ys_ref[...] = jnp.zeros_like(ys_ref)


def _experts(blk_ord, ord_e, nums, xs, w1, w3, w2):
    last = lambda i, nums_ref: jnp.maximum(jnp.minimum(i, nums_ref[0] - 1), 0)
    hbm = pl.BlockSpec(memory_space=pl.ANY)
    grid_spec = pltpu.PrefetchScalarGridSpec(
        num_scalar_prefetch=3,
        grid=(N_BLOCKS,),
        in_specs=[
            pl.BlockSpec((BM * TOKEN_ROWS, LANE), lambda i, o, oe, nu: (last(i, nu), 0)),
            hbm, hbm, hbm,
        ],
        out_specs=pl.BlockSpec((BM * TOKEN_ROWS, LANE), lambda i, o, oe, nu: (i, 0)),
        scratch_shapes=[
            pltpu.VMEM((W_SLOTS, D_MODEL, D_EXPERT), F32),
            pltpu.VMEM((W_SLOTS, D_MODEL, D_EXPERT), F32),
            pltpu.VMEM((W_SLOTS, D_EXPERT, D_MODEL), F32),
            pltpu.VMEM((D_MODEL, D_EXPERT), BF16),
            pltpu.VMEM((D_MODEL, D_EXPERT), BF16),
            pltpu.VMEM((D_EXPERT, D_MODEL), BF16),
            pltpu.SemaphoreType.DMA((W_SLOTS,)),
        ],
    )
    return pl.pallas_call(
        _expert_kernel,
        grid_spec=grid_spec,
        out_shape=jax.ShapeDtypeStruct((N_SLOTS * TOKEN_ROWS, LANE), F32),
        compiler_params=pltpu.CompilerParams(
            dimension_semantics=("arbitrary",), vmem_limit_bytes=VMEM_LIMIT),
        name="experts",
    )(blk_ord, ord_e, nums, xs, w1, w3, w2)


def _combine_kernel(d0_ref, d1_ref, route_ref, x1_ref, ys_ref, g_ref, b_ref, o_ref, y_ref, sem):
    i = pl.program_id(0)
    n_tiles = pl.num_programs(0)
    tile_rows = TC_COMB * TOKEN_ROWS

    def issue(tile, slot, t0, n):
        drow = (tile * TC_COMB + t0) >> LANE_BITS
        for uu in range(n):
            t = t0 + uu
            dst = pl.ds(pl.multiple_of(t * TOKEN_ROWS, TOKEN_ROWS), TOKEN_ROWS)
            for kk, d_ref in enumerate((d0_ref, d1_ref)):
                d = d_ref[drow + uu // LANE, uu % LANE]
                src = pl.ds(pl.multiple_of(d * TOKEN_ROWS, TOKEN_ROWS), TOKEN_ROWS)
                pltpu.make_async_copy(ys_ref.at[src], y_ref.at[slot, kk, dst],
                                      sem.at[slot]).start(priority=kk)

    def finish(slot, t0):
        t0 = pl.multiple_of(t0, COMB_ROWS)
        w0 = route_ref[pl.ds(t0, COMB_ROWS), 4:5]
        w1 = route_ref[pl.ds(t0, COMB_ROWS), 5:6]
        z = (DEEPNORM_ALPHA * _from_token_tiles(x1_ref, COMB_ROWS, t0)
             + w0 * _from_token_tiles(y_ref.at[slot, 0], COMB_ROWS, t0)
             + w1 * _from_token_tiles(y_ref.at[slot, 1], COMB_ROWS, t0))
        o_ref[pl.ds(t0, COMB_ROWS), :] = _layer_norm(z, g_ref[...], b_ref[...])

    @pl.when(i == 0)
    def _():
        def first(gi, carry):
            issue(0, 0, gi * COMB_ROWS, COMB_ROWS)
            return carry

        lax.fori_loop(0, TC_COMB // COMB_ROWS, first, 0)

    slot = i % 2
    for kk in range(TOP_K):
        pltpu.make_async_copy(ys_ref.at[pl.ds(0, tile_rows)], y_ref.at[slot, kk],
                              sem.at[slot]).wait()

    @pl.when(i + 1 < n_tiles)
    def _():
        def both(gi, carry):
            issue(i + 1, 1 - slot, gi * COMB_ROWS, COMB_ROWS)
            finish(slot, gi * COMB_ROWS)
            return carry

        lax.fori_loop(0, TC_COMB // COMB_ROWS, both, 0)

    @pl.when(i + 1 == n_tiles)
    def _():
        def last(gi, carry):
            finish(slot, gi * COMB_ROWS)
            return carry

        lax.fori_loop(0, TC_COMB // COMB_ROWS, last, 0)


def _combine(d0, d1, route, x1, ys, g, b):
    tc = TC_COMB
    grid_spec = pltpu.PrefetchScalarGridSpec(
        num_scalar_prefetch=2,
        grid=(N_TOK // tc,),
        in_specs=[
            pl.BlockSpec((tc, LANE), lambda i, *_: (i, 0)),
            pl.BlockSpec((tc * TOKEN_ROWS, LANE), lambda i, *_: (i, 0)),
            pl.BlockSpec(memory_space=pl.ANY),
            pl.BlockSpec((1, D_MODEL), lambda i, *_: (0, 0)),
            pl.BlockSpec((1, D_MODEL), lambda i, *_: (0, 0)),
        ],
        out_specs=pl.BlockSpec((tc, D_MODEL), lambda i, *_: (i, 0)),
        scratch_shapes=[
            pltpu.VMEM((2, TOP_K, tc * TOKEN_ROWS, LANE), F32),
            pltpu.SemaphoreType.DMA((2,)),
        ],
    )
    return pl.pallas_call(
        _combine_kernel,
        grid_spec=grid_spec,
        out_shape=jax.ShapeDtypeStruct((N_TOK, D_MODEL), F32),
        compiler_params=pltpu.CompilerParams(
            dimension_semantics=("arbitrary",), vmem_limit_bytes=VMEM_LIMIT),
        name="combine",
    )(d0, d1, route, x1, ys, g, b)


def _split_bf16(w):
    hi = w.astype(BF16)
    lo = (w - hi.astype(F32)).astype(BF16)
    return hi, lo


def kernel(x, w_in, b_in, conv_w, conv_b, conv_ln_g, conv_ln_b, w_conv_out, b_conv_out, w_gate_up, b_gate_up, gla_norm_g, w_gla_out, w_out, b_out, ln1_g, ln1_b, w_router_group, b_router_group, w_router_expert, b_router_expert, w1, w3, w2, ln2_g, ln2_b):
    x2 = x.reshape(N_TOK, D_MODEL)
    row = lambda v: v.reshape(1, -1)
    for l in range(w_in.shape[0]):
        f0 = N_PROJ_A
        f1 = f0 + GATE_RANK
        w_l, b_l = w_in[l], b_in[l]
        w_a = w_l[:, :f0].astype(BF16)
        w_b = w_l[:, f1:].astype(BF16)
        w_f = jnp.pad(w_l[:, f0:f1], ((0, 0), (0, F_PAD - GATE_RANK))).astype(BF16)
        b_p = row(jnp.concatenate(
            [b_l[:f0], b_l[f1:], b_l[f0:f1], jnp.zeros((F_PAD - GATE_RANK,), F32)]))
        wgu_p = jnp.concatenate(
            [w_gate_up[l], jnp.zeros((F_PAD - GATE_RANK, GLA_DK), F32)], axis=0).astype(BF16)
        cw_p = jnp.repeat(conv_w[l], SUBLANES, axis=0)
        c, q, k, v, rs, sga, sgb, la = _inproj(
            x2, w_a, w_b, w_f, b_p, wgu_p, row(b_gate_up[l]), cw_p,
            row(conv_b[l]), row(conv_ln_g[l]), row(conv_ln_b[l]))
        g = _gla(q, k, v, la, rs, row(gla_norm_g[l]))

        w_r = jnp.concatenate(
            [w_router_expert[l], w_router_group[l],
             jnp.zeros((D_MODEL, LANE - N_EXPERTS - N_GROUPS), F32)], axis=1)
        b_r = row(jnp.concatenate(
            [b_router_expert[l], b_router_group[l],
             jnp.zeros((LANE - N_EXPERTS - N_GROUPS,), F32)]))
        wrh, wrl = _split_bf16(w_r)
        ri = lax.broadcasted_iota(jnp.int32, (TM_SUB, TM_SUB), 0)
        ci = lax.broadcasted_iota(jnp.int32, (TM_SUB, TM_SUB), 1)
        tri = (ri > ci).astype(BF16)
        x1, route, meta, cnt = _mix(
            c, g, sga, sgb, x2, w_conv_out[l].astype(BF16), row(b_conv_out[l]),
            w_gla_out[l].astype(BF16), w_out[l].astype(BF16), row(b_out[l]),
            row(ln1_g[l]), row(ln1_b[l]), wrh, wrl, b_r, tri)

        cnt_i = cnt[0, :N_EXPERTS].astype(jnp.int32)
        tok_grid = lambda m: m.reshape(N_TOK // LANE, LANE)
        xs, d0, d1, blk_ord, ord_e, nums = _dispatch(
            tok_grid(meta[0]), tok_grid(meta[1]), tok_grid(meta[2]), tok_grid(meta[3]), cnt_i, x1)
        ys = _experts(blk_ord, ord_e, nums, xs, w1[l], w3[l], w2[l])
        x2 = _combine(d0, d1, route, x1, ys, row(ln2_g[l]), row(ln2_b[l]))
    return x2.reshape(x.shape)
```

```python
import jax
import jax.numpy as jnp
from jax import lax
from jax.experimental import pallas as pl
from jax.experimental.pallas import tpu as pltpu

F32 = jnp.float32
BF16 = jnp.bfloat16

D_MODEL = 1024
BATCH = 8
SEQ = 2048
N_TOK = BATCH * SEQ
CHUNK = 64
CONV_WIDTH = 31
GLA_HEADS = 4
GLA_DK = 512
GLA_DV = 1024
GLA_HK = 128
GLA_HV = 256
GATE_RANK = 16
GATE_TAU = 16.0
N_GROUPS = 8
EXPERTS_PER_GROUP = 8
N_EXPERTS = 64
TOP_K = 2
D_EXPERT = 512
LN_EPS = 1e-5
RMS_EPS = 1e-6
DEEPNORM_ALPHA = 2.0 ** 0.25

LANE = 128
LANE_BITS = LANE.bit_length() - 1
TOKEN_ROWS = D_MODEL // LANE
F_PAD = LANE
N_PROJ_A = 2 * D_MODEL + 2 * GLA_DK + 2 * GLA_DV
N_PROJ = N_PROJ_A + 2 * D_MODEL + F_PAD
TM_PROJ = 512
PROJ_PIECE = 256
CONV_HALO = 32
CONV_ROWS = 32
SUBLANES = 8
CONV_SUB = 128
CONV_SHROWS = CONV_SUB + CONV_HALO
CONV_WIN = CONV_SHROWS + 16
TS_GLA = 256
NB_GLA = 4
TM_MIX = 1024
TM_SUB = 256
META_ROWS = 8
BM = 256
W_SLOTS = 3
N_BLOCKS = N_TOK * TOP_K // BM + N_EXPERTS
N_SLOTS = N_BLOCKS * BM
TC_COMB = 512
COMB_ROWS = 128
TD_DISP = 2048
DISP_UNROLL = LANE
VMEM_LIMIT = 56 * 1024 * 1024


def _sigmoid(x):
    return 1.0 / (1.0 + jnp.exp(-x))


def _to_token_tiles(ref, val, t0=0):
    n = val.shape[0]
    for c in range(TOKEN_ROWS):
        ref[pl.ds(t0 * TOKEN_ROWS + c, n, stride=TOKEN_ROWS), :] = val[:, c * LANE:(c + 1) * LANE]


def _from_token_tiles(ref, n, t0=0):
    return jnp.concatenate(
        [ref[pl.ds(t0 * TOKEN_ROWS + c, n, stride=TOKEN_ROWS), :] for c in range(TOKEN_ROWS)],
        axis=1)


def _layer_norm(z, g, b):
    mu = jnp.mean(z, axis=-1, keepdims=True)
    zc = z - mu
    var = jnp.mean(zc * zc, axis=-1, keepdims=True)
    return zc * lax.rsqrt(var + LN_EPS) * g + b


def _conv_shift(ext_ref, shf_ref, base):
    rows = CONV_WIN
    win = ext_ref[base:base + rows, :].astype(F32)
    for b in range(SUBLANES):
        sh = win if b == 0 else pltpu.roll(win, rows - b, axis=0)
        shf_ref[b * CONV_SHROWS:(b + 1) * CONV_SHROWS, :] = sh[0:CONV_SHROWS, :]


def _conv_rows(shf_ref, cw_ref, cb_ref, g_ref, b_ref, c_ref, base, r0):
    off = CONV_HALO - (CONV_WIDTH - 1)
    n_sub = CONV_ROWS // SUBLANES
    acc = [jnp.zeros((SUBLANES, D_MODEL), F32) for _ in range(n_sub)]
    for j in range(CONV_WIDTH):
        l0 = r0 + off + j
        b = l0 % SUBLANES
        m0 = b * CONV_SHROWS + (l0 - b)
        wj = cw_ref[j * SUBLANES:(j + 1) * SUBLANES, :]
        for k in range(n_sub):
            rk = m0 + k * SUBLANES
            acc[k] = acc[k] + shf_ref[rk:rk + SUBLANES, :] * wj
    y = _layer_norm(jnp.concatenate(acc, axis=0) + cb_ref[...], g_ref[...], b_ref[...])
    c_ref[base + r0:base + r0 + CONV_ROWS, :] = (y * _sigmoid(y)).astype(BF16)


def _inproj_kernel(x_ref, wa_ref, wb_ref, wf_ref, b_ref, wgu_ref, bgu_ref,
                   cw_ref, cb_ref, cg_ref, cbeta_ref,
                   c_ref, q_ref, k_ref, v_ref, rs_ref, sga_ref, sgb_ref, la_ref,
                   ext_ref, shf_ref, xb_ref):
    i = pl.program_id(0)
    xb_ref[...] = x_ref[...].astype(BF16)
    half = D_MODEL // 2

    def seg(c0, n):
        if c0 < N_PROJ_A:
            w = wa_ref[c0:c0 + n, :]
        elif c0 < N_PROJ_A + 2 * D_MODEL:
            w = wb_ref[c0 - N_PROJ_A:c0 - N_PROJ_A + n, :]
        else:
            w = wf_ref[...]
        h = lax.dot_general(xb_ref[...], w, (((1,), (1,)), ((), ())),
                            preferred_element_type=F32)
        return h + b_ref[:, c0:c0 + n]

    @pl.when(i == 0)
    def _():
        ext_ref[...] = jnp.zeros_like(ext_ref)

    prev_tail = ext_ref[TM_PROJ:TM_PROJ + CONV_HALO, :]
    ext_ref[0:CONV_HALO, :] = jnp.where(
        i % (SEQ // TM_PROJ) == 0, jnp.zeros_like(prev_tail), prev_tail)
    for j in range(2):
        a = seg(j * half, half)
        g = seg(D_MODEL + j * half, half)
        ext_ref[CONV_HALO:CONV_HALO + TM_PROJ, j * half:(j + 1) * half] = (
            a * _sigmoid(g)).astype(BF16)

    def piece(ref, col, c0, fn):
        def run():
            ref[:, col:col + PROJ_PIECE] = fn(seg(c0 + col, PROJ_PIECE)).astype(BF16)
        return run

    mxu_pieces = []
    for ref, c0, width, fn in (
            (q_ref, 2 * D_MODEL, GLA_DK, lambda h: h * (GLA_HK ** -0.5)),
            (k_ref, 2 * D_MODEL + GLA_DK, GLA_DK, lambda h: h),
            (v_ref, 3 * D_MODEL, GLA_DV, lambda h: h),
            (rs_ref, 4 * D_MODEL, GLA_DV, lambda h: h * _sigmoid(h)),
            (sga_ref, 5 * D_MODEL, D_MODEL, _sigmoid),
            (sgb_ref, 6 * D_MODEL, D_MODEL, _sigmoid)):
        mxu_pieces += [piece(ref, col, c0, fn) for col in range(0, width, PROJ_PIECE)]

    def forget_piece():
        f = seg(7 * D_MODEL, F_PAD)
        z = jnp.dot(f.astype(BF16), wgu_ref[...], preferred_element_type=F32) + bgu_ref[...]
        la_ref[...] = ((jnp.minimum(z, 0.0) - jnp.log(1.0 + jnp.exp(-jnp.abs(z))))
                       * (1.0 / GATE_TAU))

    mxu_pieces.append(forget_piece)

    valu_pieces = []
    for sb in range(TM_PROJ // CONV_SUB):
        shf = shf_ref.at[sb % 2]
        for r0 in range(0, CONV_SUB, CONV_ROWS):
            def run(sb=sb, shf=shf, r0=r0):
                if r0 == 0:
                    _conv_shift(ext_ref, shf, sb * CONV_SUB)
                _conv_rows(shf, cw_ref, cb_ref, cg_ref, cbeta_ref, c_ref, sb * CONV_SUB, r0)
            valu_pieces.append(run)

    n_v, n_m = len(valu_pieces), len(mxu_pieces)
    done = 0
    for vi, vrun in enumerate(valu_pieces):
        vrun()
        upto = (vi + 1) * n_m // n_v
        for mrun in mxu_pieces[done:upto]:
            mrun()
        done = upto


def _inproj(x2, w_a, w_b, w_f, b_p, wgu_p, bgu, cw_p, cb, cg, cbeta):
    tm = TM_PROJ
    row = lambda i: (i, 0)
    fixed = lambda i: (0, 0)
    tok = lambda n, dt: jax.ShapeDtypeStruct((N_TOK, n), dt)
    vec = pl.BlockSpec((1, D_MODEL), fixed)
    return pl.pallas_call(
        _inproj_kernel,
        grid=(N_TOK // tm,),
        in_specs=[
            pl.BlockSpec((tm, D_MODEL), row),
            pl.BlockSpec((N_PROJ_A, D_MODEL), fixed, pipeline_mode=pl.Buffered(1)),
            pl.BlockSpec((2 * D_MODEL, D_MODEL), fixed, pipeline_mode=pl.Buffered(1)),
            pl.BlockSpec((F_PAD, D_MODEL), fixed),
            pl.BlockSpec((1, N_PROJ), fixed),
            pl.BlockSpec((F_PAD, GLA_DK), fixed),
            pl.BlockSpec((1, GLA_DK), fixed),
            pl.BlockSpec((CONV_WIDTH * SUBLANES, D_MODEL), fixed),
            vec, vec, vec,
        ],
        out_specs=[
            pl.BlockSpec((tm, D_MODEL), row),
            pl.BlockSpec((tm, GLA_DK), row),
            pl.BlockSpec((tm, GLA_DK), row),
            pl.BlockSpec((tm, GLA_DV), row),
            pl.BlockSpec((tm, GLA_DV), row),
            pl.BlockSpec((tm, D_MODEL), row),
            pl.BlockSpec((tm, D_MODEL), row),
            pl.BlockSpec((tm, GLA_DK), row),
        ],
        out_shape=[
            tok(D_MODEL, BF16), tok(GLA_DK, BF16), tok(GLA_DK, BF16), tok(GLA_DV, BF16),
            tok(GLA_DV, BF16), tok(D_MODEL, BF16), tok(D_MODEL, BF16), tok(GLA_DK, F32),
        ],
        scratch_shapes=[
            pltpu.VMEM((tm + CONV_WIN - CONV_SUB, D_MODEL), BF16),
            pltpu.VMEM((2, SUBLANES * CONV_SHROWS, D_MODEL), F32),
            pltpu.VMEM((tm, D_MODEL), BF16),
        ],
        compiler_params=pltpu.CompilerParams(
            dimension_semantics=("arbitrary",), vmem_limit_bytes=VMEM_LIMIT),
        name="inproj",
    )(x2, w_a, w_b, w_f, b_p, wgu_p, bgu, cw_p, cb, cg, cbeta)


def _gla_kernel(q_ref, k_ref, v_ref, la_ref, rs_ref, ng_ref, o_ref,
                st_ref, kv_ref, sb_ref, dec_ref):
    s = pl.program_id(1)

    @pl.when(s == 0)
    def _():
        st_ref[...] = jnp.zeros_like(st_ref)

    rr = lax.broadcasted_iota(jnp.int32, (CHUNK, CHUNK), 0)
    cc = lax.broadcasted_iota(jnp.int32, (CHUNK, CHUNK), 1)
    tri = jnp.where(rr >= cc, 1.0, 0.0).astype(BF16)
    n_chunks = TS_GLA // CHUNK
    heads = [(slice(h * GLA_HK, (h + 1) * GLA_HK), slice(h * GLA_HV, (h + 1) * GLA_HV))
             for h in range(GLA_HEADS)]

    for nb in range(NB_GLA):
        for c in range(n_chunks):
            rows = slice(c * CHUNK, (c + 1) * CHUNK)
            la = la_ref[nb, rows, :]
            hi = la.astype(BF16)
            lo = (la - hi.astype(F32)).astype(BF16)
            cum = (jnp.dot(tri, hi, preferred_element_type=F32)
                   + jnp.dot(tri, lo, preferred_element_type=F32))
            cend = cum[CHUNK - 1:CHUNK, :]
            kd = (k_ref[nb, rows, :].astype(F32) * jnp.exp(cend - cum)).astype(BF16)
            idx = nb * n_chunks + c
            dec_ref[idx * SUBLANES:(idx + 1) * SUBLANES, :] = jnp.broadcast_to(
                jnp.exp(cend), (SUBLANES, GLA_DK))
            vc = v_ref[nb, rows, :]
            for h, (ks, vs) in enumerate(heads):
                kv_ref[idx * GLA_HEADS + h] = lax.dot_general(
                    vc[:, vs], kd[:, ks], (((0,), (0,)), ((), ())), preferred_element_type=F32)

    for nb in range(NB_GLA):
        for h, (ks, vs) in enumerate(heads):
            st = st_ref[nb * GLA_HEADS + h]
            for c in range(n_chunks):
                idx = nb * n_chunks + c
                st = st * dec_ref[idx * SUBLANES:idx * SUBLANES + 1, ks] + kv_ref[
                    idx * GLA_HEADS + h]
                sb_ref[idx * GLA_HEADS + h] = st.astype(BF16)
            st_ref[nb * GLA_HEADS + h] = st

    for nb in range(NB_GLA):
        for c in range(n_chunks):
            rows = slice(c * CHUNK, (c + 1) * CHUNK)
            idx = nb * n_chunks + c
            qc = q_ref[nb, rows, :]
            for h, (ks, vs) in enumerate(heads):
                o = lax.dot_general(qc[:, ks], sb_ref[idx * GLA_HEADS + h],
                                    (((1,), (1,)), ((), ())), preferred_element_type=F32)
                ms = jnp.mean(o * o, axis=-1, keepdims=True)
                on = o * lax.rsqrt(ms + RMS_EPS) * ng_ref[:, vs]
                o_ref[nb, rows, vs] = (on * rs_ref[nb, rows, vs].astype(F32)).astype(BF16)


def _gla(q, k, v, la, rs, ng):
    ts = TS_GLA
    blk = lambda n: pl.BlockSpec((NB_GLA, ts, n), lambda bi, si: (bi, si, 0))
    seq = lambda a: a.reshape(BATCH, SEQ, a.shape[-1])
    out = pl.pallas_call(
        _gla_kernel,
        grid=(BATCH // NB_GLA, SEQ // ts),
        in_specs=[blk(GLA_DK), blk(GLA_DK), blk(GLA_DV), blk(GLA_DK), blk(GLA_DV),
                  pl.BlockSpec((1, GLA_DV), lambda bi, si: (0, 0))],
        out_specs=blk(GLA_DV),
        out_shape=jax.ShapeDtypeStruct((BATCH, SEQ, GLA_DV), BF16),
        scratch_shapes=[
            pltpu.VMEM((NB_GLA * GLA_HEADS, GLA_HV, GLA_HK), F32),
            pltpu.VMEM((NB_GLA * (ts // CHUNK) * GLA_HEADS, GLA_HV, GLA_HK), F32),
            pltpu.VMEM((NB_GLA * (ts // CHUNK) * GLA_HEADS, GLA_HV, GLA_HK), BF16),
            pltpu.VMEM((NB_GLA * (ts // CHUNK) * SUBLANES, GLA_DK), F32),
        ],
        compiler_params=pltpu.CompilerParams(
            dimension_semantics=("arbitrary", "arbitrary"), vmem_limit_bytes=VMEM_LIMIT),
        name="gla",
    )(seq(q), seq(k), seq(v), seq(la), seq(rs), ng)
    return out.reshape(N_TOK, GLA_DV)


def _mix_kernel(c_ref, g_ref, sga_ref, sgb_ref, x_ref, wco_ref, bco_ref, wgl_ref, wo_ref, bo_ref,
                l1g_ref, l1b_ref, wrh_ref, wrl_ref, br_ref, tri_ref,
                x1_ref, route_ref, meta_ref, cnt_ref, carry_ref):
    i = pl.program_id(0)

    @pl.when(i == 0)
    def _():
        carry_ref[...] = jnp.zeros_like(carry_ref)

    def rows_of(sub):
        return slice(sub * TM_SUB, (sub + 1) * TM_SUB)

    def branch_projections(sub):
        rows = rows_of(sub)
        yc = jnp.dot(c_ref[rows, :], wco_ref[...], preferred_element_type=F32) + bco_ref[...]
        yg = jnp.dot(g_ref[rows, :], wgl_ref[...], preferred_element_type=F32)
        return yc, yg

    def out_projection(sub, yc, yg):
        rows = rows_of(sub)
        merged = sga_ref[rows, :].astype(F32) * yc + sgb_ref[rows, :].astype(F32) * yg
        return jnp.dot(merged.astype(BF16), wo_ref[...], preferred_element_type=F32) + bo_ref[...]

    def norm_and_logits(sub, mix):
        x1 = _layer_norm(DEEPNORM_ALPHA * x_ref[rows_of(sub), :] + mix, l1g_ref[...], l1b_ref[...])
        _to_token_tiles(x1_ref, x1, sub * TM_SUB)
        xh = x1.astype(BF16)
        xl = (x1 - xh.astype(F32)).astype(BF16)
        return (jnp.dot(xh, wrh_ref[...], preferred_element_type=F32)
                + jnp.dot(xl, wrh_ref[...], preferred_element_type=F32)
                + jnp.dot(xh, wrl_ref[...], preferred_element_type=F32)) + br_ref[...]

    def route(sub, lg):
        rows = rows_of(sub)
        lane = lax.broadcasted_iota(jnp.int32, (TM_SUB, LANE), 1)
        neg = -jnp.inf
        big = jnp.int32(1 << 20)
        is_g = (lane >= N_EXPERTS) & (lane < N_EXPERTS + N_GROUPS)
        gl = jnp.where(is_g, lg, neg)
        gmax = jnp.max(gl, axis=-1, keepdims=True)
        gsel = jnp.min(jnp.where(gl == gmax, lane, big), axis=-1, keepdims=True) - N_EXPERTS
        gw = 1.0 / jnp.sum(jnp.exp(gl - gmax), axis=-1, keepdims=True)
        in_grp = (lane < N_EXPERTS) & ((lane >> 3) == gsel)
        el = jnp.where(in_grp, lg, neg)
        m1 = jnp.max(el, axis=-1, keepdims=True)
        i1 = jnp.min(jnp.where(el == m1, lane, big), axis=-1, keepdims=True)
        el2 = jnp.where(lane == i1, neg, el)
        m2 = jnp.max(el2, axis=-1, keepdims=True)
        i2 = jnp.min(jnp.where(el2 == m2, lane, big), axis=-1, keepdims=True)
        t = jnp.exp(m2 - m1)
        w1 = gw / (1.0 + t)
        w2 = gw * t / (1.0 + t)

        oh1 = lane == i1
        oh2 = lane == i2
        osum = jnp.where(oh1 | oh2, 1.0, 0.0)
        excl = (jnp.dot(tri_ref[...], osum.astype(BF16), preferred_element_type=F32)
                + carry_ref[...])
        rank1 = jnp.sum(jnp.where(oh1, excl, 0.0), axis=-1, keepdims=True)
        rank2 = jnp.sum(jnp.where(oh2, excl, 0.0), axis=-1, keepdims=True)
        carry_ref[...] = carry_ref[...] + jnp.sum(osum, axis=0, keepdims=True)

        out = jnp.where(lane == 0, i1.astype(F32), 0.0)
        out = jnp.where(lane == 1, i2.astype(F32), out)
        out = jnp.where(lane == 2, rank1, out)
        out = jnp.where(lane == 3, rank2, out)
        out = jnp.where(lane == 4, w1, out)
        out = jnp.where(lane == 5, w2, out)
        route_ref[rows, :] = out
        meta_ref[:, rows] = out.T[0:META_ROWS, :].astype(jnp.int32)

    subs = range(TM_MIX // TM_SUB)
    branches = [branch_projections(sub) for sub in subs]
    mixes = [out_projection(sub, *branches[sub]) for sub in subs]
    logits = [norm_and_logits(sub, mixes[sub]) for sub in subs]
    for sub in subs:
        route(sub, logits[sub])
    cnt_ref[...] = carry_ref[...]


def _mix(c, g, sga, sgb, x2, wco, bco, wgl, wo, bo, l1g, l1b, wrh, wrl, br, tri):
    tm = TM_MIX
    row = lambda i: (i, 0)
    fixed = lambda i: (0, 0)
    mat = pl.BlockSpec((D_MODEL, D_MODEL), fixed, pipeline_mode=pl.Buffered(1))
    vec = pl.BlockSpec((1, D_MODEL), fixed)
    return pl.pallas_call(
        _mix_kernel,
        grid=(N_TOK // tm,),
        in_specs=[
            pl.BlockSpec((tm, D_MODEL), row), pl.BlockSpec((tm, D_MODEL), row),
            pl.BlockSpec((tm, D_MODEL), row), pl.BlockSpec((tm, D_MODEL), row),
            pl.BlockSpec((tm, D_MODEL), row),
            mat, vec, mat, mat, vec, vec, vec,
            pl.BlockSpec((D_MODEL, LANE), fixed), pl.BlockSpec((D_MODEL, LANE), fixed),
            pl.BlockSpec((1, LANE), fixed),
            pl.BlockSpec((TM_SUB, TM_SUB), fixed),
        ],
        out_specs=[
            pl.BlockSpec((tm * TOKEN_ROWS, LANE), row),
            pl.BlockSpec((tm, LANE), row),
            pl.BlockSpec((META_ROWS, tm), lambda i: (0, i)),
            pl.BlockSpec((1, LANE), fixed),
        ],
        out_shape=[
            jax.ShapeDtypeStruct((N_TOK * TOKEN_ROWS, LANE), F32),
            jax.ShapeDtypeStruct((N_TOK, LANE), F32),
            jax.ShapeDtypeStruct((META_ROWS, N_TOK), jnp.int32),
            jax.ShapeDtypeStruct((1, LANE), F32),
        ],
        scratch_shapes=[pltpu.VMEM((1, LANE), F32)],
        compiler_params=pltpu.CompilerParams(
            dimension_semantics=("arbitrary",), vmem_limit_bytes=VMEM_LIMIT),
        name="mix",
    )(c, g, sga, sgb, x2, wco, bco, wgl, wo, bo, l1g, l1b, wrh, wrl, br, tri)


def _dispatch_kernel(cnt_ref, e0_ref, e1_ref, r0_ref, r1_ref, x1_ref,
                     xs_ref, d0_ref, d1_ref, ord_ref, oe_ref, nums_ref,
                     pstart_ref, zero_ref, dvm_ref, sem, zsem, dsem):
    i = pl.program_id(0)

    def zero_slots(first_slot, n_slots):
        rows = n_slots * TOKEN_ROWS
        dst = pl.ds(pl.multiple_of(first_slot * TOKEN_ROWS, TOKEN_ROWS), rows)
        return pltpu.make_async_copy(zero_ref.at[pl.ds(0, rows)], xs_ref.at[dst], zsem)

    def pad_copies(e, visit):
        pad = (-cnt_ref[e]) & (BM - 1)
        slot = pstart_ref[e] + cnt_ref[e]
        for bit in [1 << p for p in reversed(range(BM.bit_length() - 1))]:
            has = (pad & bit) != 0

            @pl.when(has)
            def _():
                visit(zero_slots(slot, bit))

            slot = slot + (pad & bit)

    @pl.when(i == 0)
    def _():
        zero_ref[...] = jnp.zeros_like(zero_ref)

        def plan(e, carry):
            bstart, n_ord = carry
            nb = (cnt_ref[e] + (BM - 1)) // BM
            pstart_ref[e] = bstart * BM

            def fill(j, c):
                ord_ref[bstart + j] = n_ord
                return c

            lax.fori_loop(0, nb, fill, 0)

            @pl.when(nb > 0)
            def _():
                oe_ref[n_ord] = e

            pad_copies(e, lambda cp: cp.start())

            return bstart + nb, n_ord + jnp.where(nb > 0, 1, 0)

        n_used, n_ord = lax.fori_loop(0, N_EXPERTS, plan, (jnp.int32(0), jnp.int32(0)))
        nums_ref[0] = n_used
        nums_ref[1] = n_ord

        def fill_rest(j, c):
            ord_ref[j] = n_ord - 1
            zero_slots(j * BM, BM).start()
            return c

        lax.fori_loop(n_used, N_BLOCKS, fill_rest, 0)

        def fill_oe(j, c):
            oe_ref[j] = 0
            return c

        lax.fori_loop(n_ord, N_EXPERTS, fill_oe, 0)

        for e_ref, r_ref, d_ref, k in ((e0_ref, r0_ref, d0_ref, 0), (e1_ref, r1_ref, d1_ref, 1)):
            ev = e_ref[...]
            dv = r_ref[...]
            for e in range(N_EXPERTS):
                dv = dv + jnp.where(ev == e, pstart_ref[e], 0)
            dvm_ref[k] = dv
            pltpu.make_async_copy(dvm_ref.at[k], d_ref, dsem).start()
        for d_ref, k in ((d0_ref, 0), (d1_ref, 1)):
            pltpu.make_async_copy(dvm_ref.at[k], d_ref, dsem).wait()

    base = i * TD_DISP

    def tok_group(tg, carry):
        drow = (base >> LANE_BITS) + tg
        for uu in range(DISP_UNROLL):
            tt = tg * DISP_UNROLL + uu
            src = x1_ref.at[pl.ds(pl.multiple_of(tt * TOKEN_ROWS, TOKEN_ROWS), TOKEN_ROWS)]
            for kk, d_ref in enumerate((d0_ref, d1_ref)):
                d = d_ref[drow, uu]
                pltpu.make_async_copy(
                    src, xs_ref.at[pl.ds(pl.multiple_of(d * TOKEN_ROWS, TOKEN_ROWS), TOKEN_ROWS)],
                    sem).start(priority=kk)
        return carry

    lax.fori_loop(0, TD_DISP // DISP_UNROLL, tok_group, 0)
    for kk in range(TOP_K):
        pltpu.make_async_copy(x1_ref, xs_ref.at[pl.ds(0, TD_DISP * TOKEN_ROWS)], sem).wait()

    @pl.when(i == pl.num_programs(0) - 1)
    def _():
        def drain_pad(e, c):
            pad_copies(e, lambda cp: cp.wait())
            return c

        lax.fori_loop(0, N_EXPERTS, drain_pad, 0)

        def drain_rest(j, c):
            zero_slots(0, BM).wait()
            return c

        lax.fori_loop(nums_ref[0], N_BLOCKS, drain_rest, 0)


def _dispatch(e0, e1, r0, r1, cnt, x1):
    smem = pl.BlockSpec(memory_space=pltpu.SMEM)
    idx = pl.BlockSpec((N_TOK // LANE, LANE), lambda i, c: (0, 0))
    grid_spec = pltpu.PrefetchScalarGridSpec(
        num_scalar_prefetch=1,
        grid=(N_TOK // TD_DISP,),
        in_specs=[idx, idx, idx, idx,
                  pl.BlockSpec((TD_DISP * TOKEN_ROWS, LANE), lambda i, c: (i, 0))],
        out_specs=[pl.BlockSpec(memory_space=pl.ANY), smem, smem, smem, smem, smem],
        scratch_shapes=[
            pltpu.SMEM((N_EXPERTS,), jnp.int32),
            pltpu.VMEM((BM * TOKEN_ROWS, LANE), F32),
            pltpu.VMEM((TOP_K, N_TOK // LANE, LANE), jnp.int32),
            pltpu.SemaphoreType.DMA,
            pltpu.SemaphoreType.DMA,
            pltpu.SemaphoreType.DMA,
        ],
    )
    return pl.pallas_call(
        _dispatch_kernel,
        grid_spec=grid_spec,
        out_shape=[
            jax.ShapeDtypeStruct((N_SLOTS * TOKEN_ROWS, LANE), F32),
            jax.ShapeDtypeStruct((N_TOK // LANE, LANE), jnp.int32),
            jax.ShapeDtypeStruct((N_TOK // LANE, LANE), jnp.int32),
            jax.ShapeDtypeStruct((N_BLOCKS,), jnp.int32),
            jax.ShapeDtypeStruct((N_EXPERTS,), jnp.int32),
            jax.ShapeDtypeStruct((2,), jnp.int32),
        ],
        compiler_params=pltpu.CompilerParams(
            dimension_semantics=("arbitrary",), vmem_limit_bytes=VMEM_LIMIT),
        name="dispatch",
    )(cnt, e0, e1, r0, r1, x1)


def _expert_kernel(ord_ref, oe_ref, nums_ref, xs_ref, w1_hbm, w3_hbm, w2_hbm, ys_ref,
                   w1f_ref, w3f_ref, w2f_ref, w1b_ref, w3b_ref, w2b_ref, sem):
    i = pl.program_id(0)
    n_used = nums_ref[0]
    n_ord = nums_ref[1]
    active = i < n_used
    k = ord_ref[i]
    first = (i == 0) | (k != ord_ref[jnp.maximum(i - 1, 0)])

    def weight_copies(kk):
        e = oe_ref[kk]
        slot = kk % W_SLOTS
        return [pltpu.make_async_copy(w_hbm.at[e], wf_ref.at[slot], sem.at[slot])
                for w_hbm, wf_ref in ((w1_hbm, w1f_ref), (w3_hbm, w3f_ref), (w2_hbm, w2f_ref))]

    def start_weights(kk):
        @pl.when(kk < n_ord)
        def _():
            for cp in weight_copies(kk):
                cp.start(priority=1)

    @pl.when(active & first)
    def _():
        @pl.when(i == 0)
        def _():
            for ahead in range(W_SLOTS - 1):
                start_weights(k + ahead)

        start_weights(k + W_SLOTS - 1)
        for cp in weight_copies(k):
            cp.wait()
        slot = k % W_SLOTS
        w1b_ref[...] = w1f_ref[slot].astype(BF16)
        w3b_ref[...] = w3f_ref[slot].astype(BF16)
        w2b_ref[...] = w2f_ref[slot].astype(BF16)

    @pl.when(active)
    def _():
        xb = _from_token_tiles(xs_ref, BM).astype(BF16)
        h1 = jnp.dot(xb, w1b_ref[...], preferred_element_type=F32)
        h3 = jnp.dot(xb, w3b_ref[...], preferred_element_type=F32)
        hdn = (h1 * _sigmoid(h1) * h3).astype(BF16)
        _to_token_tiles(ys_ref, jnp.dot(hdn, w2b_ref[...], preferred_element_type=F32))

    @pl.when(jnp.logical_not(active))
    def _():
        ys_ref[...] = jnp.zeros_like(ys_ref)


def _experts(blk_ord, ord_e, nums, xs, w1, w3, w2):
    last = lambda i, nums_ref: jnp.maximum(jnp.minimum(i, nums_ref[0] - 1), 0)
    hbm = pl.BlockSpec(memory_space=pl.ANY)
    grid_spec = pltpu.PrefetchScalarGridSpec(
        num_scalar_prefetch=3,
        grid=(N_BLOCKS,),
        in_specs=[
            pl.BlockSpec((BM * TOKEN_ROWS, LANE), lambda i, o, oe, nu: (last(i, nu), 0)),
            hbm, hbm, hbm,
        ],
        out_specs=pl.BlockSpec((BM * TOKEN_ROWS, LANE), lambda i, o, oe, nu: (i, 0)),
        scratch_shapes=[
            pltpu.VMEM((W_SLOTS, D_MODEL, D_EXPERT), F32),
            pltpu.VMEM((W_SLOTS, D_MODEL, D_EXPERT), F32),
            pltpu.VMEM((W_SLOTS, D_EXPERT, D_MODEL), F32),
            pltpu.VMEM((D_MODEL, D_EXPERT), BF16),
            pltpu.VMEM((D_MODEL, D_EXPERT), BF16),
            pltpu.VMEM((D_EXPERT, D_MODEL), BF16),
            pltpu.SemaphoreType.DMA((W_SLOTS,)),
        ],
    )
    return pl.pallas_call(
        _expert_kernel,
        grid_spec=grid_spec,
        out_shape=jax.ShapeDtypeStruct((N_SLOTS * TOKEN_ROWS, LANE), F32),
        compiler_params=pltpu.CompilerParams(
            dimension_semantics=("arbitrary",), vmem_limit_bytes=VMEM_LIMIT),
        name="experts",
    )(blk_ord, ord_e, nums, xs, w1, w3, w2)


def _combine_kernel(d0_ref, d1_ref, route_ref, x1_ref, ys_ref, g_ref, b_ref, o_ref, y_ref, sem):
    i = pl.program_id(0)
    n_tiles = pl.num_programs(0)
    tile_rows = TC_COMB * TOKEN_ROWS

    def issue(tile, slot, t0, n):
        drow = (tile * TC_COMB + t0) >> LANE_BITS
        for uu in range(n):
            t = t0 + uu
            dst = pl.ds(pl.multiple_of(t * TOKEN_ROWS, TOKEN_ROWS), TOKEN_ROWS)
            for kk, d_ref in enumerate((d0_ref, d1_ref)):
                d = d_ref[drow + uu // LANE, uu % LANE]
                src = pl.ds(pl.multiple_of(d * TOKEN_ROWS, TOKEN_ROWS), TOKEN_ROWS)
                pltpu.make_async_copy(ys_ref.at[src], y_ref.at[slot, kk, dst],
                                      sem.at[slot]).start(priority=kk)

    def finish(slot, t0):
        t0 = pl.multiple_of(t0, COMB_ROWS)
        w0 = route_ref[pl.ds(t0, COMB_ROWS), 4:5]
        w1 = route_ref[pl.ds(t0, COMB_ROWS), 5:6]
        z = (DEEPNORM_ALPHA * _from_token_tiles(x1_ref, COMB_ROWS, t0)
             + w0 * _from_token_tiles(y_ref.at[slot, 0], COMB_ROWS, t0)
             + w1 * _from_token_tiles(y_ref.at[slot, 1], COMB_ROWS, t0))
        o_ref[pl.ds(t0, COMB_ROWS), :] = _layer_norm(z, g_ref[...], b_ref[...])

    @pl.when(i == 0)
    def _():
        def first(gi, carry):
            issue(0, 0, gi * COMB_ROWS, COMB_ROWS)
            return carry

        lax.fori_loop(0, TC_COMB // COMB_ROWS, first, 0)

    slot = i % 2
    for kk in range(TOP_K):
        pltpu.make_async_copy(ys_ref.at[pl.ds(0, tile_rows)], y_ref.at[slot, kk],
                              sem.at[slot]).wait()

    @pl.when(i + 1 < n_tiles)
    def _():
        def both(gi, carry):
            issue(i + 1, 1 - slot, gi * COMB_ROWS, COMB_ROWS)
            finish(slot, gi * COMB_ROWS)
            return carry

        lax.fori_loop(0, TC_COMB // COMB_ROWS, both, 0)

    @pl.when(i + 1 == n_tiles)
    def _():
        def last(gi, carry):
            finish(slot, gi * COMB_ROWS)
            return carry

        lax.fori_loop(0, TC_COMB // COMB_ROWS, last, 0)


def _combine(d0, d1, route, x1, ys, g, b):
    tc = TC_COMB
    grid_spec = pltpu.PrefetchScalarGridSpec(
        num_scalar_prefetch=2,
        grid=(N_TOK // tc,),
        in_specs=[
            pl.BlockSpec((tc, LANE), lambda i, *_: (i, 0)),
            pl.BlockSpec((tc * TOKEN_ROWS, LANE), lambda i, *_: (i, 0)),
            pl.BlockSpec(memory_space=pl.ANY),
            pl.BlockSpec((1, D_MODEL), lambda i, *_: (0, 0)),
            pl.BlockSpec((1, D_MODEL), lambda i, *_: (0, 0)),
        ],
        out_specs=pl.BlockSpec((tc, D_MODEL), lambda i, *_: (i, 0)),
        scratch_shapes=[
            pltpu.VMEM((2, TOP_K, tc * TOKEN_ROWS, LANE), F32),
            pltpu.SemaphoreType.DMA((2,)),
        ],
    )
    return pl.pallas_call(
        _combine_kernel,
        grid_spec=grid_spec,
        out_shape=jax.ShapeDtypeStruct((N_TOK, D_MODEL), F32),
        compiler_params=pltpu.CompilerParams(
            dimension_semantics=("arbitrary",), vmem_limit_bytes=VMEM_LIMIT),
        name="combine",
    )(d0, d1, route, x1, ys, g, b)


def _split_bf16(w):
    hi = w.astype(BF16)
    lo = (w - hi.astype(F32)).astype(BF16)
    return hi, lo


def kernel(x, w_in, b_in, conv_w, conv_b, conv_ln_g, conv_ln_b, w_conv_out, b_conv_out, w_gate_up, b_gate_up, gla_norm_g, w_gla_out, w_out, b_out, ln1_g, ln1_b, w_router_group, b_router_group, w_router_expert, b_router_expert, w1, w3, w2, ln2_g, ln2_b):
    x2 = x.reshape(N_TOK, D_MODEL)
    row = lambda v: v.reshape(1, -1)
    for l in range(w_in.shape[0]):
        f0 = N_PROJ_A
        f1 = f0 + GATE_RANK
        w_l, b_l = w_in[l], b_in[l]
        w_t = jnp.swapaxes(w_l, 0, 1)
        w_a = w_t[:f0].astype(BF16)
        w_b = w_t[f1:].astype(BF16)
        w_f = jnp.pad(w_t[f0:f1], ((0, F_PAD - GATE_RANK), (0, 0))).astype(BF16)
        b_p = row(jnp.concatenate(
            [b_l[:f0], b_l[f1:], b_l[f0:f1], jnp.zeros((F_PAD - GATE_RANK,), F32)]))
        wgu_p = jnp.concatenate(
            [w_gate_up[l], jnp.zeros((F_PAD - GATE_RANK, GLA_DK), F32)], axis=0).astype(BF16)
        cw_p = jnp.repeat(conv_w[l], SUBLANES, axis=0)
        c, q, k, v, rs, sga, sgb, la = _inproj(
            x2, w_a, w_b, w_f, b_p, wgu_p, row(b_gate_up[l]), cw_p,
            row(conv_b[l]), row(conv_ln_g[l]), row(conv_ln_b[l]))
        g = _gla(q, k, v, la, rs, row(gla_norm_g[l]))

        w_r = jnp.concatenate(
            [w_router_expert[l], w_router_group[l],
             jnp.zeros((D_MODEL, LANE - N_EXPERTS - N_GROUPS), F32)], axis=1)
        b_r = row(jnp.concatenate(
            [b_router_expert[l], b_router_group[l],
             jnp.zeros((LANE - N_EXPERTS - N_GROUPS,), F32)]))
        wrh, wrl = _split_bf16(w_r)
        ri = lax.broadcasted_iota(jnp.int32, (TM_SUB, TM_SUB), 0)
        ci = lax.broadcasted_iota(jnp.int32, (TM_SUB, TM_SUB), 1)
        tri = (ri > ci).astype(BF16)
        x1, route, meta, cnt = _mix(
            c, g, sga, sgb, x2, w_conv_out[l].astype(BF16), row(b_conv_out[l]),
            w_gla_out[l].astype(BF16), w_out[l].astype(BF16), row(b_out[l]),
            row(ln1_g[l]), row(ln1_b[l]), wrh, wrl, b_r, tri)

        cnt_i = cnt[0, :N_EXPERTS].astype(jnp.int32)
        tok_grid = lambda m: m.reshape(N_TOK // LANE, LANE)
        xs, d0, d1, blk_ord, ord_e, nums = _dispatch(
            tok_grid(meta[0]), tok_grid(meta[1]), tok_grid(meta[2]), tok_grid(meta[3]), cnt_i, x1)
        ys = _experts(blk_ord, ord_e, nums, xs, w1[l], w3[l], w2[l])
        x2 = _combine(d0, d1, route, x1, ys, row(ln2_g[l]), row(ln2_b[l]))
    return x2.reshape(x.shape)
```

```python
import jax
import jax.numpy as jnp
from jax import lax
from jax.experimental import pallas as pl
from jax.experimental.pallas import tpu as pltpu

F32 = jnp.float32
BF16 = jnp.bfloat16

D_MODEL = 1024
BATCH = 8
SEQ = 2048
N_TOK = BATCH * SEQ
CHUNK = 64
CONV_WIDTH = 31
GLA_HEADS = 4
GLA_DK = 512
GLA_DV = 1024
GLA_HK = 128
GLA_HV = 256
GATE_RANK = 16
GATE_TAU = 16.0
N_GROUPS = 8
EXPERTS_PER_GROUP = 8
GROUP_BITS = EXPERTS_PER_GROUP.bit_length() - 1
N_EXPERTS = 64
TOP_K = 2
D_EXPERT = 512
LN_EPS = 1e-5
RMS_EPS = 1e-6
DEEPNORM_ALPHA = 2.0 ** 0.25

LANE = 128
LANE_BITS = LANE.bit_length() - 1
TOKEN_ROWS = D_MODEL // LANE
F_PAD = LANE
N_PROJ_A = 2 * D_MODEL + 2 * GLA_DK + 2 * GLA_DV
N_PROJ = N_PROJ_A + 2 * D_MODEL + F_PAD
TM_PROJ = 512
PROJ_PIECE = 256
CONV_HALO = 32
CONV_ROWS = 32
SUBLANES = 8
CONV_SUB = 128
CONV_SHROWS = CONV_SUB + CONV_HALO
CONV_WIN = CONV_SHROWS + 16
TS_GLA = 256
NB_GLA = 4
TM_MIX = 1024
TM_SUB = 256
META_ROWS = 8
BM = 256
W_SLOTS = 3
N_BLOCKS = N_TOK * TOP_K // BM + N_EXPERTS
N_SLOTS = N_BLOCKS * BM
TC_COMB = 512
COMB_ROWS = 128
TD_DISP = 2048
DISP_UNROLL = LANE
VMEM_LIMIT = 56 * 1024 * 1024


def _sigmoid(x):
    return 1.0 / (1.0 + jnp.exp(-x))


def _to_token_tiles(ref, val, t0=0):
    n = val.shape[0]
    for c in range(TOKEN_ROWS):
        ref[pl.ds(t0 * TOKEN_ROWS + c, n, stride=TOKEN_ROWS), :] = val[:, c * LANE:(c + 1) * LANE]


def _from_token_tiles(ref, n, t0=0):
    return jnp.concatenate(
        [ref[pl.ds(t0 * TOKEN_ROWS + c, n, stride=TOKEN_ROWS), :] for c in range(TOKEN_ROWS)],
        axis=1)


def _layer_norm(z, g, b):
    mu = jnp.mean(z, axis=-1, keepdims=True)
    zc = z - mu
    var = jnp.mean(zc * zc, axis=-1, keepdims=True)
    return zc * lax.rsqrt(var + LN_EPS) * g + b


def _conv_shift(ext_ref, shf_ref, base):
    rows = CONV_WIN
    win = ext_ref[base:base + rows, :].astype(F32)
    for b in range(SUBLANES):
        sh = win if b == 0 else pltpu.roll(win, rows - b, axis=0)
        shf_ref[b * CONV_SHROWS:(b + 1) * CONV_SHROWS, :] = sh[0:CONV_SHROWS, :]


def _conv_rows(shf_ref, cw_ref, cb_ref, g_ref, b_ref, c_ref, base, r0):
    off = CONV_HALO - (CONV_WIDTH - 1)
    n_sub = CONV_ROWS // SUBLANES
    acc = [jnp.zeros((SUBLANES, D_MODEL), F32) for _ in range(n_sub)]
    for j in range(CONV_WIDTH):
        l0 = r0 + off + j
        b = l0 % SUBLANES
        m0 = b * CONV_SHROWS + (l0 - b)
        wj = cw_ref[j * SUBLANES:(j + 1) * SUBLANES, :]
        for k in range(n_sub):
            rk = m0 + k * SUBLANES
            acc[k] = acc[k] + shf_ref[rk:rk + SUBLANES, :] * wj
    y = _layer_norm(jnp.concatenate(acc, axis=0) + cb_ref[...], g_ref[...], b_ref[...])
    c_ref[base + r0:base + r0 + CONV_ROWS, :] = (y * _sigmoid(y)).astype(BF16)


def _inproj_kernel(x_ref, wa_ref, wb_ref, wf_ref, b_ref, wgu_ref, bgu_ref,
                   cw_ref, cb_ref, cg_ref, cbeta_ref,
                   c_ref, q_ref, k_ref, v_ref, rs_ref, sga_ref, sgb_ref, la_ref,
                   ext_ref, shf_ref, xb_ref):
    i = pl.program_id(0)
    xb_ref[...] = x_ref[...].astype(BF16)
    half = D_MODEL // 2

    def seg(c0, n):
        if c0 < N_PROJ_A:
            w = wa_ref[c0:c0 + n, :]
        elif c0 < N_PROJ_A + 2 * D_MODEL:
            w = wb_ref[c0 - N_PROJ_A:c0 - N_PROJ_A + n, :]
        else:
            w = wf_ref[...]
        h = lax.dot_general(xb_ref[...], w, (((1,), (1,)), ((), ())),
                            preferred_element_type=F32)
        return h + b_ref[:, c0:c0 + n]

    @pl.when(i == 0)
    def _():
        ext_ref[...] = jnp.zeros_like(ext_ref)

    prev_tail = ext_ref[TM_PROJ:TM_PROJ + CONV_HALO, :]
    ext_ref[0:CONV_HALO, :] = jnp.where(
        i % (SEQ // TM_PROJ) == 0, jnp.zeros_like(prev_tail), prev_tail)
    for j in range(2):
        a = seg(j * half, half)
        g = seg(D_MODEL + j * half, half)
        ext_ref[CONV_HALO:CONV_HALO + TM_PROJ, j * half:(j + 1) * half] = (
            a * _sigmoid(g)).astype(BF16)

    def piece(ref, col, c0, fn):
        def run():
            ref[:, col:col + PROJ_PIECE] = fn(seg(c0 + col, PROJ_PIECE)).astype(BF16)
        return run

    mxu_pieces = []
    for ref, c0, width, fn in (
            (q_ref, 2 * D_MODEL, GLA_DK, lambda h: h * (GLA_HK ** -0.5)),
            (k_ref, 2 * D_MODEL + GLA_DK, GLA_DK, lambda h: h),
            (v_ref, 3 * D_MODEL, GLA_DV, lambda h: h),
            (rs_ref, 4 * D_MODEL, GLA_DV, lambda h: h * _sigmoid(h)),
            (sga_ref, 5 * D_MODEL, D_MODEL, _sigmoid),
            (sgb_ref, 6 * D_MODEL, D_MODEL, _sigmoid)):
        mxu_pieces += [piece(ref, col, c0, fn) for col in range(0, width, PROJ_PIECE)]

    def forget_piece():
        f = seg(7 * D_MODEL, F_PAD)
        z = jnp.dot(f.astype(BF16), wgu_ref[...], preferred_element_type=F32) + bgu_ref[...]
        la_ref[...] = ((jnp.minimum(z, 0.0) - jnp.log(1.0 + jnp.exp(-jnp.abs(z))))
                       * (1.0 / GATE_TAU))

    mxu_pieces.append(forget_piece)

    valu_pieces = []
    for sb in range(TM_PROJ // CONV_SUB):
        shf = shf_ref.at[sb % 2]
        for r0 in range(0, CONV_SUB, CONV_ROWS):
            def run(sb=sb, shf=shf, r0=r0):
                if r0 == 0:
                    _conv_shift(ext_ref, shf, sb * CONV_SUB)
                _conv_rows(shf, cw_ref, cb_ref, cg_ref, cbeta_ref, c_ref, sb * CONV_SUB, r0)
            valu_pieces.append(run)

    n_v, n_m = len(valu_pieces), len(mxu_pieces)
    done = 0
    for vi, vrun in enumerate(valu_pieces):
        vrun()
        upto = (vi + 1) * n_m // n_v
        for mrun in mxu_pieces[done:upto]:
            mrun()
        done = upto


def _inproj(x2, w_a, w_b, w_f, b_p, wgu_p, bgu, cw_p, cb, cg, cbeta):
    tm = TM_PROJ
    row = lambda i: (i, 0)
    fixed = lambda i: (0, 0)
    tok = lambda n, dt: jax.ShapeDtypeStruct((N_TOK, n), dt)
    vec = pl.BlockSpec((1, D_MODEL), fixed)
    return pl.pallas_call(
        _inproj_kernel,
        grid=(N_TOK // tm,),
        in_specs=[
            pl.BlockSpec((tm, D_MODEL), row),
            pl.BlockSpec((N_PROJ_A, D_MODEL), fixed, pipeline_mode=pl.Buffered(1)),
            pl.BlockSpec((2 * D_MODEL, D_MODEL), fixed, pipeline_mode=pl.Buffered(1)),
            pl.BlockSpec((F_PAD, D_MODEL), fixed),
            pl.BlockSpec((1, N_PROJ), fixed),
            pl.BlockSpec((F_PAD, GLA_DK), fixed),
            pl.BlockSpec((1, GLA_DK), fixed),
            pl.BlockSpec((CONV_WIDTH * SUBLANES, D_MODEL), fixed),
            vec, vec, vec,
        ],
        out_specs=[
            pl.BlockSpec((tm, D_MODEL), row),
            pl.BlockSpec((tm, GLA_DK), row),
            pl.BlockSpec((tm, GLA_DK), row),
            pl.BlockSpec((tm, GLA_DV), row),
            pl.BlockSpec((tm, GLA_DV), row),
            pl.BlockSpec((tm, D_MODEL), row),
            pl.BlockSpec((tm, D_MODEL), row),
            pl.BlockSpec((tm, GLA_DK), row),
        ],
        out_shape=[
            tok(D_MODEL, BF16), tok(GLA_DK, BF16), tok(GLA_DK, BF16), tok(GLA_DV, BF16),
            tok(GLA_DV, BF16), tok(D_MODEL, BF16), tok(D_MODEL, BF16), tok(GLA_DK, F32),
        ],
        scratch_shapes=[
            pltpu.VMEM((tm + CONV_WIN - CONV_SUB, D_MODEL), BF16),
            pltpu.VMEM((2, SUBLANES * CONV_SHROWS, D_MODEL), F32),
            pltpu.VMEM((tm, D_MODEL), BF16),
        ],
        compiler_params=pltpu.CompilerParams(
            dimension_semantics=("arbitrary",), vmem_limit_bytes=VMEM_LIMIT),
        name="inproj",
    )(x2, w_a, w_b, w_f, b_p, wgu_p, bgu, cw_p, cb, cg, cbeta)


def _gla_kernel(q_ref, k_ref, v_ref, la_ref, rs_ref, ng_ref, o_ref,
                st_ref, kv_ref, sb_ref, dec_ref):
    s = pl.program_id(1)

    @pl.when(s == 0)
    def _():
        st_ref[...] = jnp.zeros_like(st_ref)

    rr = lax.broadcasted_iota(jnp.int32, (CHUNK, CHUNK), 0)
    cc = lax.broadcasted_iota(jnp.int32, (CHUNK, CHUNK), 1)
    tri = jnp.where(rr >= cc, 1.0, 0.0).astype(BF16)
    n_chunks = TS_GLA // CHUNK
    heads = [(slice(h * GLA_HK, (h + 1) * GLA_HK), slice(h * GLA_HV, (h + 1) * GLA_HV))
             for h in range(GLA_HEADS)]

    for nb in range(NB_GLA):
        for c in range(n_chunks):
            rows = slice(c * CHUNK, (c + 1) * CHUNK)
            la = la_ref[nb, rows, :]
            hi = la.astype(BF16)
            lo = (la - hi.astype(F32)).astype(BF16)
            cum = (jnp.dot(tri, hi, preferred_element_type=F32)
                   + jnp.dot(tri, lo, preferred_element_type=F32))
            cend = cum[CHUNK - 1:CHUNK, :]
            kd = (k_ref[nb, rows, :].astype(F32) * jnp.exp(cend - cum)).astype(BF16)
            idx = nb * n_chunks + c
            dec_ref[idx * SUBLANES:(idx + 1) * SUBLANES, :] = jnp.broadcast_to(
                jnp.exp(cend), (SUBLANES, GLA_DK))
            vc = v_ref[nb, rows, :]
            for h, (ks, vs) in enumerate(heads):
                kv_ref[idx * GLA_HEADS + h] = lax.dot_general(
                    vc[:, vs], kd[:, ks], (((0,), (0,)), ((), ())), preferred_element_type=F32)

    for nb in range(NB_GLA):
        for h, (ks, vs) in enumerate(heads):
            st = st_ref[nb * GLA_HEADS + h]
            for c in range(n_chunks):
                idx = nb * n_chunks + c
                st = st * dec_ref[idx * SUBLANES:idx * SUBLANES + 1, ks] + kv_ref[
                    idx * GLA_HEADS + h]
                sb_ref[idx * GLA_HEADS + h] = st.astype(BF16)
            st_ref[nb * GLA_HEADS + h] = st

    for nb in range(NB_GLA):
        for c in range(n_chunks):
            rows = slice(c * CHUNK, (c + 1) * CHUNK)
            idx = nb * n_chunks + c
            qc = q_ref[nb, rows, :]
            for h, (ks, vs) in enumerate(heads):
                o = lax.dot_general(qc[:, ks], sb_ref[idx * GLA_HEADS + h],
                                    (((1,), (1,)), ((), ())), preferred_element_type=F32)
                ms = jnp.mean(o * o, axis=-1, keepdims=True)
                on = o * lax.rsqrt(ms + RMS_EPS) * ng_ref[:, vs]
                o_ref[nb, rows, vs] = (on * rs_ref[nb, rows, vs].astype(F32)).astype(BF16)


def _gla(q, k, v, la, rs, ng):
    ts = TS_GLA
    blk = lambda n: pl.BlockSpec((NB_GLA, ts, n), lambda bi, si: (bi, si, 0))
    seq = lambda a: a.reshape(BATCH, SEQ, a.shape[-1])
    out = pl.pallas_call(
        _gla_kernel,
        grid=(BATCH // NB_GLA, SEQ // ts),
        in_specs=[blk(GLA_DK), blk(GLA_DK), blk(GLA_DV), blk(GLA_DK), blk(GLA_DV),
                  pl.BlockSpec((1, GLA_DV), lambda bi, si: (0, 0))],
        out_specs=blk(GLA_DV),
        out_shape=jax.ShapeDtypeStruct((BATCH, SEQ, GLA_DV), BF16),
        scratch_shapes=[
            pltpu.VMEM((NB_GLA * GLA_HEADS, GLA_HV, GLA_HK), F32),
            pltpu.VMEM((NB_GLA * (ts // CHUNK) * GLA_HEADS, GLA_HV, GLA_HK), F32),
            pltpu.VMEM((NB_GLA * (ts // CHUNK) * GLA_HEADS, GLA_HV, GLA_HK), BF16),
            pltpu.VMEM((NB_GLA * (ts // CHUNK) * SUBLANES, GLA_DK), F32),
        ],
        compiler_params=pltpu.CompilerParams(
            dimension_semantics=("arbitrary", "arbitrary"), vmem_limit_bytes=VMEM_LIMIT),
        name="gla",
    )(seq(q), seq(k), seq(v), seq(la), seq(rs), ng)
    return out.reshape(N_TOK, GLA_DV)


def _mix_kernel(c_ref, g_ref, sga_ref, sgb_ref, x_ref, wco_ref, bco_ref, wgl_ref, wo_ref, bo_ref,
                l1g_ref, l1b_ref, wrh_ref, wrl_ref, br_ref, tri_ref,
                x1_ref, route_ref, meta_ref, cnt_ref, carry_ref):
    i = pl.program_id(0)

    @pl.when(i == 0)
    def _():
        carry_ref[...] = jnp.zeros_like(carry_ref)

    def rows_of(sub):
        return slice(sub * TM_SUB, (sub + 1) * TM_SUB)

    def branch_projections(sub):
        rows = rows_of(sub)
        yc = jnp.dot(c_ref[rows, :], wco_ref[...], preferred_element_type=F32) + bco_ref[...]
        yg = jnp.dot(g_ref[rows, :], wgl_ref[...], preferred_element_type=F32)
        return yc, yg

    def out_projection(sub, yc, yg):
        rows = rows_of(sub)
        merged = sga_ref[rows, :].astype(F32) * yc + sgb_ref[rows, :].astype(F32) * yg
        return jnp.dot(merged.astype(BF16), wo_ref[...], preferred_element_type=F32) + bo_ref[...]

    def norm_and_logits(sub, mix):
        x1 = _layer_norm(DEEPNORM_ALPHA * x_ref[rows_of(sub), :] + mix, l1g_ref[...], l1b_ref[...])
        _to_token_tiles(x1_ref, x1, sub * TM_SUB)
        xh = x1.astype(BF16)
        xl = (x1 - xh.astype(F32)).astype(BF16)
        return (jnp.dot(xh, wrh_ref[...], preferred_element_type=F32)
                + jnp.dot(xl, wrh_ref[...], preferred_element_type=F32)
                + jnp.dot(xh, wrl_ref[...], preferred_element_type=F32)) + br_ref[...]

    def route(sub, lg):
        rows = rows_of(sub)
        lane = lax.broadcasted_iota(jnp.int32, (TM_SUB, LANE), 1)
        neg = -jnp.inf
        big = jnp.int32(1 << 20)
        is_g = (lane >= N_EXPERTS) & (lane < N_EXPERTS + N_GROUPS)
        gl = jnp.where(is_g, lg, neg)
        gmax = jnp.max(gl, axis=-1, keepdims=True)
        gsel = jnp.min(jnp.where(gl == gmax, lane, big), axis=-1, keepdims=True) - N_EXPERTS
        gw = 1.0 / jnp.sum(jnp.exp(gl - gmax), axis=-1, keepdims=True)
        in_grp = (lane < N_EXPERTS) & ((lane >> GROUP_BITS) == gsel)
        el = jnp.where(in_grp, lg, neg)
        m1 = jnp.max(el, axis=-1, keepdims=True)
        i1 = jnp.min(jnp.where(el == m1, lane, big), axis=-1, keepdims=True)
        el2 = jnp.where(lane == i1, neg, el)
        m2 = jnp.max(el2, axis=-1, keepdims=True)
        i2 = jnp.min(jnp.where(el2 == m2, lane, big), axis=-1, keepdims=True)
        t = jnp.exp(m2 - m1)
        w1 = gw / (1.0 + t)
        w2 = gw * t / (1.0 + t)

        oh1 = lane == i1
        oh2 = lane == i2
        osum = jnp.where(oh1 | oh2, 1.0, 0.0)
        excl = (jnp.dot(tri_ref[...], osum.astype(BF16), preferred_element_type=F32)
                + carry_ref[...])
        rank1 = jnp.sum(jnp.where(oh1, excl, 0.0), axis=-1, keepdims=True)
        rank2 = jnp.sum(jnp.where(oh2, excl, 0.0), axis=-1, keepdims=True)
        carry_ref[...] = carry_ref[...] + jnp.sum(osum, axis=0, keepdims=True)

        out = jnp.where(lane == 0, i1.astype(F32), 0.0)
        out = jnp.where(lane == 1, i2.astype(F32), out)
        out = jnp.where(lane == 2, rank1, out)
        out = jnp.where(lane == 3, rank2, out)
        out = jnp.where(lane == 4, w1, out)
        out = jnp.where(lane == 5, w2, out)
        route_ref[rows, :] = out
        meta_ref[:, rows] = out.T[0:META_ROWS, :].astype(jnp.int32)

    subs = range(TM_MIX // TM_SUB)
    branches = [branch_projections(sub) for sub in subs]
    mixes = [out_projection(sub, *branches[sub]) for sub in subs]
    logits = [norm_and_logits(sub, mixes[sub]) for sub in subs]
    for sub in subs:
        route(sub, logits[sub])
    cnt_ref[...] = carry_ref[...]


def _mix(c, g, sga, sgb, x2, wco, bco, wgl, wo, bo, l1g, l1b, wrh, wrl, br, tri):
    tm = TM_MIX
    row = lambda i: (i, 0)
    fixed = lambda i: (0, 0)
    mat = pl.BlockSpec((D_MODEL, D_MODEL), fixed, pipeline_mode=pl.Buffered(1))
    vec = pl.BlockSpec((1, D_MODEL), fixed)
    return pl.pallas_call(
        _mix_kernel,
        grid=(N_TOK // tm,),
        in_specs=[
            pl.BlockSpec((tm, D_MODEL), row), pl.BlockSpec((tm, D_MODEL), row),
            pl.BlockSpec((tm, D_MODEL), row), pl.BlockSpec((tm, D_MODEL), row),
            pl.BlockSpec((tm, D_MODEL), row),
            mat, vec, mat, mat, vec, vec, vec,
            pl.BlockSpec((D_MODEL, LANE), fixed), pl.BlockSpec((D_MODEL, LANE), fixed),
            pl.BlockSpec((1, LANE), fixed),
            pl.BlockSpec((TM_SUB, TM_SUB), fixed),
        ],
        out_specs=[
            pl.BlockSpec((tm * TOKEN_ROWS, LANE), row),
            pl.BlockSpec((tm, LANE), row),
            pl.BlockSpec((META_ROWS, tm), lambda i: (0, i)),
            pl.BlockSpec((1, LANE), fixed),
        ],
        out_shape=[
            jax.ShapeDtypeStruct((N_TOK * TOKEN_ROWS, LANE), F32),
            jax.ShapeDtypeStruct((N_TOK, LANE), F32),
            jax.ShapeDtypeStruct((META_ROWS, N_TOK), jnp.int32),
            jax.ShapeDtypeStruct((1, LANE), F32),
        ],
        scratch_shapes=[pltpu.VMEM((1, LANE), F32)],
        compiler_params=pltpu.CompilerParams(
            dimension_semantics=("arbitrary",), vmem_limit_bytes=VMEM_LIMIT),
        name="mix",
    )(c, g, sga, sgb, x2, wco, bco, wgl, wo, bo, l1g, l1b, wrh, wrl, br, tri)


def _dispatch_kernel(cnt_ref, e0_ref, e1_ref, r0_ref, r1_ref, x1_ref,
                     xs_ref, d0_ref, d1_ref, ord_ref, oe_ref, nums_ref,
                     pstart_ref, zero_ref, dvm_ref, sem, zsem, dsem):
    i = pl.program_id(0)

    def zero_slots(first_slot, n_slots):
        rows = n_slots * TOKEN_ROWS
        dst = pl.ds(pl.multiple_of(first_slot * TOKEN_ROWS, TOKEN_ROWS), rows)
        return pltpu.make_async_copy(zero_ref.at[pl.ds(0, rows)], xs_ref.at[dst], zsem)

    def pad_copies(e, visit):
        pad = (-cnt_ref[e]) & (BM - 1)
        slot = pstart_ref[e] + cnt_ref[e]
        for bit in [1 << p for p in reversed(range(BM.bit_length() - 1))]:
            has = (pad & bit) != 0

            @pl.when(has)
            def _():
                visit(zero_slots(slot, bit))

            slot = slot + (pad & bit)

    @pl.when(i == 0)
    def _():
        zero_ref[...] = jnp.zeros_like(zero_ref)

        def plan(e, carry):
            bstart, n_ord = carry
            nb = (cnt_ref[e] + (BM - 1)) // BM
            pstart_ref[e] = bstart * BM

            def fill(j, c):
                ord_ref[bstart + j] = n_ord
                return c

            lax.fori_loop(0, nb, fill, 0)

            @pl.when(nb > 0)
            def _():
                oe_ref[n_ord] = e

            pad_copies(e, lambda cp: cp.start())

            return bstart + nb, n_ord + jnp.where(nb > 0, 1, 0)

        n_used, n_ord = lax.fori_loop(0, N_EXPERTS, plan, (jnp.int32(0), jnp.int32(0)))
        nums_ref[0] = n_used
        nums_ref[1] = n_ord

        def fill_rest(j, c):
            ord_ref[j] = n_ord - 1
            zero_slots(j * BM, BM).start()
            return c

        lax.fori_loop(n_used, N_BLOCKS, fill_rest, 0)

        def fill_oe(j, c):
            oe_ref[j] = 0
            return c

        lax.fori_loop(n_ord, N_EXPERTS, fill_oe, 0)

        for e_ref, r_ref, d_ref, k in ((e0_ref, r0_ref, d0_ref, 0), (e1_ref, r1_ref, d1_ref, 1)):
            ev = e_ref[...]
            dv = r_ref[...]
            for e in range(N_EXPERTS):
                dv = dv + jnp.where(ev == e, pstart_ref[e], 0)
            dvm_ref[k] = dv
            pltpu.make_async_copy(dvm_ref.at[k], d_ref, dsem).start()
        for d_ref, k in ((d0_ref, 0), (d1_ref, 1)):
            pltpu.make_async_copy(dvm_ref.at[k], d_ref, dsem).wait()

    base = i * TD_DISP

    def tok_group(tg, carry):
        drow = (base >> LANE_BITS) + tg
        for uu in range(DISP_UNROLL):
            tt = tg * DISP_UNROLL + uu
            src = x1_ref.at[pl.ds(pl.multiple_of(tt * TOKEN_ROWS, TOKEN_ROWS), TOKEN_ROWS)]
            for kk, d_ref in enumerate((d0_ref, d1_ref)):
                d = d_ref[drow, uu]
                pltpu.make_async_copy(
                    src, xs_ref.at[pl.ds(pl.multiple_of(d * TOKEN_ROWS, TOKEN_ROWS), TOKEN_ROWS)],
                    sem).start(priority=kk)
        return carry

    lax.fori_loop(0, TD_DISP // DISP_UNROLL, tok_group, 0)
    for kk in range(TOP_K):
        pltpu.make_async_copy(x1_ref, xs_ref.at[pl.ds(0, TD_DISP * TOKEN_ROWS)], sem).wait()

    @pl.when(i == pl.num_programs(0) - 1)
    def _():
        def drain_pad(e, c):
            pad_copies(e, lambda cp: cp.wait())
            return c

        lax.fori_loop(0, N_EXPERTS, drain_pad, 0)

        def drain_rest(j, c):
            zero_slots(0, BM).wait()
            return c

        lax.fori_loop(nums_ref[0], N_BLOCKS, drain_rest, 0)


def _dispatch(e0, e1, r0, r1, cnt, x1):
    smem = pl.BlockSpec(memory_space=pltpu.SMEM)
    idx = pl.BlockSpec((N_TOK // LANE, LANE), lambda i, c: (0, 0))
    grid_spec = pltpu.PrefetchScalarGridSpec(
        num_scalar_prefetch=1,
        grid=(N_TOK // TD_DISP,),
        in_specs=[idx, idx, idx, idx,
                  pl.BlockSpec((TD_DISP * TOKEN_ROWS, LANE), lambda i, c: (i, 0))],
        out_specs=[pl.BlockSpec(memory_space=pl.ANY), smem, smem, smem, smem, smem],
        scratch_shapes=[
            pltpu.SMEM((N_EXPERTS,), jnp.int32),
            pltpu.VMEM((BM * TOKEN_ROWS, LANE), F32),
            pltpu.VMEM((TOP_K, N_TOK // LANE, LANE), jnp.int32),
            pltpu.SemaphoreType.DMA,
            pltpu.SemaphoreType.DMA,
            pltpu.SemaphoreType.DMA,
        ],
    )
    return pl.pallas_call(
        _dispatch_kernel,
        grid_spec=grid_spec,
        out_shape=[
            jax.ShapeDtypeStruct((N_SLOTS * TOKEN_ROWS, LANE), F32),
            jax.ShapeDtypeStruct((N_TOK // LANE, LANE), jnp.int32),
            jax.ShapeDtypeStruct((N_TOK // LANE, LANE), jnp.int32),
            jax.ShapeDtypeStruct((N_BLOCKS,), jnp.int32),
            jax.ShapeDtypeStruct((N_EXPERTS,), jnp.int32),
            jax.ShapeDtypeStruct((2,), jnp.int32),
        ],
        compiler_params=pltpu.CompilerParams(
            dimension_semantics=("arbitrary",), vmem_limit_bytes=VMEM_LIMIT),
        name="dispatch",
    )(cnt, e0, e1, r0, r1, x1)


def _expert_kernel(ord_ref, oe_ref, nums_ref, xs_ref, w1_hbm, w3_hbm, w2_hbm, ys_ref,
                   w1f_ref, w3f_ref, w2f_ref, w1b_ref, w3b_ref, w2b_ref, sem):
    i = pl.program_id(0)
    n_used = nums_ref[0]
    n_ord = nums_ref[1]
    active = i < n_used
    k = ord_ref[i]
    first = (i == 0) | (k != ord_ref[jnp.maximum(i - 1, 0)])

    def weight_copies(kk):
        e = oe_ref[kk]
        slot = kk % W_SLOTS
        return [pltpu.make_async_copy(w_hbm.at[e], wf_ref.at[slot], sem.at[slot])
                for w_hbm, wf_ref in ((w1_hbm, w1f_ref), (w3_hbm, w3f_ref), (w2_hbm, w2f_ref))]

    def start_weights(kk):
        @pl.when(kk < n_ord)
        def _():
            for cp in weight_copies(kk):
                cp.start(priority=1)

    @pl.when(active & first)
    def _():
        @pl.when(i == 0)
        def _():
            for ahead in range(W_SLOTS - 1):
                start_weights(k + ahead)

        start_weights(k + W_SLOTS - 1)
        for cp in weight_copies(k):
            cp.wait()
        slot = k % W_SLOTS
        w1b_ref[...] = w1f_ref[slot].astype(BF16)
        w3b_ref[...] = w3f_ref[slot].astype(BF16)
        w2b_ref[...] = w2f_ref[slot].astype(BF16)

    @pl.when(active)
    def _():
        xb = _from_token_tiles(xs_ref, BM).astype(BF16)
        h1 = jnp.dot(xb, w1b_ref[...], preferred_element_type=F32)
        h3 = jnp.dot(xb, w3b_ref[...], preferred_element_type=F32)
        hdn = (h1 * _sigmoid(h1) * h3).astype(BF16)
        _to_token_tiles(ys_ref, jnp.dot(hdn, w2b_ref[...], preferred_element_type=F32))

    @pl.when(jnp.logical_not(active))
    def _():
        ys_ref[...] = jnp.zeros_like(ys_ref)


def _experts(blk_ord, ord_e, nums, xs, w1, w3, w2):
    last = lambda i, nums_ref: jnp.maximum(jnp.minimum(i, nums_ref[0] - 1), 0)
    hbm = pl.BlockSpec(memory_space=pl.ANY)
    grid_spec = pltpu.PrefetchScalarGridSpec(
        num_scalar_prefetch=3,
        grid=(N_BLOCKS,),
        in_specs=[
            pl.BlockSpec((BM * TOKEN_ROWS, LANE), lambda i, o, oe, nu: (last(i, nu), 0)),
            hbm, hbm, hbm,
        ],
        out_specs=pl.BlockSpec((BM * TOKEN_ROWS, LANE), lambda i, o, oe, nu: (i, 0)),
        scratch_shapes=[
            pltpu.VMEM((W_SLOTS, D_MODEL, D_EXPERT), F32),
            pltpu.VMEM((W_SLOTS, D_MODEL, D_EXPERT), F32),
            pltpu.VMEM((W_SLOTS, D_EXPERT, D_MODEL), F32),
            pltpu.VMEM((D_MODEL, D_EXPERT), BF16),
            pltpu.VMEM((D_MODEL, D_EXPERT), BF16),
            pltpu.VMEM((D_EXPERT, D_MODEL), BF16),
            pltpu.SemaphoreType.DMA((W_SLOTS,)),
        ],
    )
    return pl.pallas_call(
        _expert_kernel,
        grid_spec=grid_spec,
        out_shape=jax.ShapeDtypeStruct((N_SLOTS * TOKEN_ROWS, LANE), F32),
        compiler_params=pltpu.CompilerParams(
            dimension_semantics=("arbitrary",), vmem_limit_bytes=VMEM_LIMIT),
        name="experts",
    )(blk_ord, ord_e, nums, xs, w1, w3, w2)


def _combine_kernel(d0_ref, d1_ref, route_ref, x1_ref, ys_ref, g_ref, b_ref, o_ref, y_ref, sem):
    i = pl.program_id(0)
    n_tiles = pl.num_programs(0)
    tile_rows = TC_COMB * TOKEN_ROWS

    def issue(tile, slot, t0, n):
        drow = (tile * TC_COMB + t0) >> LANE_BITS
        for uu in range(n):
            t = t0 + uu
            dst = pl.ds(pl.multiple_of(t * TOKEN_ROWS, TOKEN_ROWS), TOKEN_ROWS)
            for kk, d_ref in enumerate((d0_ref, d1_ref)):
                d = d_ref[drow + uu // LANE, uu % LANE]
                src = pl.ds(pl.multiple_of(d * TOKEN_ROWS, TOKEN_ROWS), TOKEN_ROWS)
                pltpu.make_async_copy(ys_ref.at[src], y_ref.at[slot, kk, dst],
                                      sem.at[slot]).start(priority=kk)

    def finish(slot, t0):
        t0 = pl.multiple_of(t0, COMB_ROWS)
        w0 = route_ref[pl.ds(t0, COMB_ROWS), 4:5]
        w1 = route_ref[pl.ds(t0, COMB_ROWS), 5:6]
        z = (DEEPNORM_ALPHA * _from_token_tiles(x1_ref, COMB_ROWS, t0)
             + w0 * _from_token_tiles(y_ref.at[slot, 0], COMB_ROWS, t0)
             + w1 * _from_token_tiles(y_ref.at[slot, 1], COMB_ROWS, t0))
        o_ref[pl.ds(t0, COMB_ROWS), :] = _layer_norm(z, g_ref[...], b_ref[...])

    @pl.when(i == 0)
    def _():
        def first(gi, carry):
            issue(0, 0, gi * COMB_ROWS, COMB_ROWS)
            return carry

        lax.fori_loop(0, TC_COMB // COMB_ROWS, first, 0)

    slot = i % 2
    for kk in range(TOP_K):
        pltpu.make_async_copy(ys_ref.at[pl.ds(0, tile_rows)], y_ref.at[slot, kk],
                              sem.at[slot]).wait()

    @pl.when(i + 1 < n_tiles)
    def _():
        def both(gi, carry):
            issue(i + 1, 1 - slot, gi * COMB_ROWS, COMB_ROWS)
            finish(slot, gi * COMB_ROWS)
            return carry

        lax.fori_loop(0, TC_COMB // COMB_ROWS, both, 0)

    @pl.when(i + 1 == n_tiles)
    def _():
        def last(gi, carry):
            finish(slot, gi * COMB_ROWS)
            return carry

        lax.fori_loop(0, TC_COMB // COMB_ROWS, last, 0)


def _combine(d0, d1, route, x1, ys, g, b):
    tc = TC_COMB
    grid_spec = pltpu.PrefetchScalarGridSpec(
        num_scalar_prefetch=2,
        grid=(N_TOK // tc,),
        in_specs=[
            pl.BlockSpec((tc, LANE), lambda i, *_: (i, 0)),
            pl.BlockSpec((tc * TOKEN_ROWS, LANE), lambda i, *_: (i, 0)),
            pl.BlockSpec(memory_space=pl.ANY),
            pl.BlockSpec((1, D_MODEL), lambda i, *_: (0, 0)),
            pl.BlockSpec((1, D_MODEL), lambda i, *_: (0, 0)),
        ],
        out_specs=pl.BlockSpec((tc, D_MODEL), lambda i, *_: (i, 0)),
        scratch_shapes=[
            pltpu.VMEM((2, TOP_K, tc * TOKEN_ROWS, LANE), F32),
            pltpu.SemaphoreType.DMA((2,)),
        ],
    )
    return pl.pallas_call(
        _combine_kernel,
        grid_spec=grid_spec,
        out_shape=jax.ShapeDtypeStruct((N_TOK, D_MODEL), F32),
        compiler_params=pltpu.CompilerParams(
            dimension_semantics=("arbitrary",), vmem_limit_bytes=VMEM_LIMIT),
        name="combine",
    )(d0, d1, route, x1, ys, g, b)


def _split_bf16(w):
    hi = w.astype(BF16)
    lo = (w - hi.astype(F32)).astype(BF16)
    return hi, lo


def kernel(x, w_in, b_in, conv_w, conv_b, conv_ln_g, conv_ln_b, w_conv_out, b_conv_out, w_gate_up, b_gate_up, gla_norm_g, w_gla_out, w_out, b_out, ln1_g, ln1_b, w_router_group, b_router_group, w_router_expert, b_router_expert, w1, w3, w2, ln2_g, ln2_b):
    x2 = x.reshape(N_TOK, D_MODEL)
    row = lambda v: v.reshape(1, -1)
    for l in range(w_in.shape[0]):
        f0 = N_PROJ_A
        f1 = f0 + GATE_RANK
        w_l, b_l = w_in[l], b_in[l]
        w_t = jnp.swapaxes(w_l, 0, 1)
        w_a = w_t[:f0].astype(BF16)
        w_b = w_t[f1:].astype(BF16)
        w_f = jnp.pad(w_t[f0:f1], ((0, F_PAD - GATE_RANK), (0, 0))).astype(BF16)
        b_p = row(jnp.concatenate(
            [b_l[:f0], b_l[f1:], b_l[f0:f1], jnp.zeros((F_PAD - GATE_RANK,), F32)]))
        wgu_p = jnp.concatenate(
            [w_gate_up[l], jnp.zeros((F_PAD - GATE_RANK, GLA_DK), F32)], axis=0).astype(BF16)
        cw_p = jnp.repeat(conv_w[l], SUBLANES, axis=0)
        c, q, k, v, rs, sga, sgb, la = _inproj(
            x2, w_a, w_b, w_f, b_p, wgu_p, row(b_gate_up[l]), cw_p,
            row(conv_b[l]), row(conv_ln_g[l]), row(conv_ln_b[l]))
        g = _gla(q, k, v, la, rs, row(gla_norm_g[l]))

        w_r = jnp.concatenate(
            [w_router_expert[l], w_router_group[l],
             jnp.zeros((D_MODEL, LANE - N_EXPERTS - N_GROUPS), F32)], axis=1)
        b_r = row(jnp.concatenate(
            [b_router_expert[l], b_router_group[l],
             jnp.zeros((LANE - N_EXPERTS - N_GROUPS,), F32)]))
        wrh, wrl = _split_bf16(w_r)
        ri = lax.broadcasted_iota(jnp.int32, (TM_SUB, TM_SUB), 0)
        ci = lax.broadcasted_iota(jnp.int32, (TM_SUB, TM_SUB), 1)
        tri = (ri > ci).astype(BF16)
        x1, route, meta, cnt = _mix(
            c, g, sga, sgb, x2, w_conv_out[l].astype(BF16), row(b_conv_out[l]),
            w_gla_out[l].astype(BF16), w_out[l].astype(BF16), row(b_out[l]),
            row(ln1_g[l]), row(ln1_b[l]), wrh, wrl, b_r, tri)

        cnt_i = cnt[0, :N_EXPERTS].astype(jnp.int32)
        tok_grid = lambda m: m.reshape(N_TOK // LANE, LANE)
        xs, d0, d1, blk_ord, ord_e, nums = _dispatch(
            tok_grid(meta[0]), tok_grid(meta[1]), tok_grid(meta[2]), tok_grid(meta[3]), cnt_i, x1)
        ys = _experts(blk_ord, ord_e, nums, xs, w1[l], w3[l], w2[l])
        x2 = _combine(d0, d1, route, x1, ys, row(ln2_g[l]), row(ln2_b[l]))
    return x2.reshape(x.shape)
```

```python
import jax
import jax.numpy as jnp
from jax import lax
from jax.experimental import pallas as pl
from jax.experimental.pallas import tpu as pltpu

F32 = jnp.float32
BF16 = jnp.bfloat16

D_MODEL = 1024
BATCH = 8
SEQ = 2048
N_TOK = BATCH * SEQ
CHUNK = 64
CONV_WIDTH = 31
GLA_HEADS = 4
GLA_DK = 512
GLA_DV = 1024
GLA_HK = 128
GLA_HV = 256
GATE_RANK = 16
GATE_TAU = 16.0
N_GROUPS = 8
EXPERTS_PER_GROUP = 8
GROUP_BITS = EXPERTS_PER_GROUP.bit_length() - 1
N_EXPERTS = 64
TOP_K = 2
D_EXPERT = 512
LN_EPS = 1e-5
RMS_EPS = 1e-6
DEEPNORM_ALPHA = 2.0 ** 0.25

LANE = 128
LANE_BITS = LANE.bit_length() - 1
TOKEN_ROWS = D_MODEL // LANE
F_PAD = LANE
N_PROJ_A = 2 * D_MODEL + 2 * GLA_DK + 2 * GLA_DV
N_PROJ = N_PROJ_A + 2 * D_MODEL + F_PAD
TM_PROJ = 512
PROJ_PIECE = 256
CONV_HALO = 32
CONV_ROWS = 32
SUBLANES = 8
CONV_SUB = 128
CONV_SHROWS = CONV_SUB + CONV_HALO
CONV_WIN = CONV_SHROWS + 16
TS_GLA = 256
NB_GLA = 4
TM_MIX = 1024
TM_SUB = 256
META_ROWS = 8
BM = 256
W_SLOTS = 3
N_BLOCKS = N_TOK * TOP_K // BM + N_EXPERTS
N_SLOTS = N_BLOCKS * BM
TC_COMB = 1024
COMB_ROWS = 128
TD_DISP = 2048
DISP_UNROLL = LANE
VMEM_LIMIT = 56 * 1024 * 1024


def _sigmoid(x):
    return 1.0 / (1.0 + jnp.exp(-x))


def _to_token_tiles(ref, val, t0=0):
    n = val.shape[0]
    for c in range(TOKEN_ROWS):
        ref[pl.ds(t0 * TOKEN_ROWS + c, n, stride=TOKEN_ROWS), :] = val[:, c * LANE:(c + 1) * LANE]


def _from_token_tiles(ref, n, t0=0):
    return jnp.concatenate(
        [ref[pl.ds(t0 * TOKEN_ROWS + c, n, stride=TOKEN_ROWS), :] for c in range(TOKEN_ROWS)],
        axis=1)


def _layer_norm(z, g, b):
    mu = jnp.mean(z, axis=-1, keepdims=True)
    zc = z - mu
    var = jnp.mean(zc * zc, axis=-1, keepdims=True)
    return zc * lax.rsqrt(var + LN_EPS) * g + b


def _conv_shift(ext_ref, shf_ref, base):
    rows = CONV_WIN
    win = ext_ref[base:base + rows, :].astype(F32)
    for b in range(SUBLANES):
        sh = win if b == 0 else pltpu.roll(win, rows - b, axis=0)
        shf_ref[b * CONV_SHROWS:(b + 1) * CONV_SHROWS, :] = sh[0:CONV_SHROWS, :]


def _conv_rows(shf_ref, cw_ref, cb_ref, g_ref, b_ref, c_ref, base, r0):
    off = CONV_HALO - (CONV_WIDTH - 1)
    n_sub = CONV_ROWS // SUBLANES
    acc = [jnp.zeros((SUBLANES, D_MODEL), F32) for _ in range(n_sub)]
    for j in range(CONV_WIDTH):
        l0 = r0 + off + j
        b = l0 % SUBLANES
        m0 = b * CONV_SHROWS + (l0 - b)
        wj = cw_ref[j * SUBLANES:(j + 1) * SUBLANES, :]
        for k in range(n_sub):
            rk = m0 + k * SUBLANES
            acc[k] = acc[k] + shf_ref[rk:rk + SUBLANES, :] * wj
    y = _layer_norm(jnp.concatenate(acc, axis=0) + cb_ref[...], g_ref[...], b_ref[...])
    c_ref[base + r0:base + r0 + CONV_ROWS, :] = (y * _sigmoid(y)).astype(BF16)


def _inproj_kernel(x_ref, wa_ref, wb_ref, wf_ref, b_ref, wgu_ref, bgu_ref,
                   cw_ref, cb_ref, cg_ref, cbeta_ref,
                   c_ref, q_ref, k_ref, v_ref, rs_ref, sga_ref, sgb_ref, la_ref,
                   ext_ref, shf_ref, xb_ref):
    i = pl.program_id(0)
    xb_ref[...] = x_ref[...].astype(BF16)
    half = D_MODEL // 2

    def seg(c0, n):
        if c0 < N_PROJ_A:
            w = wa_ref[c0:c0 + n, :]
        elif c0 < N_PROJ_A + 2 * D_MODEL:
            w = wb_ref[c0 - N_PROJ_A:c0 - N_PROJ_A + n, :]
        else:
            w = wf_ref[...]
        h = lax.dot_general(xb_ref[...], w, (((1,), (1,)), ((), ())),
                            preferred_element_type=F32)
        return h + b_ref[:, c0:c0 + n]

    @pl.when(i == 0)
    def _():
        ext_ref[...] = jnp.zeros_like(ext_ref)

    prev_tail = ext_ref[TM_PROJ:TM_PROJ + CONV_HALO, :]
    ext_ref[0:CONV_HALO, :] = jnp.where(
        i % (SEQ // TM_PROJ) == 0, jnp.zeros_like(prev_tail), prev_tail)
    for j in range(2):
        a = seg(j * half, half)
        g = seg(D_MODEL + j * half, half)
        ext_ref[CONV_HALO:CONV_HALO + TM_PROJ, j * half:(j + 1) * half] = (
            a * _sigmoid(g)).astype(BF16)

    def piece(ref, col, c0, fn):
        def run():
            ref[:, col:col + PROJ_PIECE] = fn(seg(c0 + col, PROJ_PIECE)).astype(BF16)
        return run

    mxu_pieces = []
    for ref, c0, width, fn in (
            (q_ref, 2 * D_MODEL, GLA_DK, lambda h: h * (GLA_HK ** -0.5)),
            (k_ref, 2 * D_MODEL + GLA_DK, GLA_DK, lambda h: h),
            (v_ref, 3 * D_MODEL, GLA_DV, lambda h: h),
            (rs_ref, 4 * D_MODEL, GLA_DV, lambda h: h * _sigmoid(h)),
            (sga_ref, 5 * D_MODEL, D_MODEL, _sigmoid),
            (sgb_ref, 6 * D_MODEL, D_MODEL, _sigmoid)):
        mxu_pieces += [piece(ref, col, c0, fn) for col in range(0, width, PROJ_PIECE)]

    def forget_piece():
        f = seg(7 * D_MODEL, F_PAD)
        z = jnp.dot(f.astype(BF16), wgu_ref[...], preferred_element_type=F32) + bgu_ref[...]
        la_ref[...] = ((jnp.minimum(z, 0.0) - jnp.log(1.0 + jnp.exp(-jnp.abs(z))))
                       * (1.0 / GATE_TAU))

    mxu_pieces.append(forget_piece)

    valu_pieces = []
    for sb in range(TM_PROJ // CONV_SUB):
        shf = shf_ref.at[sb % 2]
        for r0 in range(0, CONV_SUB, CONV_ROWS):
            def run(sb=sb, shf=shf, r0=r0):
                if r0 == 0:
                    _conv_shift(ext_ref, shf, sb * CONV_SUB)
                _conv_rows(shf, cw_ref, cb_ref, cg_ref, cbeta_ref, c_ref, sb * CONV_SUB, r0)
            valu_pieces.append(run)

    n_v, n_m = len(valu_pieces), len(mxu_pieces)
    done = 0
    for vi, vrun in enumerate(valu_pieces):
        vrun()
        upto = (vi + 1) * n_m // n_v
        for mrun in mxu_pieces[done:upto]:
            mrun()
        done = upto


def _inproj(x2, w_a, w_b, w_f, b_p, wgu_p, bgu, cw_p, cb, cg, cbeta):
    tm = TM_PROJ
    row = lambda i: (i, 0)
    fixed = lambda i: (0, 0)
    tok = lambda n, dt: jax.ShapeDtypeStruct((N_TOK, n), dt)
    vec = pl.BlockSpec((1, D_MODEL), fixed)
    return pl.pallas_call(
        _inproj_kernel,
        grid=(N_TOK // tm,),
        in_specs=[
            pl.BlockSpec((tm, D_MODEL), row),
            pl.BlockSpec((N_PROJ_A, D_MODEL), fixed, pipeline_mode=pl.Buffered(1)),
            pl.BlockSpec((2 * D_MODEL, D_MODEL), fixed, pipeline_mode=pl.Buffered(1)),
            pl.BlockSpec((F_PAD, D_MODEL), fixed),
            pl.BlockSpec((1, N_PROJ), fixed),
            pl.BlockSpec((F_PAD, GLA_DK), fixed),
            pl.BlockSpec((1, GLA_DK), fixed),
            pl.BlockSpec((CONV_WIDTH * SUBLANES, D_MODEL), fixed),
            vec, vec, vec,
        ],
        out_specs=[
            pl.BlockSpec((tm, D_MODEL), row),
            pl.BlockSpec((tm, GLA_DK), row),
            pl.BlockSpec((tm, GLA_DK), row),
            pl.BlockSpec((tm, GLA_DV), row),
            pl.BlockSpec((tm, GLA_DV), row),
            pl.BlockSpec((tm, D_MODEL), row),
            pl.BlockSpec((tm, D_MODEL), row),
            pl.BlockSpec((tm, GLA_DK), row),
        ],
        out_shape=[
            tok(D_MODEL, BF16), tok(GLA_DK, BF16), tok(GLA_DK, BF16), tok(GLA_DV, BF16),
            tok(GLA_DV, BF16), tok(D_MODEL, BF16), tok(D_MODEL, BF16), tok(GLA_DK, F32),
        ],
        scratch_shapes=[
            pltpu.VMEM((tm + CONV_WIN - CONV_SUB, D_MODEL), BF16),
            pltpu.VMEM((2, SUBLANES * CONV_SHROWS, D_MODEL), F32),
            pltpu.VMEM((tm, D_MODEL), BF16),
        ],
        compiler_params=pltpu.CompilerParams(
            dimension_semantics=("arbitrary",), vmem_limit_bytes=VMEM_LIMIT),
        name="inproj",
    )(x2, w_a, w_b, w_f, b_p, wgu_p, bgu, cw_p, cb, cg, cbeta)


def _gla_kernel(q_ref, k_ref, v_ref, la_ref, rs_ref, ng_ref, o_ref,
                st_ref, kv_ref, sb_ref, dec_ref):
    s = pl.program_id(1)

    @pl.when(s == 0)
    def _():
        st_ref[...] = jnp.zeros_like(st_ref)

    rr = lax.broadcasted_iota(jnp.int32, (CHUNK, CHUNK), 0)
    cc = lax.broadcasted_iota(jnp.int32, (CHUNK, CHUNK), 1)
    tri = jnp.where(rr >= cc, 1.0, 0.0).astype(BF16)
    n_chunks = TS_GLA // CHUNK
    heads = [(slice(h * GLA_HK, (h + 1) * GLA_HK), slice(h * GLA_HV, (h + 1) * GLA_HV))
             for h in range(GLA_HEADS)]

    for nb in range(NB_GLA):
        for c in range(n_chunks):
            rows = slice(c * CHUNK, (c + 1) * CHUNK)
            la = la_ref[nb, rows, :]
            hi = la.astype(BF16)
            lo = (la - hi.astype(F32)).astype(BF16)
            cum = (jnp.dot(tri, hi, preferred_element_type=F32)
                   + jnp.dot(tri, lo, preferred_element_type=F32))
            cend = cum[CHUNK - 1:CHUNK, :]
            kd = (k_ref[nb, rows, :].astype(F32) * jnp.exp(cend - cum)).astype(BF16)
            idx = nb * n_chunks + c
            dec_ref[idx * SUBLANES:(idx + 1) * SUBLANES, :] = jnp.broadcast_to(
                jnp.exp(cend), (SUBLANES, GLA_DK))
            vc = v_ref[nb, rows, :]
            for h, (ks, vs) in enumerate(heads):
                kv_ref[idx * GLA_HEADS + h] = lax.dot_general(
                    vc[:, vs], kd[:, ks], (((0,), (0,)), ((), ())), preferred_element_type=F32)

    for nb in range(NB_GLA):
        for h, (ks, vs) in enumerate(heads):
            st = st_ref[nb * GLA_HEADS + h]
            for c in range(n_chunks):
                idx = nb * n_chunks + c
                st = st * dec_ref[idx * SUBLANES:idx * SUBLANES + 1, ks] + kv_ref[
                    idx * GLA_HEADS + h]
                sb_ref[idx * GLA_HEADS + h] = st.astype(BF16)
            st_ref[nb * GLA_HEADS + h] = st

    for nb in range(NB_GLA):
        for c in range(n_chunks):
            rows = slice(c * CHUNK, (c + 1) * CHUNK)
            idx = nb * n_chunks + c
            qc = q_ref[nb, rows, :]
            for h, (ks, vs) in enumerate(heads):
                o = lax.dot_general(qc[:, ks], sb_ref[idx * GLA_HEADS + h],
                                    (((1,), (1,)), ((), ())), preferred_element_type=F32)
                ms = jnp.mean(o * o, axis=-1, keepdims=True)
                on = o * lax.rsqrt(ms + RMS_EPS) * ng_ref[:, vs]
                o_ref[nb, rows, vs] = (on * rs_ref[nb, rows, vs].astype(F32)).astype(BF16)


def _gla(q, k, v, la, rs, ng):
    ts = TS_GLA
    blk = lambda n: pl.BlockSpec((NB_GLA, ts, n), lambda bi, si: (bi, si, 0))
    seq = lambda a: a.reshape(BATCH, SEQ, a.shape[-1])
    out = pl.pallas_call(
        _gla_kernel,
        grid=(BATCH // NB_GLA, SEQ // ts),
        in_specs=[blk(GLA_DK), blk(GLA_DK), blk(GLA_DV), blk(GLA_DK), blk(GLA_DV),
                  pl.BlockSpec((1, GLA_DV), lambda bi, si: (0, 0))],
        out_specs=blk(GLA_DV),
        out_shape=jax.ShapeDtypeStruct((BATCH, SEQ, GLA_DV), BF16),
        scratch_shapes=[
            pltpu.VMEM((NB_GLA * GLA_HEADS, GLA_HV, GLA_HK), F32),
            pltpu.VMEM((NB_GLA * (ts // CHUNK) * GLA_HEADS, GLA_HV, GLA_HK), F32),
            pltpu.VMEM((NB_GLA * (ts // CHUNK) * GLA_HEADS, GLA_HV, GLA_HK), BF16),
            pltpu.VMEM((NB_GLA * (ts // CHUNK) * SUBLANES, GLA_DK), F32),
        ],
        compiler_params=pltpu.CompilerParams(
            dimension_semantics=("arbitrary", "arbitrary"), vmem_limit_bytes=VMEM_LIMIT),
        name="gla",
    )(seq(q), seq(k), seq(v), seq(la), seq(rs), ng)
    return out.reshape(N_TOK, GLA_DV)


def _mix_kernel(c_ref, g_ref, sga_ref, sgb_ref, x_ref, wco_ref, bco_ref, wgl_ref, wo_ref, bo_ref,
                l1g_ref, l1b_ref, wrh_ref, wrl_ref, br_ref, tri_ref,
                x1_ref, route_ref, meta_ref, cnt_ref, carry_ref):
    i = pl.program_id(0)

    @pl.when(i == 0)
    def _():
        carry_ref[...] = jnp.zeros_like(carry_ref)

    def rows_of(sub):
        return slice(sub * TM_SUB, (sub + 1) * TM_SUB)

    def branch_projections(sub):
        rows = rows_of(sub)
        yc = jnp.dot(c_ref[rows, :], wco_ref[...], preferred_element_type=F32) + bco_ref[...]
        yg = jnp.dot(g_ref[rows, :], wgl_ref[...], preferred_element_type=F32)
        return yc, yg

    def out_projection(sub, yc, yg):
        rows = rows_of(sub)
        merged = sga_ref[rows, :].astype(F32) * yc + sgb_ref[rows, :].astype(F32) * yg
        return jnp.dot(merged.astype(BF16), wo_ref[...], preferred_element_type=F32) + bo_ref[...]

    def norm_and_logits(sub, mix):
        x1 = _layer_norm(DEEPNORM_ALPHA * x_ref[rows_of(sub), :] + mix, l1g_ref[...], l1b_ref[...])
        _to_token_tiles(x1_ref, x1, sub * TM_SUB)
        xh = x1.astype(BF16)
        xl = (x1 - xh.astype(F32)).astype(BF16)
        return (jnp.dot(xh, wrh_ref[...], preferred_element_type=F32)
                + jnp.dot(xl, wrh_ref[...], preferred_element_type=F32)
                + jnp.dot(xh, wrl_ref[...], preferred_element_type=F32)) + br_ref[...]

    def route(sub, lg):
        rows = rows_of(sub)
        lane = lax.broadcasted_iota(jnp.int32, (TM_SUB, LANE), 1)
        neg = -jnp.inf
        big = jnp.int32(1 << 20)
        is_g = (lane >= N_EXPERTS) & (lane < N_EXPERTS + N_GROUPS)
        gl = jnp.where(is_g, lg, neg)
        gmax = jnp.max(gl, axis=-1, keepdims=True)
        gsel = jnp.min(jnp.where(gl == gmax, lane, big), axis=-1, keepdims=True) - N_EXPERTS
        gw = 1.0 / jnp.sum(jnp.exp(gl - gmax), axis=-1, keepdims=True)
        in_grp = (lane < N_EXPERTS) & ((lane >> GROUP_BITS) == gsel)
        el = jnp.where(in_grp, lg, neg)
        m1 = jnp.max(el, axis=-1, keepdims=True)
        i1 = jnp.min(jnp.where(el == m1, lane, big), axis=-1, keepdims=True)
        el2 = jnp.where(lane == i1, neg, el)
        m2 = jnp.max(el2, axis=-1, keepdims=True)
        i2 = jnp.min(jnp.where(el2 == m2, lane, big), axis=-1, keepdims=True)
        t = jnp.exp(m2 - m1)
        w1 = gw / (1.0 + t)
        w2 = gw * t / (1.0 + t)

        oh1 = lane == i1
        oh2 = lane == i2
        osum = jnp.where(oh1 | oh2, 1.0, 0.0)
        excl = (jnp.dot(tri_ref[...], osum.astype(BF16), preferred_element_type=F32)
                + carry_ref[...])
        rank1 = jnp.sum(jnp.where(oh1, excl, 0.0), axis=-1, keepdims=True)
        rank2 = jnp.sum(jnp.where(oh2, excl, 0.0), axis=-1, keepdims=True)
        carry_ref[...] = carry_ref[...] + jnp.sum(osum, axis=0, keepdims=True)

        out = jnp.where(lane == 0, i1.astype(F32), 0.0)
        out = jnp.where(lane == 1, i2.astype(F32), out)
        out = jnp.where(lane == 2, rank1, out)
        out = jnp.where(lane == 3, rank2, out)
        out = jnp.where(lane == 4, w1, out)
        out = jnp.where(lane == 5, w2, out)
        route_ref[rows, :] = out
        meta_ref[:, rows] = out.T[0:META_ROWS, :].astype(jnp.int32)

    subs = range(TM_MIX // TM_SUB)
    branches = [branch_projections(sub) for sub in subs]
    mixes = [out_projection(sub, *branches[sub]) for sub in subs]
    logits = [norm_and_logits(sub, mixes[sub]) for sub in subs]
    for sub in subs:
        route(sub, logits[sub])
    cnt_ref[...] = carry_ref[...]


def _mix(c, g, sga, sgb, x2, wco, bco, wgl, wo, bo, l1g, l1b, wrh, wrl, br, tri):
    tm = TM_MIX
    row = lambda i: (i, 0)
    fixed = lambda i: (0, 0)
    mat = pl.BlockSpec((D_MODEL, D_MODEL), fixed, pipeline_mode=pl.Buffered(1))
    vec = pl.BlockSpec((1, D_MODEL), fixed)
    return pl.pallas_call(
        _mix_kernel,
        grid=(N_TOK // tm,),
        in_specs=[
            pl.BlockSpec((tm, D_MODEL), row), pl.BlockSpec((tm, D_MODEL), row),
            pl.BlockSpec((tm, D_MODEL), row), pl.BlockSpec((tm, D_MODEL), row),
            pl.BlockSpec((tm, D_MODEL), row),
            mat, vec, mat, mat, vec, vec, vec,
            pl.BlockSpec((D_MODEL, LANE), fixed), pl.BlockSpec((D_MODEL, LANE), fixed),
            pl.BlockSpec((1, LANE), fixed),
            pl.BlockSpec((TM_SUB, TM_SUB), fixed),
        ],
        out_specs=[
            pl.BlockSpec((tm * TOKEN_ROWS, LANE), row),
            pl.BlockSpec((tm, LANE), row),
            pl.BlockSpec((META_ROWS, tm), lambda i: (0, i)),
            pl.BlockSpec((1, LANE), fixed),
        ],
        out_shape=[
            jax.ShapeDtypeStruct((N_TOK * TOKEN_ROWS, LANE), F32),
            jax.ShapeDtypeStruct((N_TOK, LANE), F32),
            jax.ShapeDtypeStruct((META_ROWS, N_TOK), jnp.int32),
            jax.ShapeDtypeStruct((1, LANE), F32),
        ],
        scratch_shapes=[pltpu.VMEM((1, LANE), F32)],
        compiler_params=pltpu.CompilerParams(
            dimension_semantics=("arbitrary",), vmem_limit_bytes=VMEM_LIMIT),
        name="mix",
    )(c, g, sga, sgb, x2, wco, bco, wgl, wo, bo, l1g, l1b, wrh, wrl, br, tri)


def _dispatch_kernel(cnt_ref, e0_ref, e1_ref, r0_ref, r1_ref, x1_ref,
                     xs_ref, d0_ref, d1_ref, ord_ref, oe_ref, nums_ref,
                     pstart_ref, zero_ref, dvm_ref, sem, zsem, dsem):
    i = pl.program_id(0)

    def zero_slots(first_slot, n_slots):
        rows = n_slots * TOKEN_ROWS
        dst = pl.ds(pl.multiple_of(first_slot * TOKEN_ROWS, TOKEN_ROWS), rows)
        return pltpu.make_async_copy(zero_ref.at[pl.ds(0, rows)], xs_ref.at[dst], zsem)

    def pad_copies(e, visit):
        pad = (-cnt_ref[e]) & (BM - 1)
        slot = pstart_ref[e] + cnt_ref[e]
        for bit in [1 << p for p in reversed(range(BM.bit_length() - 1))]:
            has = (pad & bit) != 0

            @pl.when(has)
            def _():
                visit(zero_slots(slot, bit))

            slot = slot + (pad & bit)

    @pl.when(i == 0)
    def _():
        zero_ref[...] = jnp.zeros_like(zero_ref)

        def plan(e, carry):
            bstart, n_ord = carry
            nb = (cnt_ref[e] + (BM - 1)) // BM
            pstart_ref[e] = bstart * BM

            def fill(j, c):
                ord_ref[bstart + j] = n_ord
                return c

            lax.fori_loop(0, nb, fill, 0)

            @pl.when(nb > 0)
            def _():
                oe_ref[n_ord] = e

            pad_copies(e, lambda cp: cp.start())

            return bstart + nb, n_ord + jnp.where(nb > 0, 1, 0)

        n_used, n_ord = lax.fori_loop(0, N_EXPERTS, plan, (jnp.int32(0), jnp.int32(0)))
        nums_ref[0] = n_used
        nums_ref[1] = n_ord

        def fill_rest(j, c):
            ord_ref[j] = n_ord - 1
            zero_slots(j * BM, BM).start()
            return c

        lax.fori_loop(n_used, N_BLOCKS, fill_rest, 0)

        def fill_oe(j, c):
            oe_ref[j] = 0
            return c

        lax.fori_loop(n_ord, N_EXPERTS, fill_oe, 0)

        for e_ref, r_ref, d_ref, k in ((e0_ref, r0_ref, d0_ref, 0), (e1_ref, r1_ref, d1_ref, 1)):
            ev = e_ref[...]
            dv = r_ref[...]
            for e in range(N_EXPERTS):
                dv = dv + jnp.where(ev == e, pstart_ref[e], 0)
            dvm_ref[k] = dv
            pltpu.make_async_copy(dvm_ref.at[k], d_ref, dsem).start()
        for d_ref, k in ((d0_ref, 0), (d1_ref, 1)):
            pltpu.make_async_copy(dvm_ref.at[k], d_ref, dsem).wait()

    base = i * TD_DISP

    def tok_group(tg, carry):
        drow = (base >> LANE_BITS) + tg
        for uu in range(DISP_UNROLL):
            tt = tg * DISP_UNROLL + uu
            src = x1_ref.at[pl.ds(pl.multiple_of(tt * TOKEN_ROWS, TOKEN_ROWS), TOKEN_ROWS)]
            for kk, d_ref in enumerate((d0_ref, d1_ref)):
                d = d_ref[drow, uu]
                pltpu.make_async_copy(
                    src, xs_ref.at[pl.ds(pl.multiple_of(d * TOKEN_ROWS, TOKEN_ROWS), TOKEN_ROWS)],
                    sem).start(priority=kk)
        return carry

    lax.fori_loop(0, TD_DISP // DISP_UNROLL, tok_group, 0)
    for kk in range(TOP_K):
        pltpu.make_async_copy(x1_ref, xs_ref.at[pl.ds(0, TD_DISP * TOKEN_ROWS)], sem).wait()

    @pl.when(i == pl.num_programs(0) - 1)
    def _():
        def drain_pad(e, c):
            pad_copies(e, lambda cp: cp.wait())
            return c

        lax.fori_loop(0, N_EXPERTS, drain_pad, 0)

        def drain_rest(j, c):
            zero_slots(0, BM).wait()
            return c

        lax.fori_loop(nums_ref[0], N_BLOCKS, drain_rest, 0)


def _dispatch(e0, e1, r0, r1, cnt, x1):
    smem = pl.BlockSpec(memory_space=pltpu.SMEM)
    idx = pl.BlockSpec((N_TOK // LANE, LANE), lambda i, c: (0, 0))
    grid_spec = pltpu.PrefetchScalarGridSpec(
        num_scalar_prefetch=1,
        grid=(N_TOK // TD_DISP,),
        in_specs=[idx, idx, idx, idx,
                  pl.BlockSpec((TD_DISP * TOKEN_ROWS, LANE), lambda i, c: (i, 0))],
        out_specs=[pl.BlockSpec(memory_space=pl.ANY), smem, smem, smem, smem, smem],
        scratch_shapes=[
            pltpu.SMEM((N_EXPERTS,), jnp.int32),
            pltpu.VMEM((BM * TOKEN_ROWS, LANE), F32),
            pltpu.VMEM((TOP_K, N_TOK // LANE, LANE), jnp.int32),
            pltpu.SemaphoreType.DMA,
            pltpu.SemaphoreType.DMA,
            pltpu.SemaphoreType.DMA,
        ],
    )
    return pl.pallas_call(
        _dispatch_kernel,
        grid_spec=grid_spec,
        out_shape=[
            jax.ShapeDtypeStruct((N_SLOTS * TOKEN_ROWS, LANE), F32),
            jax.ShapeDtypeStruct((N_TOK // LANE, LANE), jnp.int32),
            jax.ShapeDtypeStruct((N_TOK // LANE, LANE), jnp.int32),
            jax.ShapeDtypeStruct((N_BLOCKS,), jnp.int32),
            jax.ShapeDtypeStruct((N_EXPERTS,), jnp.int32),
            jax.ShapeDtypeStruct((2,), jnp.int32),
        ],
        compiler_params=pltpu.CompilerParams(
            dimension_semantics=("arbitrary",), vmem_limit_bytes=VMEM_LIMIT),
        name="dispatch",
    )(cnt, e0, e1, r0, r1, x1)


def _expert_kernel(ord_ref, oe_ref, nums_ref, xs_ref, w1_hbm, w3_hbm, w2_hbm, ys_ref,
                   w1f_ref, w3f_ref, w2f_ref, w1b_ref, w3b_ref, w2b_ref, sem):
    i = pl.program_id(0)
    n_used = nums_ref[0]
    n_ord = nums_ref[1]
    active = i < n_used
    k = ord_ref[i]
    first = (i == 0) | (k != ord_ref[jnp.maximum(i - 1, 0)])

    def weight_copies(kk):
        e = oe_ref[kk]
        slot = kk % W_SLOTS
        return [pltpu.make_async_copy(w_hbm.at[e], wf_ref.at[slot], sem.at[slot])
                for w_hbm, wf_ref in ((w1_hbm, w1f_ref), (w3_hbm, w3f_ref), (w2_hbm, w2f_ref))]

    def start_weights(kk):
        @pl.when(kk < n_ord)
        def _():
            for cp in weight_copies(kk):
                cp.start(priority=1)

    @pl.when(active & first)
    def _():
        @pl.when(i == 0)
        def _():
            for ahead in range(W_SLOTS - 1):
                start_weights(k + ahead)

        start_weights(k + W_SLOTS - 1)
        for cp in weight_copies(k):
            cp.wait()
        slot = k % W_SLOTS
        w1b_ref[...] = w1f_ref[slot].astype(BF16)
        w3b_ref[...] = w3f_ref[slot].astype(BF16)
        w2b_ref[...] = w2f_ref[slot].astype(BF16)

    @pl.when(active)
    def _():
        xb = _from_token_tiles(xs_ref, BM).astype(BF16)
        h1 = jnp.dot(xb, w1b_ref[...], preferred_element_type=F32)
        h3 = jnp.dot(xb, w3b_ref[...], preferred_element_type=F32)
        hdn = (h1 * _sigmoid(h1) * h3).astype(BF16)
        _to_token_tiles(ys_ref, jnp.dot(hdn, w2b_ref[...], preferred_element_type=F32))

    @pl.when(jnp.logical_not(active))
    def _():
        ys_ref[...] = jnp.zeros_like(ys_ref)


def _experts(blk_ord, ord_e, nums, xs, w1, w3, w2):
    last = lambda i, nums_ref: jnp.maximum(jnp.minimum(i, nums_ref[0] - 1), 0)
    hbm = pl.BlockSpec(memory_space=pl.ANY)
    grid_spec = pltpu.PrefetchScalarGridSpec(
        num_scalar_prefetch=3,
        grid=(N_BLOCKS,),
        in_specs=[
            pl.BlockSpec((BM * TOKEN_ROWS, LANE), lambda i, o, oe, nu: (last(i, nu), 0)),
            hbm, hbm, hbm,
        ],
        out_specs=pl.BlockSpec((BM * TOKEN_ROWS, LANE), lambda i, o, oe, nu: (i, 0)),
        scratch_shapes=[
            pltpu.VMEM((W_SLOTS, D_MODEL, D_EXPERT), F32),
            pltpu.VMEM((W_SLOTS, D_MODEL, D_EXPERT), F32),
            pltpu.VMEM((W_SLOTS, D_EXPERT, D_MODEL), F32),
            pltpu.VMEM((D_MODEL, D_EXPERT), BF16),
            pltpu.VMEM((D_MODEL, D_EXPERT), BF16),
            pltpu.VMEM((D_EXPERT, D_MODEL), BF16),
            pltpu.SemaphoreType.DMA((W_SLOTS,)),
        ],
    )
    return pl.pallas_call(
        _expert_kernel,
        grid_spec=grid_spec,
        out_shape=jax.ShapeDtypeStruct((N_SLOTS * TOKEN_ROWS, LANE), F32),
        compiler_params=pltpu.CompilerParams(
            dimension_semantics=("arbitrary",), vmem_limit_bytes=VMEM_LIMIT),
        name="experts",
    )(blk_ord, ord_e, nums, xs, w1, w3, w2)


def _combine_kernel(d0_ref, d1_ref, route_ref, x1_ref, ys_ref, g_ref, b_ref, o_ref, y_ref, sem):
    i = pl.program_id(0)
    n_tiles = pl.num_programs(0)
    tile_rows = TC_COMB * TOKEN_ROWS

    def issue(tile, slot, t0, n):
        drow = (tile * TC_COMB + t0) >> LANE_BITS
        for uu in range(n):
            t = t0 + uu
            dst = pl.ds(pl.multiple_of(t * TOKEN_ROWS, TOKEN_ROWS), TOKEN_ROWS)
            for kk, d_ref in enumerate((d0_ref, d1_ref)):
                d = d_ref[drow + uu // LANE, uu % LANE]
                src = pl.ds(pl.multiple_of(d * TOKEN_ROWS, TOKEN_ROWS), TOKEN_ROWS)
                pltpu.make_async_copy(ys_ref.at[src], y_ref.at[slot, kk, dst],
                                      sem.at[slot]).start(priority=kk)

    def finish(slot, t0):
        t0 = pl.multiple_of(t0, COMB_ROWS)
        w0 = route_ref[pl.ds(t0, COMB_ROWS), 4:5]
        w1 = route_ref[pl.ds(t0, COMB_ROWS), 5:6]
        z = (DEEPNORM_ALPHA * _from_token_tiles(x1_ref, COMB_ROWS, t0)
             + w0 * _from_token_tiles(y_ref.at[slot, 0], COMB_ROWS, t0)
             + w1 * _from_token_tiles(y_ref.at[slot, 1], COMB_ROWS, t0))
        o_ref[pl.ds(t0, COMB_ROWS), :] = _layer_norm(z, g_ref[...], b_ref[...])

    @pl.when(i == 0)
    def _():
        def first(gi, carry):
            issue(0, 0, gi * COMB_ROWS, COMB_ROWS)
            return carry

        lax.fori_loop(0, TC_COMB // COMB_ROWS, first, 0)

    slot = i % 2
    for kk in range(TOP_K):
        pltpu.make_async_copy(ys_ref.at[pl.ds(0, tile_rows)], y_ref.at[slot, kk],
                              sem.at[slot]).wait()

    @pl.when(i + 1 < n_tiles)
    def _():
        def both(gi, carry):
            issue(i + 1, 1 - slot, gi * COMB_ROWS, COMB_ROWS)
            finish(slot, gi * COMB_ROWS)
            return carry

        lax.fori_loop(0, TC_COMB // COMB_ROWS, both, 0)

    @pl.when(i + 1 == n_tiles)
    def _():
        def last(gi, carry):
            finish(slot, gi * COMB_ROWS)
            return carry

        lax.fori_loop(0, TC_COMB // COMB_ROWS, last, 0)


def _combine(d0, d1, route, x1, ys, g, b):
    tc = TC_COMB
    grid_spec = pltpu.PrefetchScalarGridSpec(
        num_scalar_prefetch=2,
        grid=(N_TOK // tc,),
        in_specs=[
            pl.BlockSpec((tc, LANE), lambda i, *_: (i, 0)),
            pl.BlockSpec((tc * TOKEN_ROWS, LANE), lambda i, *_: (i, 0)),
            pl.BlockSpec(memory_space=pl.ANY),
            pl.BlockSpec((1, D_MODEL), lambda i, *_: (0, 0)),
            pl.BlockSpec((1, D_MODEL), lambda i, *_: (0, 0)),
        ],
        out_specs=pl.BlockSpec((tc, D_MODEL), lambda i, *_: (i, 0)),
        scratch_shapes=[
            pltpu.VMEM((2, TOP_K, tc * TOKEN_ROWS, LANE), F32),
            pltpu.SemaphoreType.DMA((2,)),
        ],
    )
    return pl.pallas_call(
        _combine_kernel,
        grid_spec=grid_spec,
        out_shape=jax.ShapeDtypeStruct((N_TOK, D_MODEL), F32),
        compiler_params=pltpu.CompilerParams(
            dimension_semantics=("arbitrary",), vmem_limit_bytes=VMEM_LIMIT),
        name="combine",
    )(d0, d1, route, x1, ys, g, b)


def _split_bf16(w):
    hi = w.astype(BF16)
    lo = (w - hi.astype(F32)).astype(BF16)
    return hi, lo


def kernel(x, w_in, b_in, conv_w, conv_b, conv_ln_g, conv_ln_b, w_conv_out, b_conv_out, w_gate_up, b_gate_up, gla_norm_g, w_gla_out, w_out, b_out, ln1_g, ln1_b, w_router_group, b_router_group, w_router_expert, b_router_expert, w1, w3, w2, ln2_g, ln2_b):
    x2 = x.reshape(N_TOK, D_MODEL)
    row = lambda v: v.reshape(1, -1)
    for l in range(w_in.shape[0]):
        f0 = N_PROJ_A
        f1 = f0 + GATE_RANK
        w_l, b_l = w_in[l], b_in[l]
        w_t = jnp.swapaxes(w_l, 0, 1)
        w_a = w_t[:f0].astype(BF16)
        w_b = w_t[f1:].astype(BF16)
        w_f = jnp.pad(w_t[f0:f1], ((0, F_PAD - GATE_RANK), (0, 0))).astype(BF16)
        b_p = row(jnp.concatenate(
            [b_l[:f0], b_l[f1:], b_l[f0:f1], jnp.zeros((F_PAD - GATE_RANK,), F32)]))
        wgu_p = jnp.concatenate(
            [w_gate_up[l], jnp.zeros((F_PAD - GATE_RANK, GLA_DK), F32)], axis=0).astype(BF16)
        cw_p = jnp.repeat(conv_w[l], SUBLANES, axis=0)
        c, q, k, v, rs, sga, sgb, la = _inproj(
            x2, w_a, w_b, w_f, b_p, wgu_p, row(b_gate_up[l]), cw_p,
            row(conv_b[l]), row(conv_ln_g[l]), row(conv_ln_b[l]))
        g = _gla(q, k, v, la, rs, row(gla_norm_g[l]))

        w_r = jnp.concatenate(
            [w_router_expert[l], w_router_group[l],
             jnp.zeros((D_MODEL, LANE - N_EXPERTS - N_GROUPS), F32)], axis=1)
        b_r = row(jnp.concatenate(
            [b_router_expert[l], b_router_group[l],
             jnp.zeros((LANE - N_EXPERTS - N_GROUPS,), F32)]))
        wrh, wrl = _split_bf16(w_r)
        ri = lax.broadcasted_iota(jnp.int32, (TM_SUB, TM_SUB), 0)
        ci = lax.broadcasted_iota(jnp.int32, (TM_SUB, TM_SUB), 1)
        tri = (ri > ci).astype(BF16)
        x1, route, meta, cnt = _mix(
            c, g, sga, sgb, x2, w_conv_out[l].astype(BF16), row(b_conv_out[l]),
            w_gla_out[l].astype(BF16), w_out[l].astype(BF16), row(b_out[l]),
            row(ln1_g[l]), row(ln1_b[l]), wrh, wrl, b_r, tri)

        cnt_i = cnt[0, :N_EXPERTS].astype(jnp.int32)
        tok_grid = lambda m: m.reshape(N_TOK // LANE, LANE)
        xs, d0, d1, blk_ord, ord_e, nums = _dispatch(
            tok_grid(meta[0]), tok_grid(meta[1]), tok_grid(meta[2]), tok_grid(meta[3]), cnt_i, x1)
        ys = _experts(blk_ord, ord_e, nums, xs, w1[l], w3[l], w2[l])
        x2 = _combine(d0, d1, route, x1, ys, row(ln2_g[l]), row(ln2_b[l]))
    return x2.reshape(x.shape)
```
